```python
import jax
import jax.numpy as jnp
from jax import lax
import numpy as np

D_MODEL = 2048
BATCH = 2
SEQ = 16384
DEPTH = 2

HEAD_DIM = 64
BRANCH_DIM = D_MODEL // 2
RWKV_HEADS = BRANCH_DIM // HEAD_DIM
DECAY_LORA = 96
ICLR_LORA = 96
GATE_LORA = 256
RWKV_COLS = 3 * BRANCH_DIM + DECAY_LORA + ICLR_LORA + GATE_LORA
POOL_WINDOWS = (2, 4, 8, 16)
POOL_GROUPS = len(POOL_WINDOWS)
POOL_GROUP_DIM = BRANCH_DIM // POOL_GROUPS
ATTN_Q_HEADS = BRANCH_DIM // HEAD_DIM
ATTN_KV_HEADS = 4
ATTN_GROUP = ATTN_Q_HEADS // ATTN_KV_HEADS
ATTN_KV_DIM = ATTN_KV_HEADS * HEAD_DIM
WINDOW = 128
ROPE_THETA = 10000.0
N_BRANCHES = 3
IN_COLS = RWKV_COLS + BRANCH_DIM + BRANCH_DIM + 2 * ATTN_KV_DIM + N_BRANCHES * D_MODEL
N_GROUPS = 4
EXPERTS_PER_GROUP = 8
N_EXPERTS = N_GROUPS * EXPERTS_PER_GROUP
TOP_K = 2
EXPERT_FF = 512
EXPERT_BLOCK = 128
NORM_EPS = 1e-6
GN_EPS = 64e-5

kernel_name = 'hybrid_rwkv7_pool_swa_hiermoe'


def rms_norm(x, g):
    xf = x.astype(jnp.float32)
    y = xf * lax.rsqrt(jnp.mean(xf * xf, axis=-1, keepdims=True) + NORM_EPS)
    return (y * g.astype(jnp.float32)).astype(x.dtype)


def token_shift(z, mu):
    z_prev = jnp.pad(z, ((0, 0), (1, 0), (0, 0)))[:, :-1]
    return z + (z_prev - z) * mu


def rwkv7_time_mix(z, w0, w_up, a0, a_up, g_up, k_k, k_a, r_k, ln_g, ln_b):
    B, S, _ = z.shape
    H, N, c = RWKV_HEADS, HEAD_DIM, BRANCH_DIM
    r, k, v, wd, ad, gd = jnp.split(
        z, [c, 2 * c, 3 * c, 3 * c + DECAY_LORA, 3 * c + DECAY_LORA + ICLR_LORA], axis=-1)
    w_log = -jax.nn.softplus(-(w0 + jnp.tanh(wd) @ w_up)) - 0.5
    decay = jnp.exp(-jnp.exp(w_log))
    a = jax.nn.sigmoid(a0 + ad @ a_up)
    g = jax.nn.sigmoid(gd) @ g_up
    kk = (k * k_k).reshape(B, S, H, N)
    kk = kk * lax.rsqrt(jnp.maximum(jnp.sum(kk * kk, axis=-1, keepdims=True), 1e-24))
    k = k * (1.0 + (a - 1.0) * k_a)
    heads = lambda t: t.reshape(B, S, H, N)
    r, k, v, decay, a = heads(r), heads(k), heads(v), heads(decay), heads(a)
    xs = tuple(jnp.moveaxis(t, 1, 0) for t in (r, decay, k, v, kk, a))

    def step(state, inp):
        r_t, w_t, k_t, v_t, kk_t, a_t = inp
        sa = jnp.einsum('bhvk,bhk->bhv', state, -kk_t)
        state = (state * w_t[:, :, None, :]
                 + sa[..., None] * (kk_t * a_t)[:, :, None, :]
                 + v_t[..., None] * k_t[:, :, None, :])
        return state, jnp.einsum('bhvk,bhk->bhv', state, r_t)

    state0 = jnp.zeros((B, H, N, N), jnp.float32)
    _, y = lax.scan(step, state0, xs)
    y = jnp.moveaxis(y, 0, 1)
    mean = jnp.mean(y, axis=-1, keepdims=True)
    var = jnp.mean(jnp.square(y - mean), axis=-1, keepdims=True)
    y = ((y - mean) * lax.rsqrt(var + GN_EPS)).reshape(B, S, c) * ln_g + ln_b
    bonus = jnp.sum(r * k * r_k, axis=-1, keepdims=True) * v
    return (y + bonus.reshape(B, S, c)) * g


def multiscale_pool(u, pool_w, pool_scale):
    B, S, _ = u.shape
    uf = u.astype(jnp.float32)
    cs = jnp.cumsum(uf, axis=1)
    t = jnp.arange(S)
    outs = []
    for gi, m in enumerate(POOL_WINDOWS):
        sl = slice(gi * POOL_GROUP_DIM, (gi + 1) * POOL_GROUP_DIM)
        c = cs[..., sl]
        lag = jnp.pad(c, ((0, 0), (m, 0), (0, 0)))[:, :S]
        count = jnp.minimum(t + 1, m).astype(jnp.float32)[None, :, None]
        pooled = (c - lag) / count - uf[..., sl]
        outs.append(pooled @ pool_w[gi])
    return jnp.concatenate(outs, axis=-1) * pool_scale


def rope(x, positions):
    half = HEAD_DIM // 2
    inv_freq = ROPE_THETA ** (-jnp.arange(half, dtype=jnp.float32) / half)
    ang = positions.astype(jnp.float32)[..., None] * inv_freq
    cos = jnp.cos(ang)[:, :, None, :]
    sin = jnp.sin(ang)[:, :, None, :]
    xf = x.astype(jnp.float32)
    x1, x2 = xf[..., :half], xf[..., half:]
    return jnp.concatenate([x1 * cos - x2 * sin, x2 * cos + x1 * sin], axis=-1).astype(x.dtype)


def sliding_window_attention(q, k, v, sinks):
    B, S, _, _ = q.shape
    nb = S // WINDOW
    qb = q.reshape(B, nb, WINDOW, ATTN_KV_HEADS, ATTN_GROUP, HEAD_DIM)

    def band(t):
        t = t.reshape(B, nb, WINDOW, ATTN_KV_HEADS, HEAD_DIM)
        prev = jnp.pad(t, ((0, 0), (1, 0), (0, 0), (0, 0), (0, 0)))[:, :-1]
        return jnp.concatenate([prev, t], axis=2)

    kb, vb = band(k), band(v)
    s = jnp.einsum('bnqhgd,bnkhd->bnhgqk', qb, kb).astype(jnp.float32) * (HEAD_DIM ** -0.5)
    qi = jnp.arange(WINDOW)[:, None]
    kj = jnp.arange(2 * WINDOW)[None, :]
    rel = kj - qi
    in_window = (rel >= 1) & (rel <= WINDOW)
    valid = in_window[None] & ((jnp.arange(nb)[:, None, None] > 0) | (kj >= WINDOW)[None])
    s = jnp.where(valid[None, :, None, None], s, -jnp.inf)
    sink = sinks.astype(jnp.float32).reshape(ATTN_KV_HEADS, ATTN_GROUP)[None, None, :, :, None, None]
    m = jnp.maximum(jnp.max(s, axis=-1, keepdims=True), sink)
    e = jnp.exp(s - m)
    p = e / (jnp.sum(e, axis=-1, keepdims=True) + jnp.exp(sink - m))
    o = jnp.einsum('bnhgqk,bnkhd->bnqhgd', p.astype(v.dtype), vb)
    return o.reshape(B, S, ATTN_Q_HEADS * HEAD_DIM)


def hier_moe(h, w_grp, b_grp, w_exp, b_exp, w1, w3, w2):
    B, S, D = h.shape
    n = B * S
    hf = h.reshape(n, D)
    grp_logits = (hf @ w_grp).astype(jnp.float32) + b_grp.astype(jnp.float32)
    grp_prob = jax.nn.softmax(grp_logits, axis=-1)
    grp = jnp.argmax(grp_logits, axis=-1).astype(jnp.int32)
    p_grp = jnp.take_along_axis(grp_prob, grp[:, None], axis=1)
    exp_logits = ((hf @ w_exp).astype(jnp.float32) + b_exp.astype(jnp.float32)).reshape(
        n, N_GROUPS, EXPERTS_PER_GROUP)
    in_grp = jnp.take_along_axis(exp_logits, grp[:, None, None], axis=1)[:, 0]
    top_p, top_i = lax.top_k(jax.nn.softmax(in_grp, axis=-1), TOP_K)
    gate = p_grp * top_p / jnp.sum(top_p, axis=-1, keepdims=True)
    expert = (grp[:, None] * EXPERTS_PER_GROUP + top_i).reshape(-1).astype(jnp.int32)
    token = jnp.repeat(jnp.arange(n, dtype=jnp.int32), TOP_K)
    order = jnp.argsort(expert)
    e_sorted = expert[order]
    counts = jnp.zeros((N_EXPERTS,), jnp.int32).at[expert].add(1)
    padded = (counts + EXPERT_BLOCK - 1) // EXPERT_BLOCK * EXPERT_BLOCK
    starts = jnp.cumsum(counts) - counts
    pends = jnp.cumsum(padded)
    pstarts = pends - padded
    dest_sorted = pstarts[e_sorted] + jnp.arange(n * TOP_K, dtype=jnp.int32) - starts[e_sorted]
    n_rows = -(-(n * TOP_K) // EXPERT_BLOCK) * EXPERT_BLOCK + N_EXPERTS * EXPERT_BLOCK
    n_blocks = n_rows // EXPERT_BLOCK
    row_token = jnp.full((n_rows,), n, jnp.int32).at[dest_sorted].set(token[order])
    h_pad = jnp.concatenate([hf, jnp.zeros((1, D), hf.dtype)], axis=0)
    xg = h_pad[row_token].reshape(n_blocks, EXPERT_BLOCK, D)
    block_start = jnp.arange(n_blocks, dtype=jnp.int32) * EXPERT_BLOCK
    block_expert = jnp.minimum(jnp.searchsorted(pends, block_start, side='right'), N_EXPERTS - 1)

    def expert_block(args):
        xb, e = args
        return (jax.nn.silu(xb @ w1[e]) * (xb @ w3[e])) @ w2[e]

    yg = lax.map(expert_block, (xg, block_expert)).reshape(n_rows, D)
    dest = jnp.zeros((n * TOP_K,), jnp.int32).at[order].set(dest_sorted).reshape(n, TOP_K)
    y = jnp.einsum('nk,nkd->nd', gate.astype(h.dtype), yg[dest])
    return y.reshape(B, S, D)


def setup_inputs(seed: int = 0) -> dict:
    key = jax.random.key(seed)
    ks = jax.random.split(key, 32)
    L, D, c = DEPTH, D_MODEL, BRANCH_DIM
    nrm = lambda k, shape, scale: scale * jax.random.normal(k, shape, jnp.float32)
    return {
        'x': nrm(ks[0], (BATCH, SEQ, D), 1.0),
        'positions': jnp.broadcast_to(jnp.arange(SEQ, dtype=jnp.int32)[None, :], (BATCH, SEQ)),
        'attn_norm': 1.0 + nrm(ks[1], (L, D), 0.05),
        'w_in': nrm(ks[2], (L, D, IN_COLS), D ** -0.5),
        'tmix_mu': jax.random.uniform(ks[3], (L, RWKV_COLS), jnp.float32),
        'rwkv_w0': -2.0 + nrm(ks[4], (L, c), 1.0),
        'rwkv_w_up': nrm(ks[5], (L, DECAY_LORA, c), DECAY_LORA ** -0.5),
        'rwkv_a0': nrm(ks[6], (L, c), 0.5),
        'rwkv_a_up': nrm(ks[7], (L, ICLR_LORA, c), ICLR_LORA ** -0.5),
        'rwkv_g_up': nrm(ks[8], (L, GATE_LORA, c), GATE_LORA ** -0.5),
        'rwkv_k_k': 0.85 + nrm(ks[9], (L, c), 0.05),
        'rwkv_k_a': 1.0 + nrm(ks[10], (L, c), 0.05),
        'rwkv_r_k': nrm(ks[11], (L, RWKV_HEADS, HEAD_DIM), 0.1),
        'rwkv_ln_g': 1.0 + nrm(ks[12], (L, c), 0.05),
        'rwkv_ln_b': nrm(ks[13], (L, c), 0.02),
        'pool_w': nrm(ks[14], (L, POOL_GROUPS, POOL_GROUP_DIM, POOL_GROUP_DIM), POOL_GROUP_DIM ** -0.5),
        'pool_scale': 1.0 + nrm(ks[15], (L, c), 0.1),
        'q_norm': 1.0 + nrm(ks[16], (L, HEAD_DIM), 0.05),
        'k_norm': 1.0 + nrm(ks[17], (L, HEAD_DIM), 0.05),
        'attn_sinks': nrm(ks[18], (L, ATTN_Q_HEADS), 1.0),
        'w_branch': nrm(ks[19], (L, N_BRANCHES, c, D), c ** -0.5),
        'w_out': nrm(ks[20], (L, D, D), D ** -0.5),
        'ffn_norm': 1.0 + nrm(ks[21], (L, D), 0.05),
        'router_grp_w': nrm(ks[22], (L, D, N_GROUPS), D ** -0.5),
        'router_grp_b': nrm(ks[23], (L, N_GROUPS), 0.01),
        'router_exp_w': nrm(ks[24], (L, D, N_EXPERTS), D ** -0.5),
        'router_exp_b': nrm(ks[25], (L, N_EXPERTS), 0.01),
        'expert_w1': nrm(ks[26], (L, N_EXPERTS, D, EXPERT_FF), D ** -0.5),
        'expert_w3': nrm(ks[27], (L, N_EXPERTS, D, EXPERT_FF), D ** -0.5),
        'expert_w2': nrm(ks[28], (L, N_EXPERTS, EXPERT_FF, D), EXPERT_FF ** -0.5),
    }


def reference(x, positions, attn_norm, w_in, tmix_mu, rwkv_w0, rwkv_w_up, rwkv_a0, rwkv_a_up,
              rwkv_g_up, rwkv_k_k, rwkv_k_a, rwkv_r_k, rwkv_ln_g, rwkv_ln_b, pool_w, pool_scale,
              q_norm, k_norm, attn_sinks, w_branch, w_out, ffn_norm, router_grp_w, router_grp_b,
              router_exp_w, router_exp_b, expert_w1, expert_w3, expert_w2):
    B, S, _ = x.shape
    c = BRANCH_DIM
    splits = np.cumsum([RWKV_COLS, c, c, ATTN_KV_DIM, ATTN_KV_DIM]).tolist()
    for l in range(DEPTH):
        h = rms_norm(x, attn_norm[l])
        proj = h @ w_in[l]
        z, u, q, k, v, gate_logits = jnp.split(proj, splits, axis=-1)
        z = token_shift(z.astype(jnp.float32), tmix_mu[l])
        a_out = rwkv7_time_mix(z, rwkv_w0[l], rwkv_w_up[l], rwkv_a0[l], rwkv_a_up[l], rwkv_g_up[l],
                               rwkv_k_k[l], rwkv_k_a[l], rwkv_r_k[l], rwkv_ln_g[l], rwkv_ln_b[l])
        b_out = multiscale_pool(u, pool_w[l], pool_scale[l])
        q = rope(rms_norm(q.reshape(B, S, ATTN_Q_HEADS, HEAD_DIM), q_norm[l]), positions)
        k = rope(rms_norm(k.reshape(B, S, ATTN_KV_HEADS, HEAD_DIM), k_norm[l]), positions)
        v = v.reshape(B, S, ATTN_KV_HEADS, HEAD_DIM)
        c_out = sliding_window_attention(q, k, v, attn_sinks[l])
        gates = jax.nn.sigmoid(gate_logits.astype(jnp.float32).reshape(B, S, N_BRANCHES, D_MODEL))
        merged = (gates[:, :, 0] * (a_out.astype(x.dtype) @ w_branch[l, 0])
                  + gates[:, :, 1] * (b_out.astype(x.dtype) @ w_branch[l, 1])
                  + gates[:, :, 2] * (c_out @ w_branch[l, 2]))
        x = x + merged.astype(x.dtype) @ w_out[l]
        x = x + hier_moe(rms_norm(x, ffn_norm[l]), router_grp_w[l], router_grp_b[l], router_exp_w[l],
                         router_exp_b[l], expert_w1[l], expert_w3[l], expert_w2[l])
    return x
```

```python
import functools

import numpy as np
import jax
import jax.numpy as jnp
from jax import lax
from jax.experimental import pallas as pl
from jax.experimental.pallas import tpu as pltpu

F32 = jnp.float32
BF16 = jnp.bfloat16
HI = lax.Precision.HIGHEST

D_MODEL = 2048
HEAD_DIM = 64
BRANCH_DIM = D_MODEL // 2
DECAY_LORA = 96
ICLR_LORA = 96
GATE_LORA = 256
LORA_PAD = 128
POOL_WINDOWS = (2, 4, 8, 16)
POOL_GROUP_DIM = BRANCH_DIM // len(POOL_WINDOWS)
POOL_HALO = 16
ATTN_Q_HEADS = BRANCH_DIM // HEAD_DIM
ATTN_KV_HEADS = 4
ATTN_KV_DIM = ATTN_KV_HEADS * HEAD_DIM
WINDOW = 128
ROPE_THETA = 10000.0
N_BRANCHES = 3
N_GROUPS = 4
EXPERTS_PER_GROUP = 8
N_EXPERTS = N_GROUPS * EXPERTS_PER_GROUP
TOP_K = 2
EXPERT_FF = 512
NORM_EPS = 1e-6
GN_EPS = 64e-5

LANES = 128
PAIR = 2 * HEAD_DIM
CHUNK = 64
ROUTER_ROWS = 40
MOE_BLOCK = 256

Z_COLS = 3 * BRANCH_DIM + 2 * LORA_PAD + GATE_LORA
COL_U = Z_COLS
COL_KV = COL_U + BRANCH_DIM
COL_Q = COL_KV + 2 * ATTN_KV_DIM
COL_G = COL_Q + BRANCH_DIM
P_COLS = COL_G + N_BRANCHES * D_MODEL

VMEM_LIMIT = 56 * 1024 * 1024


def _cp(*sem):
    return pltpu.CompilerParams(dimension_semantics=sem, vmem_limit_bytes=VMEM_LIMIT)


def _dot(a, b, precision=None):
    return jnp.dot(a, b, preferred_element_type=F32, precision=precision)


def _dot_nt(a, b, precision=None):
    return lax.dot_general(a, b, (((1,), (1,)), ((), ())), preferred_element_type=F32, precision=precision)


def _sigmoid(x):
    return 1.0 / (1.0 + jnp.exp(-x))


def _head_block_diag(scale):
    r = lax.broadcasted_iota(jnp.int32, (PAIR, PAIR), 0) // HEAD_DIM
    c = lax.broadcasted_iota(jnp.int32, (PAIR, PAIR), 1) // HEAD_DIM
    return jnp.where(r == c, scale, 0.0).astype(F32)


def _norm_matmul_kernel(x_ref, g_ref, w_ref, o_ref, h_ref):
    @pl.when(pl.program_id(1) == 0)
    def _():
        x = x_ref[...]
        ms = jnp.mean(x * x, axis=-1, keepdims=True)
        h_ref[...] = (x * lax.rsqrt(ms + NORM_EPS) * g_ref[...]).astype(BF16)

    o_ref[...] = _dot(h_ref[...], w_ref[...])


def _norm_matmul(x2d, gain, w_bf16, tm, tn):
    n, d = x2d.shape
    cols = w_bf16.shape[1]
    return pl.pallas_call(
        _norm_matmul_kernel,
        grid=(n // tm, cols // tn),
        in_specs=[
            pl.BlockSpec((tm, d), lambda i, j: (i, 0)),
            pl.BlockSpec((1, d), lambda i, j: (0, 0)),
            pl.BlockSpec((d, tn), lambda i, j: (0, j)),
        ],
        out_specs=pl.BlockSpec((tm, tn), lambda i, j: (i, j)),
        out_shape=jax.ShapeDtypeStruct((n, cols), F32),
        scratch_shapes=[pltpu.VMEM((tm, d), BF16)],
        compiler_params=_cp("arbitrary", "arbitrary"),
        name="norm_matmul",
    )(x2d, gain.reshape(1, d), w_bf16)


def _stack_heads(x, lane_is_a):
    return jnp.concatenate([jnp.where(lane_is_a, x, 0.0), jnp.where(lane_is_a, 0.0, x)], axis=0)


def _rwkv_kernel(zr_ref, zk_ref, zv_ref, zl_ref, mur_ref, muk_ref, muv_ref, mul_ref,
                 w0_ref, a0_ref, kkg_ref, ka_ref, rk_ref, lng_ref, lnb_ref,
                 wup_ref, aup_ref, gup_ref, o_ref,
                 h_ref, cr_ref, ck_ref, cv_ref, cl_ref, *, precision):
    t_rows = zr_ref.shape[1]

    @pl.when(pl.program_id(2) == 0)
    def _():
        h_ref[...] = jnp.zeros_like(h_ref)
        cr_ref[...] = jnp.zeros_like(cr_ref)
        ck_ref[...] = jnp.zeros_like(ck_ref)
        cv_ref[...] = jnp.zeros_like(cv_ref)
        cl_ref[...] = jnp.zeros_like(cl_ref)

    row = lax.broadcasted_iota(jnp.int32, (t_rows, 1), 0)

    def shifted(z_ref, carry_ref, mu_ref):
        z = z_ref[0]
        prev = jnp.where(row == 0, carry_ref[...], pltpu.roll(z, 1, 0))
        carry_ref[...] = z_ref[0, t_rows - 1:t_rows, :]
        return z + (prev - z) * mu_ref[...]

    r = shifted(zr_ref, cr_ref, mur_ref)
    k = shifted(zk_ref, ck_ref, muk_ref)
    v = shifted(zv_ref, cv_ref, muv_ref)
    zl = shifted(zl_ref, cl_ref, mul_ref)
    wd = zl[:, 0:LORA_PAD]
    ad = zl[:, LORA_PAD:2 * LORA_PAD]
    gd = zl[:, 2 * LORA_PAD:]

    head_sum = _head_block_diag(1.0)
    head_avg = _head_block_diag(1.0 / HEAD_DIM)

    w_pre = w0_ref[...] + _dot(jnp.tanh(wd), wup_ref[...], HI)
    neg = -w_pre
    softplus = jnp.maximum(neg, 0.0) + jnp.log(1.0 + jnp.exp(-jnp.abs(neg)))
    log_decay = -jnp.exp(-softplus - 0.5)
    a = _sigmoid(a0_ref[...] + _dot(ad, aup_ref[...], HI))
    gate = _dot(_sigmoid(gd), gup_ref[...], HI)
    kk = k * kkg_ref[...]
    kk = kk * lax.rsqrt(jnp.maximum(_dot(kk * kk, head_sum, HI), 1e-24))
    k = k * (1.0 + (a - 1.0) * ka_ref[...])
    bonus = _dot(r * k * rk_ref[...], head_sum, HI) * v

    c = CHUNK
    ri = lax.broadcasted_iota(jnp.int32, (c, c), 0)
    ci = lax.broadcasted_iota(jnp.int32, (c, c), 1)
    tri_incl = (ci <= ri).astype(F32)
    lane_is_a = lax.broadcasted_iota(jnp.int32, (1, PAIR), 1) < HEAD_DIM
    rs = lax.broadcasted_iota(jnp.int32, (PAIR, PAIR), 0)
    cs = lax.broadcasted_iota(jnp.int32, (PAIR, PAIR), 1)
    strict = (cs % c) < (rs % c)
    incl = (cs % c) <= (rs % c)
    eye = rs == cs
    eye_f = eye.astype(F32)
    dot = functools.partial(_dot, precision=precision)

    h = h_ref[...]
    for ch in range(t_rows // c):
        sl = slice(ch * c, (ch + 1) * c)
        lw = log_decay[sl]
        cum = _dot(tri_incl, lw, HI)
        tot = cum[c - 1:c, :]
        p_in = jnp.exp(cum)
        p_ex = jnp.exp(cum - lw)
        q_inv = jnp.exp(-cum)
        q_end = jnp.exp(tot - cum)
        kk_c, a_c, k_c = kk[sl], a[sl], k[sl]
        beta = kk_c * a_c
        at_s = _stack_heads(-kk_c * p_ex, lane_is_a)
        rt_s = _stack_heads(r[sl] * p_in, lane_is_a)
        bh_s = _stack_heads(beta * q_inv, lane_is_a)
        kh_s = _stack_heads(k_c * q_inv, lane_is_a)
        be_s = _stack_heads(beta * q_end, lane_is_a)
        ke_s = _stack_heads(k_c * q_end, lane_is_a)
        v_s = _stack_heads(v[sl], lane_is_a)

        lhs = jnp.concatenate([at_s, rt_s], axis=0)
        rhs_t = jnp.concatenate([bh_s, kh_s], axis=0)
        gm = _dot_nt(lhs, rhs_t, precision)
        a_ab = jnp.where(strict, gm[0:PAIR, 0:PAIR], 0.0)
        a_ak = jnp.where(strict, gm[0:PAIR, PAIR:], 0.0)
        a_rb = jnp.where(incl, gm[PAIR:, 0:PAIR], 0.0)
        a_rk = jnp.where(incl, gm[PAIR:, PAIR:], 0.0)

        pw = dot(a_ab, a_ab)
        inv = eye_f + a_ab
        inv = inv + dot(inv, pw)
        for _ in range(4):
            pw = dot(pw, pw)
            inv = inv + dot(inv, pw)

        u_s = dot(inv, dot(at_s, h) + dot(a_ak, v_s))
        y_s = dot(rt_s, h) + dot(a_rb, u_s) + dot(a_rk, v_s)
        decay_end = jnp.where(eye, jnp.exp(tot), 0.0)
        h = dot(decay_end, h) + dot(be_s.T, u_s) + dot(ke_s.T, v_s)

        y = y_s[0:c] + y_s[c:]
        mean = _dot(y, head_avg, HI)
        yc = y - mean
        var = _dot(yc * yc, head_avg, HI)
        gn = yc * lax.rsqrt(var + GN_EPS)
        out = (gn * lng_ref[...] + lnb_ref[...] + bonus[sl]) * gate[sl]
        o_ref[0, sl, :] = out.astype(o_ref.dtype)
    h_ref[...] = h


def _rwkv(proj3, mu_pack, w0, a0, k_k, k_a, r_k, ln_g, ln_b, w_up, a_up, g_up, t_rows, precision):
    b, s, _ = proj3.shape
    n_pairs = BRANCH_DIM // PAIR
    vec = lambda off: pl.BlockSpec((1, PAIR), lambda bi, p, c: (0, off + p))
    zcol = lambda off: pl.BlockSpec((1, t_rows, PAIR), lambda bi, p, c: (bi, c, off + p))
    lora_w = 2 * LORA_PAD + GATE_LORA
    lora_blk = (3 * BRANCH_DIM) // lora_w
    row = lambda x: x.reshape(1, BRANCH_DIM)
    kernel = functools.partial(_rwkv_kernel, precision=precision)
    return pl.pallas_call(
        kernel,
        grid=(b, n_pairs, s // t_rows),
        in_specs=[
            zcol(0), zcol(n_pairs), zcol(2 * n_pairs),
            pl.BlockSpec((1, t_rows, lora_w), lambda bi, p, c: (bi, c, lora_blk)),
            vec(0), vec(n_pairs), vec(2 * n_pairs),
            pl.BlockSpec((1, lora_w), lambda bi, p, c: (0, lora_blk)),
            vec(0), vec(0), vec(0), vec(0), vec(0), vec(0), vec(0),
            pl.BlockSpec((LORA_PAD, PAIR), lambda bi, p, c: (0, p)),
            pl.BlockSpec((LORA_PAD, PAIR), lambda bi, p, c: (0, p)),
            pl.BlockSpec((GATE_LORA, PAIR), lambda bi, p, c: (0, p)),
        ],
        out_specs=pl.BlockSpec((1, t_rows, PAIR), lambda bi, p, c: (bi, c, p)),
        out_shape=jax.ShapeDtypeStruct((b, s, BRANCH_DIM), BF16),
        scratch_shapes=[
            pltpu.VMEM((PAIR, PAIR), F32),
            pltpu.VMEM((1, PAIR), F32), pltpu.VMEM((1, PAIR), F32), pltpu.VMEM((1, PAIR), F32),
            pltpu.VMEM((1, lora_w), F32),
        ],
        compiler_params=_cp("arbitrary", "arbitrary", "arbitrary"),
        name="rwkv",
    )(proj3, proj3, proj3, proj3, mu_pack, mu_pack, mu_pack, mu_pack,
      row(w0), row(a0), row(k_k), row(k_a), row(r_k), row(ln_g), row(ln_b), w_up, a_up, g_up)


def _pool_kernel(u0_ref, u1_ref, u2_ref, u3_ref, h0_ref, h1_ref, h2_ref, h3_ref, w_ref, sc_ref, o_ref, buf_ref):
    t_rows = u0_ref.shape[1]
    first = pl.program_id(1) == 0
    t = pl.program_id(1) * t_rows + lax.broadcasted_iota(jnp.int32, (t_rows, 1), 0)
    gd = POOL_GROUP_DIM
    for gi, (m, u_ref, halo_ref) in enumerate(zip(POOL_WINDOWS, (u0_ref, u1_ref, u2_ref, u3_ref),
                                                  (h0_ref, h1_ref, h2_ref, h3_ref))):
        u = u_ref[0]
        buf_ref[0:POOL_HALO, :] = jnp.where(first, 0.0, halo_ref[0])
        buf_ref[POOL_HALO:, :] = u
        acc = u
        for sft in range(1, m):
            acc = acc + buf_ref[pl.ds(POOL_HALO - sft, t_rows), :]
        count = jnp.minimum(t + 1, m).astype(F32)
        pooled = acc / count - u
        y = _dot(pooled.astype(BF16), w_ref[gi])
        o_ref[0, :, gi * gd:(gi + 1) * gd] = (y * sc_ref[:, gi * gd:(gi + 1) * gd]).astype(o_ref.dtype)


def _pool(proj3, pool_w_bf16, pool_scale, t_rows):
    b, s, _ = proj3.shape
    gd = POOL_GROUP_DIM
    ublk = COL_U // gd
    hb = t_rows // POOL_HALO
    u_spec = lambda gi: pl.BlockSpec((1, t_rows, gd), lambda bi, c: (bi, c, ublk + gi))
    h_spec = lambda gi: pl.BlockSpec((1, POOL_HALO, gd), lambda bi, c: (bi, jnp.maximum(c * hb - 1, 0), ublk + gi))
    return pl.pallas_call(
        _pool_kernel,
        grid=(b, s // t_rows),
        in_specs=[u_spec(0), u_spec(1), u_spec(2), u_spec(3), h_spec(0), h_spec(1), h_spec(2), h_spec(3),
                  pl.BlockSpec((len(POOL_WINDOWS), gd, gd), lambda bi, c: (0, 0, 0)),
                  pl.BlockSpec((1, BRANCH_DIM), lambda bi, c: (0, 0))],
        out_specs=pl.BlockSpec((1, t_rows, BRANCH_DIM), lambda bi, c: (bi, c, 0)),
        out_shape=jax.ShapeDtypeStruct((b, s, BRANCH_DIM), BF16),
        scratch_shapes=[pltpu.VMEM((t_rows + POOL_HALO, gd), F32)],
        compiler_params=_cp("parallel", "arbitrary"),
        name="pool",
    )(proj3, proj3, proj3, proj3, proj3, proj3, proj3, proj3, pool_w_bf16, pool_scale.reshape(1, BRANCH_DIM))


def _attn_kernel(sink_ref, q_ref, kv_ref, kvh_ref, pos_ref, posh_ref, invf_ref, qg_ref, kg_ref, o_ref):
    w = WINDOW
    not_first = pl.program_id(1) > 0
    head_avg = _head_block_diag(1.0 / HEAD_DIM)
    lane = lax.broadcasted_iota(jnp.int32, (1, PAIR), 1)
    lane_is_a = lane < HEAD_DIM
    rope_lo = (lane % HEAD_DIM) < (HEAD_DIM // 2)

    def norm_rope(x, cos, sin, gain):
        ms = _dot(x * x, head_avg, HI)
        x = x * lax.rsqrt(ms + NORM_EPS) * gain
        rot = jnp.where(rope_lo, -pltpu.roll(x, PAIR - HEAD_DIM // 2, 1), pltpu.roll(x, HEAD_DIM // 2, 1))
        return x * cos + rot * sin

    ang = pos_ref[0] * invf_ref[...]
    cos, sin = jnp.cos(ang), jnp.sin(ang)
    ang_h = posh_ref[0] * invf_ref[...]
    cos_k = jnp.concatenate([jnp.cos(ang_h), cos], axis=0)
    sin_k = jnp.concatenate([jnp.sin(ang_h), sin], axis=0)

    kv = jnp.concatenate([kvh_ref[0], kv_ref[0]], axis=0)
    qi = lax.broadcasted_iota(jnp.int32, (2 * w, 2 * w), 0) % w
    kj = lax.broadcasted_iota(jnp.int32, (2 * w, 2 * w), 1)
    rel = kj - qi
    valid = (rel >= 1) & (rel <= w) & ((kj >= w) | not_first)
    row_is_a = lax.broadcasted_iota(jnp.int32, (2 * w, 1), 0) < w

    for kb in range(ATTN_KV_DIM // PAIR):
        kn = norm_rope(kv[:, kb * PAIR:(kb + 1) * PAIR], cos_k, sin_k, kg_ref[...])
        vv = kv[:, ATTN_KV_DIM + kb * PAIR:ATTN_KV_DIM + (kb + 1) * PAIR]
        kn_sw = pltpu.roll(kn, HEAD_DIM, 1)
        vv_sw = pltpu.roll(vv, HEAD_DIM, 1)
        for half in range(2):
            g = 2 * kb + half
            if half == 0:
                k2 = jnp.where(lane_is_a, kn, kn_sw)
                v2 = jnp.where(lane_is_a, vv, vv_sw)
            else:
                k2 = jnp.where(lane_is_a, kn_sw, kn)
                v2 = jnp.where(lane_is_a, vv_sw, vv)
            k2 = k2.astype(BF16)
            v2 = v2.astype(BF16)
            for jp in range(2):
                qb = 2 * g + jp
                qn = norm_rope(q_ref[0, :, qb * PAIR:(qb + 1) * PAIR], cos, sin, qg_ref[...])
                qn = qn * (HEAD_DIM ** -0.5)
                qs = _stack_heads(qn, lane_is_a).astype(BF16)
                sc = _dot_nt(qs, k2)
                sc = jnp.where(valid, sc, -jnp.inf)
                sink = jnp.where(row_is_a, sink_ref[2 * qb], sink_ref[2 * qb + 1])
                mx = jnp.maximum(jnp.max(sc, axis=-1, keepdims=True), sink)
                e = jnp.exp(sc - mx)
                p = e / (jnp.sum(e, axis=-1, keepdims=True) + jnp.exp(sink - mx))
                o2 = _dot(p.astype(BF16), v2)
                o_ref[0, :, qb * PAIR:(qb + 1) * PAIR] = jnp.where(lane_is_a, o2[0:w], o2[w:]).astype(o_ref.dtype)


def _attn(proj3, pos_lanes, inv_freq, q_gain, k_gain, sinks):
    b, s, _ = proj3.shape
    w = WINDOW
    kvw = 2 * ATTN_KV_DIM
    return pl.pallas_call(
        _attn_kernel,
        grid=(b, s // w),
        in_specs=[
            pl.BlockSpec(memory_space=pltpu.SMEM),
            pl.BlockSpec((1, w, BRANCH_DIM), lambda bi, c: (bi, c, COL_Q // BRANCH_DIM)),
            pl.BlockSpec((1, w, kvw), lambda bi, c: (bi, c, COL_KV // kvw)),
            pl.BlockSpec((1, w, kvw), lambda bi, c: (bi, jnp.maximum(c - 1, 0), COL_KV // kvw)),
            pl.BlockSpec((1, w, PAIR), lambda bi, c: (bi, c, 0)),
            pl.BlockSpec((1, w, PAIR), lambda bi, c: (bi, jnp.maximum(c - 1, 0), 0)),
            pl.BlockSpec((1, PAIR), lambda bi, c: (0, 0)),
            pl.BlockSpec((1, PAIR), lambda bi, c: (0, 0)),
            pl.BlockSpec((1, PAIR), lambda bi, c: (0, 0)),
        ],
        out_specs=pl.BlockSpec((1, w, BRANCH_DIM), lambda bi, c: (bi, c, 0)),
        out_shape=jax.ShapeDtypeStruct((b, s, BRANCH_DIM), BF16),
        compiler_params=_cp("parallel", "arbitrary"),
        name="attn",
    )(sinks, proj3, proj3, proj3, pos_lanes, pos_lanes, inv_freq, q_gain, k_gain)


def _merge_kernel(a_ref, b_ref, c_ref, g0_ref, g1_ref, g2_ref, w_ref, o_ref):
    acc = _sigmoid(g0_ref[...]) * _dot(a_ref[...], w_ref[0])
    acc = acc + _sigmoid(g1_ref[...]) * _dot(b_ref[...], w_ref[1])
    acc = acc + _sigmoid(g2_ref[...]) * _dot(c_ref[...], w_ref[2])
    o_ref[...] = acc.astype(o_ref.dtype)


def _merge(a_out, b_out, c_out, proj, w_branch_bf16, tm, tn):
    n = proj.shape[0]
    br = lambda: pl.BlockSpec((tm, BRANCH_DIM), lambda i, j: (i, 0))
    gate = lambda g: pl.BlockSpec((tm, tn), lambda i, j: (i, (COL_G + g * D_MODEL) // tn + j))
    return pl.pallas_call(
        _merge_kernel,
        grid=(n // tm, D_MODEL // tn),
        in_specs=[br(), br(), br(), gate(0), gate(1), gate(2),
                  pl.BlockSpec((N_BRANCHES, BRANCH_DIM, tn), lambda i, j: (0, 0, j))],
        out_specs=pl.BlockSpec((tm, tn), lambda i, j: (i, j)),
        out_shape=jax.ShapeDtypeStruct((n, D_MODEL), BF16),
        compiler_params=_cp("parallel", "arbitrary"),
        name="merge",
    )(a_out, b_out, c_out, proj, proj, proj, w_branch_bf16)


def _out_proj_kernel(m_ref, x_ref, w_ref, g_ref, wr_ref, x1_ref, hn_ref, lg_ref):
    x1 = x_ref[...] + _dot(m_ref[...], w_ref[...])
    x1_ref[...] = x1
    ms = jnp.mean(x1 * x1, axis=-1, keepdims=True)
    hn = x1 * lax.rsqrt(ms + NORM_EPS) * g_ref[...]
    hn_ref[...] = hn
    lg_ref[...] = _dot_nt(wr_ref[...], hn, HI)


def _out_proj(merged, x2d, w_out_bf16, ffn_gain, w_router_t, tm):
    n, d = x2d.shape
    return pl.pallas_call(
        _out_proj_kernel,
        grid=(n // tm,),
        in_specs=[
            pl.BlockSpec((tm, d), lambda i: (i, 0)),
            pl.BlockSpec((tm, d), lambda i: (i, 0)),
            pl.BlockSpec((d, d), lambda i: (0, 0)),
            pl.BlockSpec((1, d), lambda i: (0, 0)),
            pl.BlockSpec((ROUTER_ROWS, d), lambda i: (0, 0)),
        ],
        out_specs=[
            pl.BlockSpec((tm, d), lambda i: (i, 0)),
            pl.BlockSpec((tm, d), lambda i: (i, 0)),
            pl.BlockSpec((ROUTER_ROWS, tm), lambda i: (0, i)),
        ],
        out_shape=[
            jax.ShapeDtypeStruct((n, d), F32),
            jax.ShapeDtypeStruct((n, d), F32),
            jax.ShapeDtypeStruct((ROUTER_ROWS, n), F32),
        ],
        compiler_params=_cp("parallel"),
        name="out_proj",
    )(merged, x2d, w_out_bf16, ffn_gain.reshape(1, d), w_router_t)


def _experts_kernel(be_ref, x_ref, w1_ref, w3_ref, w2_ref, o_ref):
    del be_ref
    xb = x_ref[...].astype(BF16)
    h1 = _dot(xb, w1_ref[0])
    h3 = _dot(xb, w3_ref[0])
    act = (h1 * _sigmoid(h1)) * h3
    o_ref[...] = _dot(act.astype(BF16), w2_ref[0])


def _experts(xg, block_expert, w1, w3, w2):
    n_rows, d = xg.shape
    ff = w1.shape[-1]
    blk = MOE_BLOCK
    return pl.pallas_call(
        _experts_kernel,
        grid_spec=pltpu.PrefetchScalarGridSpec(
            num_scalar_prefetch=1,
            grid=(n_rows // blk,),
            in_specs=[
                pl.BlockSpec((blk, d), lambda i, be: (i, 0)),
                pl.BlockSpec((1, d, ff), lambda i, be: (be[i], 0, 0)),
                pl.BlockSpec((1, d, ff), lambda i, be: (be[i], 0, 0)),
                pl.BlockSpec((1, ff, d), lambda i, be: (be[i], 0, 0)),
            ],
            out_specs=pl.BlockSpec((blk, d), lambda i, be: (i, 0)),
        ),
        out_shape=jax.ShapeDtypeStruct((n_rows, d), F32),
        compiler_params=_cp("arbitrary"),
        name="experts",
    )(block_expert, xg, w1, w3, w2)


def _pack_w_in(w):
    d = w.shape[0]
    z32 = jnp.zeros((d, LORA_PAD - DECAY_LORA), w.dtype)
    c = BRANCH_DIM
    o_wd, o_ad, o_gd = 3 * c, 3 * c + DECAY_LORA, 3 * c + DECAY_LORA + ICLR_LORA
    o_u = o_gd + GATE_LORA
    o_q = o_u + c
    o_k = o_q + c
    o_g = o_k + 2 * ATTN_KV_DIM
    parts = [w[:, :o_wd], w[:, o_wd:o_ad], z32, w[:, o_ad:o_gd], z32, w[:, o_gd:o_u],
             w[:, o_u:o_q], w[:, o_k:o_g], w[:, o_q:o_k], w[:, o_g:]]
    return jnp.concatenate(parts, axis=1).astype(BF16)


def _pack_mu(mu):
    z32 = jnp.zeros((LORA_PAD - DECAY_LORA,), mu.dtype)
    c = BRANCH_DIM
    o_wd, o_ad, o_gd = 3 * c, 3 * c + DECAY_LORA, 3 * c + DECAY_LORA + ICLR_LORA
    return jnp.concatenate([mu[:o_wd], mu[o_wd:o_ad], z32, mu[o_ad:o_gd], z32, mu[o_gd:]]).reshape(1, Z_COLS)


def _pad_rows(w, rows):
    return jnp.concatenate([w, jnp.zeros((rows - w.shape[0], w.shape[1]), w.dtype)], axis=0)


def _router_weights(w_grp, w_exp):
    d = w_grp.shape[0]
    pad = jnp.zeros((d, 8 - N_GROUPS), w_grp.dtype)
    return jnp.concatenate([w_grp, pad, w_exp], axis=1).T


def _route(logits_t, b_grp, b_exp, n):
    grp_logits = logits_t[:N_GROUPS].T + b_grp
    exp_logits = (logits_t[8:].T + b_exp).reshape(n, N_GROUPS, EXPERTS_PER_GROUP)
    grp_prob = jax.nn.softmax(grp_logits, axis=-1)
    grp = jnp.argmax(grp_logits, axis=-1).astype(jnp.int32)
    p_grp = jnp.take_along_axis(grp_prob, grp[:, None], axis=1)
    in_grp = jnp.take_along_axis(exp_logits, grp[:, None, None], axis=1)[:, 0]
    top_p, top_i = lax.top_k(jax.nn.softmax(in_grp, axis=-1), TOP_K)
    gate = p_grp * top_p / jnp.sum(top_p, axis=-1, keepdims=True)
    expert = (grp[:, None] * EXPERTS_PER_GROUP + top_i).astype(jnp.int32)
    return expert, gate


def _moe(hn, logits_t, b_grp, b_exp, w1, w3, w2):
    n, d = hn.shape
    blk = MOE_BLOCK
    expert, gate = _route(logits_t, b_grp, b_exp, n)
    flat_e = expert.reshape(-1)
    onehot = (flat_e[:, None] == jnp.arange(N_EXPERTS, dtype=jnp.int32)[None, :]).astype(jnp.int32)
    rank = jnp.take_along_axis(jnp.cumsum(onehot, axis=0) - onehot, flat_e[:, None], axis=1)[:, 0]
    counts = jnp.sum(onehot, axis=0)
    padded = (counts + blk - 1) // blk * blk
    pends = jnp.cumsum(padded)
    pstarts = pends - padded
    dest = pstarts[flat_e] + rank
    n_rows = n * TOP_K + N_EXPERTS * blk
    token = jnp.repeat(jnp.arange(n, dtype=jnp.int32), TOP_K)
    row_token = jnp.full((n_rows,), n, jnp.int32).at[dest].set(token)
    h_pad = jnp.concatenate([hn, jnp.zeros((1, d), hn.dtype)], axis=0)
    xg = h_pad[row_token]
    block_start = jnp.arange(n_rows // blk, dtype=jnp.int32) * blk
    block_expert = jnp.minimum(jnp.searchsorted(pends, block_start, side='right'), N_EXPERTS - 1).astype(jnp.int32)
    yg = _experts(xg, block_expert, w1, w3, w2)
    return jnp.einsum('nk,nkd->nd', gate, yg[dest.reshape(n, TOP_K)])


def _pick(n, pref):
    t = pref
    while n % t:
        t //= 2
    return t


def _layer(x2d, b, s, pos_lanes, inv_freq, p, rwkv_precision):
    n = x2d.shape[0]
    proj = _norm_matmul(x2d, p['attn_norm'], p['w_in'], _pick(n, 1024), 1024)
    proj3 = proj.reshape(b, s, P_COLS)
    a_out = _rwkv(proj3, p['mu'], p['w0'], p['a0'], p['k_k'], p['k_a'], p['r_k'], p['ln_g'], p['ln_b'],
                  p['w_up'], p['a_up'], p['g_up'], _pick(s, 256), rwkv_precision)
    b_out = _pool(proj3, p['pool_w'], p['pool_scale'], _pick(s, 512))
    c_out = _attn(proj3, pos_lanes, inv_freq, p['q_gain'], p['k_gain'], p['sinks'])
    merged = _merge(a_out.reshape(n, -1), b_out.reshape(n, -1), c_out.reshape(n, -1), proj,
                    p['w_branch'], _pick(n, 1024), 512)
    x1, hn, logits_t = _out_proj(merged, x2d, p['w_out'], p['ffn_norm'], p['w_router_t'], _pick(n, 512))
    return x1 + _moe(hn, logits_t, p['b_grp'], p['b_exp'], p['w1'], p['w3'], p['w2'])


def kernel(x, positions, attn_norm, w_in, tmix_mu, rwkv_w0, rwkv_w_up, rwkv_a0, rwkv_a_up, rwkv_g_up, rwkv_k_k, rwkv_k_a, rwkv_r_k, rwkv_ln_g, rwkv_ln_b, pool_w, pool_scale, q_norm, k_norm, attn_sinks, w_branch, w_out, ffn_norm, router_grp_w, router_grp_b, router_exp_w, router_exp_b, expert_w1, expert_w3, expert_w2):
    b, s, d = x.shape
    n = b * s
    half = HEAD_DIM // 2
    inv_freq = ROPE_THETA ** (-jnp.arange(half, dtype=F32) / half)
    inv_freq = jnp.tile(inv_freq, PAIR // half).reshape(1, PAIR)
    pos_lanes = jnp.broadcast_to(positions.astype(F32)[..., None], (b, s, PAIR))
    x2d = x.reshape(n, d)
    for l in range(w_in.shape[0]):
        p = {
            'attn_norm': attn_norm[l],
            'w_in': _pack_w_in(w_in[l]),
            'mu': _pack_mu(tmix_mu[l]),
            'w0': rwkv_w0[l], 'a0': rwkv_a0[l], 'k_k': rwkv_k_k[l], 'k_a': rwkv_k_a[l],
            'r_k': rwkv_r_k[l].reshape(-1), 'ln_g': rwkv_ln_g[l], 'ln_b': rwkv_ln_b[l],
            'w_up': _pad_rows(rwkv_w_up[l], LORA_PAD), 'a_up': _pad_rows(rwkv_a_up[l], LORA_PAD),
            'g_up': rwkv_g_up[l],
            'pool_w': pool_w[l].astype(BF16), 'pool_scale': pool_scale[l],
            'q_gain': jnp.tile(q_norm[l], PAIR // HEAD_DIM).reshape(1, PAIR),
            'k_gain': jnp.tile(k_norm[l], PAIR // HEAD_DIM).reshape(1, PAIR),
            'sinks': attn_sinks[l],
            'w_branch': w_branch[l].astype(BF16), 'w_out': w_out[l].astype(BF16),
            'ffn_norm': ffn_norm[l],
            'w_router_t': _router_weights(router_grp_w[l], router_exp_w[l]),
            'b_grp': router_grp_b[l], 'b_exp': router_exp_b[l],
            'w1': expert_w1[l].astype(BF16), 'w3': expert_w3[l].astype(BF16), 'w2': expert_w2[l].astype(BF16),
        }
        x2d = _layer(x2d, b, s, pos_lanes, inv_freq, p, HI)
    return x2d.reshape(b, s, d)
```

```python
import functools

import numpy as np
import jax
import jax.numpy as jnp
from jax import lax
from jax.experimental import pallas as pl
from jax.experimental.pallas import tpu as pltpu

F32 = jnp.float32
BF16 = jnp.bfloat16
HI = lax.Precision.HIGHEST

D_MODEL = 2048
HEAD_DIM = 64
BRANCH_DIM = D_MODEL // 2
DECAY_LORA = 96
ICLR_LORA = 96
GATE_LORA = 256
LORA_PAD = 128
POOL_WINDOWS = (2, 4, 8, 16)
POOL_GROUP_DIM = BRANCH_DIM // len(POOL_WINDOWS)
POOL_HALO = 16
ATTN_Q_HEADS = BRANCH_DIM // HEAD_DIM
ATTN_KV_HEADS = 4
ATTN_KV_DIM = ATTN_KV_HEADS * HEAD_DIM
WINDOW = 128
ROPE_THETA = 10000.0
N_BRANCHES = 3
N_GROUPS = 4
EXPERTS_PER_GROUP = 8
N_EXPERTS = N_GROUPS * EXPERTS_PER_GROUP
TOP_K = 2
EXPERT_FF = 512
NORM_EPS = 1e-6
GN_EPS = 64e-5

LANES = 128
PAIR = 2 * HEAD_DIM
CHUNK = 64
ROUTER_ROWS = 40
MOE_BLOCK = 256

Z_COLS = 3 * BRANCH_DIM + 2 * LORA_PAD + GATE_LORA
COL_U = Z_COLS
COL_KV = COL_U + BRANCH_DIM
COL_Q = COL_KV + 2 * ATTN_KV_DIM
COL_G = COL_Q + BRANCH_DIM
P_COLS = COL_G + N_BRANCHES * D_MODEL

VMEM_LIMIT = 56 * 1024 * 1024


def _cp(*sem):
    return pltpu.CompilerParams(dimension_semantics=sem, vmem_limit_bytes=VMEM_LIMIT)


def _dot(a, b, precision=None):
    return jnp.dot(a, b, preferred_element_type=F32, precision=precision)


def _dot_nt(a, b, precision=None):
    return lax.dot_general(a, b, (((1,), (1,)), ((), ())), preferred_element_type=F32, precision=precision)


def _sigmoid(x):
    return 1.0 / (1.0 + jnp.exp(-x))


def _head_block_diag(scale):
    r = lax.broadcasted_iota(jnp.int32, (PAIR, PAIR), 0) // HEAD_DIM
    c = lax.broadcasted_iota(jnp.int32, (PAIR, PAIR), 1) // HEAD_DIM
    return jnp.where(r == c, scale, 0.0).astype(F32)


def _norm_matmul_kernel(x_ref, g_ref, w_ref, o_ref, h_ref):
    @pl.when(pl.program_id(1) == 0)
    def _():
        x = x_ref[...]
        ms = jnp.mean(x * x, axis=-1, keepdims=True)
        h_ref[...] = (x * lax.rsqrt(ms + NORM_EPS) * g_ref[...]).astype(BF16)

    o_ref[...] = _dot(h_ref[...], w_ref[...])


def _norm_matmul(x2d, gain, w_bf16, tm, tn):
    n, d = x2d.shape
    cols = w_bf16.shape[1]
    return pl.pallas_call(
        _norm_matmul_kernel,
        grid=(n // tm, cols // tn),
        in_specs=[
            pl.BlockSpec((tm, d), lambda i, j: (i, 0)),
            pl.BlockSpec((1, d), lambda i, j: (0, 0)),
            pl.BlockSpec((d, tn), lambda i, j: (0, j)),
        ],
        out_specs=pl.BlockSpec((tm, tn), lambda i, j: (i, j)),
        out_shape=jax.ShapeDtypeStruct((n, cols), F32),
        scratch_shapes=[pltpu.VMEM((tm, d), BF16)],
        compiler_params=_cp("arbitrary", "arbitrary"),
        name="norm_matmul",
    )(x2d, gain.reshape(1, d), w_bf16)


def _stack_heads(x, lane_is_a):
    return jnp.concatenate([jnp.where(lane_is_a, x, 0.0), jnp.where(lane_is_a, 0.0, x)], axis=0)


def _rwkv_kernel(zr_ref, zk_ref, zv_ref, zl_ref, mur_ref, muk_ref, muv_ref, mul_ref,
                 w0_ref, a0_ref, kkg_ref, ka_ref, rk_ref, lng_ref, lnb_ref,
                 wup_ref, aup_ref, gup_ref, o_ref,
                 h_ref, cr_ref, ck_ref, cv_ref, cl_ref, *, precision):
    t_rows = zr_ref.shape[1]

    @pl.when(pl.program_id(2) == 0)
    def _():
        h_ref[...] = jnp.zeros_like(h_ref)
        cr_ref[...] = jnp.zeros_like(cr_ref)
        ck_ref[...] = jnp.zeros_like(ck_ref)
        cv_ref[...] = jnp.zeros_like(cv_ref)
        cl_ref[...] = jnp.zeros_like(cl_ref)

    row = lax.broadcasted_iota(jnp.int32, (t_rows, 1), 0)

    def shifted(z_ref, carry_ref, mu_ref):
        z = z_ref[0]
        prev = jnp.where(row == 0, carry_ref[...], pltpu.roll(z, 1, 0))
        carry_ref[...] = z_ref[0, t_rows - 1:t_rows, :]
        return z + (prev - z) * mu_ref[...]

    r = shifted(zr_ref, cr_ref, mur_ref)
    k = shifted(zk_ref, ck_ref, muk_ref)
    v = shifted(zv_ref, cv_ref, muv_ref)
    zl = shifted(zl_ref, cl_ref, mul_ref)
    wd = zl[:, 0:LORA_PAD]
    ad = zl[:, LORA_PAD:2 * LORA_PAD]
    gd = zl[:, 2 * LORA_PAD:]

    head_sum = _head_block_diag(1.0)
    head_avg = _head_block_diag(1.0 / HEAD_DIM)

    w_pre = w0_ref[...] + _dot(jnp.tanh(wd), wup_ref[...], HI)
    neg = -w_pre
    softplus = jnp.maximum(neg, 0.0) + jnp.log(1.0 + jnp.exp(-jnp.abs(neg)))
    log_decay = -jnp.exp(-softplus - 0.5)
    a = _sigmoid(a0_ref[...] + _dot(ad, aup_ref[...], HI))
    gate = _dot(_sigmoid(gd), gup_ref[...], HI)
    kk = k * kkg_ref[...]
    kk = kk * lax.rsqrt(jnp.maximum(_dot(kk * kk, head_sum, HI), 1e-24))
    k = k * (1.0 + (a - 1.0) * ka_ref[...])
    bonus = _dot(r * k * rk_ref[...], head_sum, HI) * v

    c = CHUNK
    ri = lax.broadcasted_iota(jnp.int32, (c, c), 0)
    ci = lax.broadcasted_iota(jnp.int32, (c, c), 1)
    tri_incl = (ci <= ri).astype(F32)
    lane_is_a = lax.broadcasted_iota(jnp.int32, (1, PAIR), 1) < HEAD_DIM
    rs = lax.broadcasted_iota(jnp.int32, (PAIR, PAIR), 0)
    cs = lax.broadcasted_iota(jnp.int32, (PAIR, PAIR), 1)
    strict = (cs % c) < (rs % c)
    incl = (cs % c) <= (rs % c)
    eye = rs == cs
    eye_f = eye.astype(F32)
    if precision is None:
        dot = lambda x, y: _dot(x.astype(BF16), y.astype(BF16))
        dot_nt = lambda x, y: _dot_nt(x.astype(BF16), y.astype(BF16))
    else:
        dot = functools.partial(_dot, precision=precision)
        dot_nt = functools.partial(_dot_nt, precision=precision)

    n_ch = t_rows // c
    sls = [slice(ch * c, (ch + 1) * c) for ch in range(n_ch)]
    cums = [_dot(tri_incl, log_decay[sl], HI) for sl in sls]
    pre = []
    for sl, cum in zip(sls, cums):
        lw = log_decay[sl]
        tot = cum[c - 1:c, :]
        p_in = jnp.exp(cum)
        p_ex = jnp.exp(cum - lw)
        q_inv = jnp.exp(-cum)
        q_end = jnp.exp(tot - cum)
        kk_c, a_c, k_c = kk[sl], a[sl], k[sl]
        beta = kk_c * a_c
        pre.append(dict(
            tot=tot,
            at_s=_stack_heads(-kk_c * p_ex, lane_is_a),
            rt_s=_stack_heads(r[sl] * p_in, lane_is_a),
            bh_s=_stack_heads(beta * q_inv, lane_is_a),
            kh_s=_stack_heads(k_c * q_inv, lane_is_a),
            be_s=_stack_heads(beta * q_end, lane_is_a),
            ke_s=_stack_heads(k_c * q_end, lane_is_a),
            v_s=_stack_heads(v[sl], lane_is_a)))
    gms = [dot_nt(jnp.concatenate([d['at_s'], d['rt_s']], axis=0),
                  jnp.concatenate([d['bh_s'], d['kh_s']], axis=0)) for d in pre]
    a_ab = [jnp.where(strict, g[0:PAIR, 0:PAIR], 0.0) for g in gms]
    a_ak = [jnp.where(strict, g[0:PAIR, PAIR:], 0.0) for g in gms]
    a_rb = [jnp.where(incl, g[PAIR:, 0:PAIR], 0.0) for g in gms]
    a_rk = [jnp.where(incl, g[PAIR:, PAIR:], 0.0) for g in gms]
    pw = [dot(x, x) for x in a_ab]
    akv = [dot(x, d['v_s']) for x, d in zip(a_ak, pre)]
    inv = [eye_f + x for x in a_ab]
    inv = [x + dot(x, p) for x, p in zip(inv, pw)]
    for _ in range(4):
        pw = [dot(p, p) for p in pw]
        inv = [x + dot(x, p) for x, p in zip(inv, pw)]
    wu = [dot(x, jnp.concatenate([d['at_s'], y], axis=1)) for x, d, y in zip(inv, pre, akv)]
    rb_wu = [dot(x, y) for x, y in zip(a_rb, wu)]
    be_wu = [dot(d['be_s'].T, y) for d, y in zip(pre, wu)]
    rkv = [dot(x, d['v_s']) for x, d in zip(a_rk, pre)]
    kev = [dot(d['ke_s'].T, d['v_s']) for d in pre]
    chunks = []
    for i in range(n_ch):
        rt2 = pre[i]['rt_s'] + rb_wu[i][:, 0:PAIR]
        y0 = rb_wu[i][:, PAIR:] + rkv[i]
        m = jnp.where(eye, jnp.exp(pre[i]['tot']), 0.0) + be_wu[i][:, 0:PAIR]
        h0 = be_wu[i][:, PAIR:] + kev[i]
        chunks.append((sls[i], rt2, y0, m, h0))

    h = h_ref[...]
    for sl, rt2, y0, m, h0 in chunks:
        y_s = dot(rt2, h) + y0
        h = dot(m, h) + h0

        y = y_s[0:c] + y_s[c:]
        mean = _dot(y, head_avg, HI)
        yc = y - mean
        var = _dot(yc * yc, head_avg, HI)
        gn = yc * lax.rsqrt(var + GN_EPS)
        out = (gn * lng_ref[...] + lnb_ref[...] + bonus[sl]) * gate[sl]
        o_ref[0, sl, :] = out.astype(o_ref.dtype)
    h_ref[...] = h


def _rwkv(proj3, mu_pack, w0, a0, k_k, k_a, r_k, ln_g, ln_b, w_up, a_up, g_up, t_rows, precision):
    b, s, _ = proj3.shape
    n_pairs = BRANCH_DIM // PAIR
    vec = lambda off: pl.BlockSpec((1, PAIR), lambda bi, p, c: (0, off + p))
    zcol = lambda off: pl.BlockSpec((1, t_rows, PAIR), lambda bi, p, c: (bi, c, off + p))
    lora_w = 2 * LORA_PAD + GATE_LORA
    lora_blk = (3 * BRANCH_DIM) // lora_w
    row = lambda x: x.reshape(1, BRANCH_DIM)
    kernel = functools.partial(_rwkv_kernel, precision=precision)
    return pl.pallas_call(
        kernel,
        grid=(b, n_pairs, s // t_rows),
        in_specs=[
            zcol(0), zcol(n_pairs), zcol(2 * n_pairs),
            pl.BlockSpec((1, t_rows, lora_w), lambda bi, p, c: (bi, c, lora_blk)),
            vec(0), vec(n_pairs), vec(2 * n_pairs),
            pl.BlockSpec((1, lora_w), lambda bi, p, c: (0, lora_blk)),
            vec(0), vec(0), vec(0), vec(0), vec(0), vec(0), vec(0),
            pl.BlockSpec((LORA_PAD, PAIR), lambda bi, p, c: (0, p)),
            pl.BlockSpec((LORA_PAD, PAIR), lambda bi, p, c: (0, p)),
            pl.BlockSpec((GATE_LORA, PAIR), lambda bi, p, c: (0, p)),
        ],
        out_specs=pl.BlockSpec((1, t_rows, PAIR), lambda bi, p, c: (bi, c, p)),
        out_shape=jax.ShapeDtypeStruct((b, s, BRANCH_DIM), BF16),
        scratch_shapes=[
            pltpu.VMEM((PAIR, PAIR), F32),
            pltpu.VMEM((1, PAIR), F32), pltpu.VMEM((1, PAIR), F32), pltpu.VMEM((1, PAIR), F32),
            pltpu.VMEM((1, lora_w), F32),
        ],
        compiler_params=_cp("arbitrary", "arbitrary", "arbitrary"),
        name="rwkv",
    )(proj3, proj3, proj3, proj3, mu_pack, mu_pack, mu_pack, mu_pack,
      row(w0), row(a0), row(k_k), row(k_a), row(r_k), row(ln_g), row(ln_b), w_up, a_up, g_up)


def _pool_kernel(u0_ref, u1_ref, u2_ref, u3_ref, h0_ref, h1_ref, h2_ref, h3_ref, w_ref, sc_ref, o_ref, buf_ref):
    t_rows = u0_ref.shape[1]
    first = pl.program_id(1) == 0
    t = pl.program_id(1) * t_rows + lax.broadcasted_iota(jnp.int32, (t_rows, 1), 0)
    gd = POOL_GROUP_DIM
    for gi, (m, u_ref, halo_ref) in enumerate(zip(POOL_WINDOWS, (u0_ref, u1_ref, u2_ref, u3_ref),
                                                  (h0_ref, h1_ref, h2_ref, h3_ref))):
        u = u_ref[0]
        buf_ref[0:POOL_HALO, :] = jnp.where(first, 0.0, halo_ref[0])
        buf_ref[POOL_HALO:, :] = u
        acc = u
        for sft in range(1, m):
            acc = acc + buf_ref[pl.ds(POOL_HALO - sft, t_rows), :]
        count = jnp.minimum(t + 1, m).astype(F32)
        pooled = acc / count - u
        y = _dot(pooled.astype(BF16), w_ref[gi])
        o_ref[0, :, gi * gd:(gi + 1) * gd] = (y * sc_ref[:, gi * gd:(gi + 1) * gd]).astype(o_ref.dtype)


def _pool(proj3, pool_w_bf16, pool_scale, t_rows):
    b, s, _ = proj3.shape
    gd = POOL_GROUP_DIM
    ublk = COL_U // gd
    hb = t_rows // POOL_HALO
    u_spec = lambda gi: pl.BlockSpec((1, t_rows, gd), lambda bi, c: (bi, c, ublk + gi))
    h_spec = lambda gi: pl.BlockSpec((1, POOL_HALO, gd), lambda bi, c: (bi, jnp.maximum(c * hb - 1, 0), ublk + gi))
    return pl.pallas_call(
        _pool_kernel,
        grid=(b, s // t_rows),
        in_specs=[u_spec(0), u_spec(1), u_spec(2), u_spec(3), h_spec(0), h_spec(1), h_spec(2), h_spec(3),
                  pl.BlockSpec((len(POOL_WINDOWS), gd, gd), lambda bi, c: (0, 0, 0)),
                  pl.BlockSpec((1, BRANCH_DIM), lambda bi, c: (0, 0))],
        out_specs=pl.BlockSpec((1, t_rows, BRANCH_DIM), lambda bi, c: (bi, c, 0)),
        out_shape=jax.ShapeDtypeStruct((b, s, BRANCH_DIM), BF16),
        scratch_shapes=[pltpu.VMEM((t_rows + POOL_HALO, gd), F32)],
        compiler_params=_cp("parallel", "arbitrary"),
        name="pool",
    )(proj3, proj3, proj3, proj3, proj3, proj3, proj3, proj3, pool_w_bf16, pool_scale.reshape(1, BRANCH_DIM))


def _attn_kernel(sink_ref, q_ref, kv_ref, kvh_ref, pos_ref, posh_ref, invf_ref, qg_ref, kg_ref, o_ref):
    w = WINDOW
    not_first = pl.program_id(1) > 0
    head_avg = _head_block_diag(1.0 / HEAD_DIM)
    lane = lax.broadcasted_iota(jnp.int32, (1, PAIR), 1)
    lane_is_a = lane < HEAD_DIM
    rope_lo = (lane % HEAD_DIM) < (HEAD_DIM // 2)

    def norm_rope(x, cos, sin, gain):
        ms = _dot(x * x, head_avg, HI)
        x = x * lax.rsqrt(ms + NORM_EPS) * gain
        rot = jnp.where(rope_lo, -pltpu.roll(x, PAIR - HEAD_DIM // 2, 1), pltpu.roll(x, HEAD_DIM // 2, 1))
        return x * cos + rot * sin

    ang = pos_ref[0] * invf_ref[...]
    cos, sin = jnp.cos(ang), jnp.sin(ang)
    ang_h = posh_ref[0] * invf_ref[...]
    cos_k = jnp.concatenate([jnp.cos(ang_h), cos], axis=0)
    sin_k = jnp.concatenate([jnp.sin(ang_h), sin], axis=0)

    kv = jnp.concatenate([kvh_ref[0], kv_ref[0]], axis=0)
    qi = lax.broadcasted_iota(jnp.int32, (2 * w, 2 * w), 0) % w
    kj = lax.broadcasted_iota(jnp.int32, (2 * w, 2 * w), 1)
    rel = kj - qi
    valid = (rel >= 1) & (rel <= w) & ((kj >= w) | not_first)
    row_is_a = lax.broadcasted_iota(jnp.int32, (2 * w, 1), 0) < w

    for kb in range(ATTN_KV_DIM // PAIR):
        kn = norm_rope(kv[:, kb * PAIR:(kb + 1) * PAIR], cos_k, sin_k, kg_ref[...])
        vv = kv[:, ATTN_KV_DIM + kb * PAIR:ATTN_KV_DIM + (kb + 1) * PAIR]
        kn_sw = pltpu.roll(kn, HEAD_DIM, 1)
        vv_sw = pltpu.roll(vv, HEAD_DIM, 1)
        for half in range(2):
            g = 2 * kb + half
            if half == 0:
                k2 = jnp.where(lane_is_a, kn, kn_sw)
                v2 = jnp.where(lane_is_a, vv, vv_sw)
            else:
                k2 = jnp.where(lane_is_a, kn_sw, kn)
                v2 = jnp.where(lane_is_a, vv_sw, vv)
            k2 = k2.astype(BF16)
            v2 = v2.astype(BF16)
            for jp in range(2):
                qb = 2 * g + jp
                qn = norm_rope(q_ref[0, :, qb * PAIR:(qb + 1) * PAIR], cos, sin, qg_ref[...])
                qn = qn * (HEAD_DIM ** -0.5)
                qs = _stack_heads(qn, lane_is_a).astype(BF16)
                sc = _dot_nt(qs, k2)
                sc = jnp.where(valid, sc, -jnp.inf)
                sink = jnp.where(row_is_a, sink_ref[2 * qb], sink_ref[2 * qb + 1])
                mx = jnp.maximum(jnp.max(sc, axis=-1, keepdims=True), sink)
                e = jnp.exp(sc - mx)
                p = e / (jnp.sum(e, axis=-1, keepdims=True) + jnp.exp(sink - mx))
                o2 = _dot(p.astype(BF16), v2)
                o_ref[0, :, qb * PAIR:(qb + 1) * PAIR] = jnp.where(lane_is_a, o2[0:w], o2[w:]).astype(o_ref.dtype)


def _attn(proj3, pos_lanes, inv_freq, q_gain, k_gain, sinks):
    b, s, _ = proj3.shape
    w = WINDOW
    kvw = 2 * ATTN_KV_DIM
    return pl.pallas_call(
        _attn_kernel,
        grid=(b, s // w),
        in_specs=[
            pl.BlockSpec(memory_space=pltpu.SMEM),
            pl.BlockSpec((1, w, BRANCH_DIM), lambda bi, c: (bi, c, COL_Q // BRANCH_DIM)),
            pl.BlockSpec((1, w, kvw), lambda bi, c: (bi, c, COL_KV // kvw)),
            pl.BlockSpec((1, w, kvw), lambda bi, c: (bi, jnp.maximum(c - 1, 0), COL_KV // kvw)),
            pl.BlockSpec((1, w, PAIR), lambda bi, c: (bi, c, 0)),
            pl.BlockSpec((1, w, PAIR), lambda bi, c: (bi, jnp.maximum(c - 1, 0), 0)),
            pl.BlockSpec((1, PAIR), lambda bi, c: (0, 0)),
            pl.BlockSpec((1, PAIR), lambda bi, c: (0, 0)),
            pl.BlockSpec((1, PAIR), lambda bi, c: (0, 0)),
        ],
        out_specs=pl.BlockSpec((1, w, BRANCH_DIM), lambda bi, c: (bi, c, 0)),
        out_shape=jax.ShapeDtypeStruct((b, s, BRANCH_DIM), BF16),
        compiler_params=_cp("parallel", "arbitrary"),
        name="attn",
    )(sinks, proj3, proj3, proj3, pos_lanes, pos_lanes, inv_freq, q_gain, k_gain)


def _merge_kernel(a_ref, b_ref, c_ref, g0_ref, g1_ref, g2_ref, w_ref, o_ref):
    acc = _sigmoid(g0_ref[...]) * _dot(a_ref[...], w_ref[0])
    acc = acc + _sigmoid(g1_ref[...]) * _dot(b_ref[...], w_ref[1])
    acc = acc + _sigmoid(g2_ref[...]) * _dot(c_ref[...], w_ref[2])
    o_ref[...] = acc.astype(o_ref.dtype)


def _merge(a_out, b_out, c_out, proj, w_branch_bf16, tm, tn):
    n = proj.shape[0]
    br = lambda: pl.BlockSpec((tm, BRANCH_DIM), lambda i, j: (i, 0))
    gate = lambda g: pl.BlockSpec((tm, tn), lambda i, j: (i, (COL_G + g * D_MODEL) // tn + j))
    return pl.pallas_call(
        _merge_kernel,
        grid=(n // tm, D_MODEL // tn),
        in_specs=[br(), br(), br(), gate(0), gate(1), gate(2),
                  pl.BlockSpec((N_BRANCHES, BRANCH_DIM, tn), lambda i, j: (0, 0, j))],
        out_specs=pl.BlockSpec((tm, tn), lambda i, j: (i, j)),
        out_shape=jax.ShapeDtypeStruct((n, D_MODEL), BF16),
        compiler_params=_cp("parallel", "arbitrary"),
        name="merge",
    )(a_out, b_out, c_out, proj, proj, proj, w_branch_bf16)


def _out_proj_kernel(m_ref, x_ref, w_ref, g_ref, wr_ref, x1_ref, hn_ref, lg_ref):
    x1 = x_ref[...] + _dot(m_ref[...], w_ref[...])
    x1_ref[...] = x1
    ms = jnp.mean(x1 * x1, axis=-1, keepdims=True)
    hn = x1 * lax.rsqrt(ms + NORM_EPS) * g_ref[...]
    hn_ref[...] = hn
    lg_ref[...] = _dot_nt(wr_ref[...], hn, HI)


def _out_proj(merged, x2d, w_out_bf16, ffn_gain, w_router_t, tm):
    n, d = x2d.shape
    return pl.pallas_call(
        _out_proj_kernel,
        grid=(n // tm,),
        in_specs=[
            pl.BlockSpec((tm, d), lambda i: (i, 0)),
            pl.BlockSpec((tm, d), lambda i: (i, 0)),
            pl.BlockSpec((d, d), lambda i: (0, 0)),
            pl.BlockSpec((1, d), lambda i: (0, 0)),
            pl.BlockSpec((ROUTER_ROWS, d), lambda i: (0, 0)),
        ],
        out_specs=[
            pl.BlockSpec((tm, d), lambda i: (i, 0)),
            pl.BlockSpec((tm, d), lambda i: (i, 0)),
            pl.BlockSpec((ROUTER_ROWS, tm), lambda i: (0, i)),
        ],
        out_shape=[
            jax.ShapeDtypeStruct((n, d), F32),
            jax.ShapeDtypeStruct((n, d), F32),
            jax.ShapeDtypeStruct((ROUTER_ROWS, n), F32),
        ],
        compiler_params=_cp("parallel"),
        name="out_proj",
    )(merged, x2d, w_out_bf16, ffn_gain.reshape(1, d), w_router_t)


def _experts_kernel(be_ref, x_ref, w1_ref, w3_ref, w2_ref, o_ref):
    del be_ref
    xb = x_ref[...].astype(BF16)
    h1 = _dot(xb, w1_ref[0])
    h3 = _dot(xb, w3_ref[0])
    act = (h1 * _sigmoid(h1)) * h3
    o_ref[...] = _dot(act.astype(BF16), w2_ref[0])


def _experts(xg, block_expert, w1, w3, w2):
    n_rows, d = xg.shape
    ff = w1.shape[-1]
    blk = MOE_BLOCK
    return pl.pallas_call(
        _experts_kernel,
        grid_spec=pltpu.PrefetchScalarGridSpec(
            num_scalar_prefetch=1,
            grid=(n_rows // blk,),
            in_specs=[
                pl.BlockSpec((blk, d), lambda i, be: (i, 0)),
                pl.BlockSpec((1, d, ff), lambda i, be: (be[i], 0, 0)),
                pl.BlockSpec((1, d, ff), lambda i, be: (be[i], 0, 0)),
                pl.BlockSpec((1, ff, d), lambda i, be: (be[i], 0, 0)),
            ],
            out_specs=pl.BlockSpec((blk, d), lambda i, be: (i, 0)),
        ),
        out_shape=jax.ShapeDtypeStruct((n_rows, d), F32),
        compiler_params=_cp("arbitrary"),
        name="experts",
    )(block_expert, xg, w1, w3, w2)


def _pack_w_in(w):
    d = w.shape[0]
    z32 = jnp.zeros((d, LORA_PAD - DECAY_LORA), w.dtype)
    c = BRANCH_DIM
    o_wd, o_ad, o_gd = 3 * c, 3 * c + DECAY_LORA, 3 * c + DECAY_LORA + ICLR_LORA
    o_u = o_gd + GATE_LORA
    o_q = o_u + c
    o_k = o_q + c
    o_g = o_k + 2 * ATTN_KV_DIM
    parts = [w[:, :o_wd], w[:, o_wd:o_ad], z32, w[:, o_ad:o_gd], z32, w[:, o_gd:o_u],
             w[:, o_u:o_q], w[:, o_k:o_g], w[:, o_q:o_k], w[:, o_g:]]
    return jnp.concatenate(parts, axis=1).astype(BF16)


def _pack_mu(mu):
    z32 = jnp.zeros((LORA_PAD - DECAY_LORA,), mu.dtype)
    c = BRANCH_DIM
    o_wd, o_ad, o_gd = 3 * c, 3 * c + DECAY_LORA, 3 * c + DECAY_LORA + ICLR_LORA
    return jnp.concatenate([mu[:o_wd], mu[o_wd:o_ad], z32, mu[o_ad:o_gd], z32, mu[o_gd:]]).reshape(1, Z_COLS)


def _pad_rows(w, rows):
    return jnp.concatenate([w, jnp.zeros((rows - w.shape[0], w.shape[1]), w.dtype)], axis=0)


def _router_weights(w_grp, w_exp):
    d = w_grp.shape[0]
    pad = jnp.zeros((d, 8 - N_GROUPS), w_grp.dtype)
    return jnp.concatenate([w_grp, pad, w_exp], axis=1).T


def _route(logits_t, b_grp, b_exp, n):
    grp_logits = logits_t[:N_GROUPS].T + b_grp
    exp_logits = (logits_t[8:].T + b_exp).reshape(n, N_GROUPS, EXPERTS_PER_GROUP)
    grp_prob = jax.nn.softmax(grp_logits, axis=-1)
    grp = jnp.argmax(grp_logits, axis=-1).astype(jnp.int32)
    p_grp = jnp.take_along_axis(grp_prob, grp[:, None], axis=1)
    in_grp = jnp.take_along_axis(exp_logits, grp[:, None, None], axis=1)[:, 0]
    top_p, top_i = lax.top_k(jax.nn.softmax(in_grp, axis=-1), TOP_K)
    gate = p_grp * top_p / jnp.sum(top_p, axis=-1, keepdims=True)
    expert = (grp[:, None] * EXPERTS_PER_GROUP + top_i).astype(jnp.int32)
    return expert, gate


def _moe(hn, logits_t, b_grp, b_exp, w1, w3, w2):
    n, d = hn.shape
    blk = MOE_BLOCK
    expert, gate = _route(logits_t, b_grp, b_exp, n)
    flat_e = expert.reshape(-1)
    onehot = (flat_e[:, None] == jnp.arange(N_EXPERTS, dtype=jnp.int32)[None, :]).astype(jnp.int32)
    rank = jnp.take_along_axis(jnp.cumsum(onehot, axis=0) - onehot, flat_e[:, None], axis=1)[:, 0]
    counts = jnp.sum(onehot, axis=0)
    padded = (counts + blk - 1) // blk * blk
    pends = jnp.cumsum(padded)
    pstarts = pends - padded
    dest = pstarts[flat_e] + rank
    n_rows = n * TOP_K + N_EXPERTS * blk
    token = jnp.repeat(jnp.arange(n, dtype=jnp.int32), TOP_K)
    row_token = jnp.full((n_rows,), n, jnp.int32).at[dest].set(token)
    h_pad = jnp.concatenate([hn, jnp.zeros((1, d), hn.dtype)], axis=0)
    xg = h_pad[row_token]
    block_start = jnp.arange(n_rows // blk, dtype=jnp.int32) * blk
    block_expert = jnp.minimum(jnp.searchsorted(pends, block_start, side='right'), N_EXPERTS - 1).astype(jnp.int32)
    yg = _experts(xg, block_expert, w1, w3, w2)
    return jnp.einsum('nk,nkd->nd', gate, yg[dest.reshape(n, TOP_K)])


def _pick(n, pref):
    t = pref
    while n % t:
        t //= 2
    return t


def _layer(x2d, b, s, pos_lanes, inv_freq, p, rwkv_precision):
    n = x2d.shape[0]
    proj = _norm_matmul(x2d, p['attn_norm'], p['w_in'], _pick(n, 1024), 1024)
    proj3 = proj.reshape(b, s, P_COLS)
    a_out = _rwkv(proj3, p['mu'], p['w0'], p['a0'], p['k_k'], p['k_a'], p['r_k'], p['ln_g'], p['ln_b'],
                  p['w_up'], p['a_up'], p['g_up'], _pick(s, 256), rwkv_precision)
    b_out = _pool(proj3, p['pool_w'], p['pool_scale'], _pick(s, 512))
    c_out = _attn(proj3, pos_lanes, inv_freq, p['q_gain'], p['k_gain'], p['sinks'])
    merged = _merge(a_out.reshape(n, -1), b_out.reshape(n, -1), c_out.reshape(n, -1), proj,
                    p['w_branch'], _pick(n, 1024), 512)
    x1, hn, logits_t = _out_proj(merged, x2d, p['w_out'], p['ffn_norm'], p['w_router_t'], _pick(n, 512))
    return x1 + _moe(hn, logits_t, p['b_grp'], p['b_exp'], p['w1'], p['w3'], p['w2'])


def kernel(x, positions, attn_norm, w_in, tmix_mu, rwkv_w0, rwkv_w_up, rwkv_a0, rwkv_a_up, rwkv_g_up, rwkv_k_k, rwkv_k_a, rwkv_r_k, rwkv_ln_g, rwkv_ln_b, pool_w, pool_scale, q_norm, k_norm, attn_sinks, w_branch, w_out, ffn_norm, router_grp_w, router_grp_b, router_exp_w, router_exp_b, expert_w1, expert_w3, expert_w2):
    b, s, d = x.shape
    n = b * s
    half = HEAD_DIM // 2
    inv_freq = ROPE_THETA ** (-jnp.arange(half, dtype=F32) / half)
    inv_freq = jnp.tile(inv_freq, PAIR // half).reshape(1, PAIR)
    pos_lanes = jnp.broadcast_to(positions.astype(F32)[..., None], (b, s, PAIR))
    x2d = x.reshape(n, d)
    for l in range(w_in.shape[0]):
        p = {
            'attn_norm': attn_norm[l],
            'w_in': _pack_w_in(w_in[l]),
            'mu': _pack_mu(tmix_mu[l]),
            'w0': rwkv_w0[l], 'a0': rwkv_a0[l], 'k_k': rwkv_k_k[l], 'k_a': rwkv_k_a[l],
            'r_k': rwkv_r_k[l].reshape(-1), 'ln_g': rwkv_ln_g[l], 'ln_b': rwkv_ln_b[l],
            'w_up': _pad_rows(rwkv_w_up[l], LORA_PAD), 'a_up': _pad_rows(rwkv_a_up[l], LORA_PAD),
            'g_up': rwkv_g_up[l],
            'pool_w': pool_w[l].astype(BF16), 'pool_scale': pool_scale[l],
            'q_gain': jnp.tile(q_norm[l], PAIR // HEAD_DIM).reshape(1, PAIR),
            'k_gain': jnp.tile(k_norm[l], PAIR // HEAD_DIM).reshape(1, PAIR),
            'sinks': attn_sinks[l],
            'w_branch': w_branch[l].astype(BF16), 'w_out': w_out[l].astype(BF16),
            'ffn_norm': ffn_norm[l],
            'w_router_t': _router_weights(router_grp_w[l], router_exp_w[l]),
            'b_grp': router_grp_b[l], 'b_exp': router_exp_b[l],
            'w1': expert_w1[l].astype(BF16), 'w3': expert_w3[l].astype(BF16), 'w2': expert_w2[l].astype(BF16),
        }
        x2d = _layer(x2d, b, s, pos_lanes, inv_freq, p, None)
    return x2d.reshape(b, s, d)
```

```python
import functools

import numpy as np
import jax
import jax.numpy as jnp
from jax import lax
from jax.experimental import pallas as pl
from jax.experimental.pallas import tpu as pltpu

F32 = jnp.float32
BF16 = jnp.bfloat16
HI = lax.Precision.HIGHEST

D_MODEL = 2048
HEAD_DIM = 64
BRANCH_DIM = D_MODEL // 2
DECAY_LORA = 96
ICLR_LORA = 96
GATE_LORA = 256
LORA_PAD = 128
POOL_WINDOWS = (2, 4, 8, 16)
POOL_GROUP_DIM = BRANCH_DIM // len(POOL_WINDOWS)
POOL_HALO = 16
ATTN_Q_HEADS = BRANCH_DIM // HEAD_DIM
ATTN_KV_HEADS = 4
ATTN_KV_DIM = ATTN_KV_HEADS * HEAD_DIM
WINDOW = 128
ROPE_THETA = 10000.0
N_BRANCHES = 3
N_GROUPS = 4
EXPERTS_PER_GROUP = 8
N_EXPERTS = N_GROUPS * EXPERTS_PER_GROUP
TOP_K = 2
EXPERT_FF = 512
NORM_EPS = 1e-6
GN_EPS = 64e-5

LANES = 128
PAIR = 2 * HEAD_DIM
CHUNK = 64
ROUTER_ROWS = 40
MOE_BLOCK = 256

Z_COLS = 3 * BRANCH_DIM + 2 * LORA_PAD + GATE_LORA
COL_U = Z_COLS
COL_KV = COL_U + BRANCH_DIM
COL_Q = COL_KV + 2 * ATTN_KV_DIM
COL_G = COL_Q + BRANCH_DIM
P_COLS = COL_G + N_BRANCHES * D_MODEL

VMEM_LIMIT = 56 * 1024 * 1024


def _cp(*sem):
    return pltpu.CompilerParams(dimension_semantics=sem, vmem_limit_bytes=VMEM_LIMIT)


def _dot(a, b, precision=None):
    return jnp.dot(a, b, preferred_element_type=F32, precision=precision)


def _dot_nt(a, b, precision=None):
    return lax.dot_general(a, b, (((1,), (1,)), ((), ())), preferred_element_type=F32, precision=precision)


def _sigmoid(x):
    return 1.0 / (1.0 + jnp.exp(-x))


def _head_block_diag(scale):
    r = lax.broadcasted_iota(jnp.int32, (PAIR, PAIR), 0) // HEAD_DIM
    c = lax.broadcasted_iota(jnp.int32, (PAIR, PAIR), 1) // HEAD_DIM
    return jnp.where(r == c, scale, 0.0).astype(F32)


def _norm_matmul_kernel(x_ref, g_ref, w_ref, o_ref, h_ref):
    @pl.when(pl.program_id(1) == 0)
    def _():
        x = x_ref[...]
        ms = jnp.mean(x * x, axis=-1, keepdims=True)
        h_ref[...] = (x * lax.rsqrt(ms + NORM_EPS) * g_ref[...]).astype(BF16)

    o_ref[...] = _dot(h_ref[...], w_ref[...])


def _norm_matmul(x2d, gain, w_bf16, tm, tn):
    n, d = x2d.shape
    cols = w_bf16.shape[1]
    return pl.pallas_call(
        _norm_matmul_kernel,
        grid=(n // tm, cols // tn),
        in_specs=[
            pl.BlockSpec((tm, d), lambda i, j: (i, 0)),
            pl.BlockSpec((1, d), lambda i, j: (0, 0)),
            pl.BlockSpec((d, tn), lambda i, j: (0, j)),
        ],
        out_specs=pl.BlockSpec((tm, tn), lambda i, j: (i, j)),
        out_shape=jax.ShapeDtypeStruct((n, cols), F32),
        scratch_shapes=[pltpu.VMEM((tm, d), BF16)],
        compiler_params=_cp("arbitrary", "arbitrary"),
        name="norm_matmul",
    )(x2d, gain.reshape(1, d), w_bf16)


def _stack_heads(x, lane_is_a):
    return jnp.concatenate([jnp.where(lane_is_a, x, 0.0), jnp.where(lane_is_a, 0.0, x)], axis=0)


def _rwkv_kernel(zr_ref, zk_ref, zv_ref, zl_ref, mur_ref, muk_ref, muv_ref, mul_ref,
                 w0_ref, a0_ref, kkg_ref, ka_ref, rk_ref, lng_ref, lnb_ref,
                 wup_ref, aup_ref, gup_ref, o_ref,
                 h_ref, cr_ref, ck_ref, cv_ref, cl_ref, *, precision):
    t_rows = zr_ref.shape[1]

    @pl.when(pl.program_id(2) == 0)
    def _():
        h_ref[...] = jnp.zeros_like(h_ref)
        cr_ref[...] = jnp.zeros_like(cr_ref)
        ck_ref[...] = jnp.zeros_like(ck_ref)
        cv_ref[...] = jnp.zeros_like(cv_ref)
        cl_ref[...] = jnp.zeros_like(cl_ref)

    row = lax.broadcasted_iota(jnp.int32, (t_rows, 1), 0)

    def shifted(z_ref, carry_ref, mu_ref):
        z = z_ref[0]
        prev = jnp.where(row == 0, carry_ref[...], pltpu.roll(z, 1, 0))
        carry_ref[...] = z_ref[0, t_rows - 1:t_rows, :]
        return z + (prev - z) * mu_ref[...]

    r = shifted(zr_ref, cr_ref, mur_ref)
    k = shifted(zk_ref, ck_ref, muk_ref)
    v = shifted(zv_ref, cv_ref, muv_ref)
    zl = shifted(zl_ref, cl_ref, mul_ref)
    wd = zl[:, 0:LORA_PAD]
    ad = zl[:, LORA_PAD:2 * LORA_PAD]
    gd = zl[:, 2 * LORA_PAD:]

    head_sum = _head_block_diag(1.0)
    head_avg = _head_block_diag(1.0 / HEAD_DIM)

    w_pre = w0_ref[...] + _dot(jnp.tanh(wd), wup_ref[...], HI)
    neg = -w_pre
    softplus = jnp.maximum(neg, 0.0) + jnp.log(1.0 + jnp.exp(-jnp.abs(neg)))
    log_decay = -jnp.exp(-softplus - 0.5)
    a = _sigmoid(a0_ref[...] + _dot(ad, aup_ref[...], HI))
    gate = _dot(_sigmoid(gd), gup_ref[...], HI)
    kk = k * kkg_ref[...]
    kk = kk * lax.rsqrt(jnp.maximum(_dot(kk * kk, head_sum, HI), 1e-24))
    k = k * (1.0 + (a - 1.0) * ka_ref[...])
    bonus = _dot(r * k * rk_ref[...], head_sum, HI) * v

    c = CHUNK
    ri = lax.broadcasted_iota(jnp.int32, (c, c), 0)
    ci = lax.broadcasted_iota(jnp.int32, (c, c), 1)
    tri_incl = (ci <= ri).astype(F32)
    lane_is_a = lax.broadcasted_iota(jnp.int32, (1, PAIR), 1) < HEAD_DIM
    rs = lax.broadcasted_iota(jnp.int32, (PAIR, PAIR), 0)
    cs = lax.broadcasted_iota(jnp.int32, (PAIR, PAIR), 1)
    strict = (cs % c) < (rs % c)
    incl = (cs % c) <= (rs % c)
    eye = rs == cs
    eye_f = eye.astype(F32)
    if precision is None:
        dot = lambda x, y: _dot(x.astype(BF16), y.astype(BF16))
        dot_nt = lambda x, y: _dot_nt(x.astype(BF16), y.astype(BF16))
    else:
        dot = functools.partial(_dot, precision=precision)
        dot_nt = functools.partial(_dot_nt, precision=precision)

    n_ch = t_rows // c
    sls = [slice(ch * c, (ch + 1) * c) for ch in range(n_ch)]
    cums = [_dot(tri_incl, log_decay[sl], HI) for sl in sls]
    pre = []
    for sl, cum in zip(sls, cums):
        lw = log_decay[sl]
        tot = cum[c - 1:c, :]
        p_in = jnp.exp(cum)
        p_ex = jnp.exp(cum - lw)
        q_inv = jnp.exp(-cum)
        q_end = jnp.exp(tot - cum)
        kk_c, a_c, k_c = kk[sl], a[sl], k[sl]
        beta = kk_c * a_c
        pre.append(dict(
            tot=tot,
            at_s=_stack_heads(-kk_c * p_ex, lane_is_a),
            rt_s=_stack_heads(r[sl] * p_in, lane_is_a),
            bh_s=_stack_heads(beta * q_inv, lane_is_a),
            kh_s=_stack_heads(k_c * q_inv, lane_is_a),
            be_s=_stack_heads(beta * q_end, lane_is_a),
            ke_s=_stack_heads(k_c * q_end, lane_is_a),
            v_s=_stack_heads(v[sl], lane_is_a)))
    gms = [dot_nt(jnp.concatenate([d['at_s'], d['rt_s']], axis=0),
                  jnp.concatenate([d['bh_s'], d['kh_s']], axis=0)) for d in pre]
    a_ab = [jnp.where(strict, g[0:PAIR, 0:PAIR], 0.0) for g in gms]
    a_ak = [jnp.where(strict, g[0:PAIR, PAIR:], 0.0) for g in gms]
    a_rb = [jnp.where(incl, g[PAIR:, 0:PAIR], 0.0) for g in gms]
    a_rk = [jnp.where(incl, g[PAIR:, PAIR:], 0.0) for g in gms]
    pw = [dot(x, x) for x in a_ab]
    akv = [dot(x, d['v_s']) for x, d in zip(a_ak, pre)]
    inv = [eye_f + x for x in a_ab]
    inv = [x + dot(x, p) for x, p in zip(inv, pw)]
    for _ in range(4):
        pw = [dot(p, p) for p in pw]
        inv = [x + dot(x, p) for x, p in zip(inv, pw)]
    wu = [dot(x, jnp.concatenate([d['at_s'], y], axis=1)) for x, d, y in zip(inv, pre, akv)]
    rb_wu = [dot(x, y) for x, y in zip(a_rb, wu)]
    be_wu = [dot(d['be_s'].T, y) for d, y in zip(pre, wu)]
    rkv = [dot(x, d['v_s']) for x, d in zip(a_rk, pre)]
    kev = [dot(d['ke_s'].T, d['v_s']) for d in pre]
    chunks = []
    for i in range(n_ch):
        rt2 = pre[i]['rt_s'] + rb_wu[i][:, 0:PAIR]
        y0 = rb_wu[i][:, PAIR:] + rkv[i]
        m = jnp.where(eye, jnp.exp(pre[i]['tot']), 0.0) + be_wu[i][:, 0:PAIR]
        h0 = be_wu[i][:, PAIR:] + kev[i]
        chunks.append((sls[i], rt2, y0, m, h0))

    h = h_ref[...]
    for sl, rt2, y0, m, h0 in chunks:
        y_s = dot(rt2, h) + y0
        h = dot(m, h) + h0

        y = y_s[0:c] + y_s[c:]
        mean = _dot(y, head_avg, HI)
        yc = y - mean
        var = _dot(yc * yc, head_avg, HI)
        gn = yc * lax.rsqrt(var + GN_EPS)
        out = (gn * lng_ref[...] + lnb_ref[...] + bonus[sl]) * gate[sl]
        o_ref[0, sl, :] = out.astype(o_ref.dtype)
    h_ref[...] = h


def _rwkv(proj3, mu_pack, w0, a0, k_k, k_a, r_k, ln_g, ln_b, w_up, a_up, g_up, t_rows, precision):
    b, s, _ = proj3.shape
    n_pairs = BRANCH_DIM // PAIR
    vec = lambda off: pl.BlockSpec((1, PAIR), lambda bi, p, c: (0, off + p))
    zcol = lambda off: pl.BlockSpec((1, t_rows, PAIR), lambda bi, p, c: (bi, c, off + p))
    lora_w = 2 * LORA_PAD + GATE_LORA
    lora_blk = (3 * BRANCH_DIM) // lora_w
    row = lambda x: x.reshape(1, BRANCH_DIM)
    kernel = functools.partial(_rwkv_kernel, precision=precision)
    return pl.pallas_call(
        kernel,
        grid=(b, n_pairs, s // t_rows),
        in_specs=[
            zcol(0), zcol(n_pairs), zcol(2 * n_pairs),
            pl.BlockSpec((1, t_rows, lora_w), lambda bi, p, c: (bi, c, lora_blk)),
            vec(0), vec(n_pairs), vec(2 * n_pairs),
            pl.BlockSpec((1, lora_w), lambda bi, p, c: (0, lora_blk)),
            vec(0), vec(0), vec(0), vec(0), vec(0), vec(0), vec(0),
            pl.BlockSpec((LORA_PAD, PAIR), lambda bi, p, c: (0, p)),
            pl.BlockSpec((LORA_PAD, PAIR), lambda bi, p, c: (0, p)),
            pl.BlockSpec((GATE_LORA, PAIR), lambda bi, p, c: (0, p)),
        ],
        out_specs=pl.BlockSpec((1, t_rows, PAIR), lambda bi, p, c: (bi, c, p)),
        out_shape=jax.ShapeDtypeStruct((b, s, BRANCH_DIM), BF16),
        scratch_shapes=[
            pltpu.VMEM((PAIR, PAIR), F32),
            pltpu.VMEM((1, PAIR), F32), pltpu.VMEM((1, PAIR), F32), pltpu.VMEM((1, PAIR), F32),
            pltpu.VMEM((1, lora_w), F32),
        ],
        compiler_params=_cp("arbitrary", "arbitrary", "arbitrary"),
        name="rwkv",
    )(proj3, proj3, proj3, proj3, mu_pack, mu_pack, mu_pack, mu_pack,
      row(w0), row(a0), row(k_k), row(k_a), row(r_k), row(ln_g), row(ln_b), w_up, a_up, g_up)


def _pool_kernel(u0_ref, u1_ref, u2_ref, u3_ref, h0_ref, h1_ref, h2_ref, h3_ref, w_ref, sc_ref, o_ref, buf_ref):
    t_rows = u0_ref.shape[1]
    first = pl.program_id(1) == 0
    t = pl.program_id(1) * t_rows + lax.broadcasted_iota(jnp.int32, (t_rows, 1), 0)
    gd = POOL_GROUP_DIM
    for gi, (m, u_ref, halo_ref) in enumerate(zip(POOL_WINDOWS, (u0_ref, u1_ref, u2_ref, u3_ref),
                                                  (h0_ref, h1_ref, h2_ref, h3_ref))):
        u = u_ref[0]
        buf_ref[0:POOL_HALO, :] = jnp.where(first, 0.0, halo_ref[0])
        buf_ref[POOL_HALO:, :] = u
        acc = u
        for sft in range(1, m):
            acc = acc + buf_ref[pl.ds(POOL_HALO - sft, t_rows), :]
        count = jnp.minimum(t + 1, m).astype(F32)
        pooled = acc / count - u
        y = _dot(pooled.astype(BF16), w_ref[gi])
        o_ref[0, :, gi * gd:(gi + 1) * gd] = (y * sc_ref[:, gi * gd:(gi + 1) * gd]).astype(o_ref.dtype)


def _pool(proj3, pool_w_bf16, pool_scale, t_rows):
    b, s, _ = proj3.shape
    gd = POOL_GROUP_DIM
    ublk = COL_U // gd
    hb = t_rows // POOL_HALO
    u_spec = lambda gi: pl.BlockSpec((1, t_rows, gd), lambda bi, c: (bi, c, ublk + gi))
    h_spec = lambda gi: pl.BlockSpec((1, POOL_HALO, gd), lambda bi, c: (bi, jnp.maximum(c * hb - 1, 0), ublk + gi))
    return pl.pallas_call(
        _pool_kernel,
        grid=(b, s // t_rows),
        in_specs=[u_spec(0), u_spec(1), u_spec(2), u_spec(3), h_spec(0), h_spec(1), h_spec(2), h_spec(3),
                  pl.BlockSpec((len(POOL_WINDOWS), gd, gd), lambda bi, c: (0, 0, 0)),
                  pl.BlockSpec((1, BRANCH_DIM), lambda bi, c: (0, 0))],
        out_specs=pl.BlockSpec((1, t_rows, BRANCH_DIM), lambda bi, c: (bi, c, 0)),
        out_shape=jax.ShapeDtypeStruct((b, s, BRANCH_DIM), BF16),
        scratch_shapes=[pltpu.VMEM((t_rows + POOL_HALO, gd), F32)],
        compiler_params=_cp("parallel", "arbitrary"),
        name="pool",
    )(proj3, proj3, proj3, proj3, proj3, proj3, proj3, proj3, pool_w_bf16, pool_scale.reshape(1, BRANCH_DIM))


def _attn_kernel(sink_ref, q_ref, kv_ref, kvh_ref, pos_ref, posh_ref, invf_ref, qg_ref, kg_ref, o_ref):
    w = WINDOW
    not_first = pl.program_id(1) > 0
    head_avg = _head_block_diag(1.0 / HEAD_DIM)
    lane = lax.broadcasted_iota(jnp.int32, (1, PAIR), 1)
    lane_is_a = lane < HEAD_DIM
    rope_lo = (lane % HEAD_DIM) < (HEAD_DIM // 2)

    def norm_rope(x, cos, sin, gain):
        ms = _dot(x * x, head_avg, HI)
        x = x * lax.rsqrt(ms + NORM_EPS) * gain
        rot = jnp.where(rope_lo, -pltpu.roll(x, PAIR - HEAD_DIM // 2, 1), pltpu.roll(x, HEAD_DIM // 2, 1))
        return x * cos + rot * sin

    ang = pos_ref[0] * invf_ref[...]
    cos, sin = jnp.cos(ang), jnp.sin(ang)
    ang_h = posh_ref[0] * invf_ref[...]
    cos_k = jnp.concatenate([jnp.cos(ang_h), cos], axis=0)
    sin_k = jnp.concatenate([jnp.sin(ang_h), sin], axis=0)

    kv = jnp.concatenate([kvh_ref[0], kv_ref[0]], axis=0)
    qi = lax.broadcasted_iota(jnp.int32, (2 * w, 2 * w), 0) % w
    kj = lax.broadcasted_iota(jnp.int32, (2 * w, 2 * w), 1)
    rel = kj - qi
    valid = (rel >= 1) & (rel <= w) & ((kj >= w) | not_first)
    row_is_a = lax.broadcasted_iota(jnp.int32, (2 * w, 1), 0) < w

    for kb in range(ATTN_KV_DIM // PAIR):
        kn = norm_rope(kv[:, kb * PAIR:(kb + 1) * PAIR], cos_k, sin_k, kg_ref[...])
        vv = kv[:, ATTN_KV_DIM + kb * PAIR:ATTN_KV_DIM + (kb + 1) * PAIR]
        kn_sw = pltpu.roll(kn, HEAD_DIM, 1)
        vv_sw = pltpu.roll(vv, HEAD_DIM, 1)
        for half in range(2):
            g = 2 * kb + half
            if half == 0:
                k2 = jnp.where(lane_is_a, kn, kn_sw)
                v2 = jnp.where(lane_is_a, vv, vv_sw)
            else:
                k2 = jnp.where(lane_is_a, kn_sw, kn)
                v2 = jnp.where(lane_is_a, vv_sw, vv)
            k2 = k2.astype(BF16)
            v2 = v2.astype(BF16)
            for jp in range(2):
                qb = 2 * g + jp
                qn = norm_rope(q_ref[0, :, qb * PAIR:(qb + 1) * PAIR], cos, sin, qg_ref[...])
                qn = qn * (HEAD_DIM ** -0.5)
                qs = _stack_heads(qn, lane_is_a).astype(BF16)
                sc = _dot_nt(qs, k2)
                sc = jnp.where(valid, sc, -jnp.inf)
                sink = jnp.where(row_is_a, sink_ref[2 * qb], sink_ref[2 * qb + 1])
                mx = jnp.maximum(jnp.max(sc, axis=-1, keepdims=True), sink)
                e = jnp.exp(sc - mx)
                p = e / (jnp.sum(e, axis=-1, keepdims=True) + jnp.exp(sink - mx))
                o2 = _dot(p.astype(BF16), v2)
                o_ref[0, :, qb * PAIR:(qb + 1) * PAIR] = jnp.where(lane_is_a, o2[0:w], o2[w:]).astype(o_ref.dtype)


def _attn(proj3, pos_lanes, inv_freq, q_gain, k_gain, sinks):
    b, s, _ = proj3.shape
    w = WINDOW
    kvw = 2 * ATTN_KV_DIM
    return pl.pallas_call(
        _attn_kernel,
        grid=(b, s // w),
        in_specs=[
            pl.BlockSpec(memory_space=pltpu.SMEM),
            pl.BlockSpec((1, w, BRANCH_DIM), lambda bi, c: (bi, c, COL_Q // BRANCH_DIM)),
            pl.BlockSpec((1, w, kvw), lambda bi, c: (bi, c, COL_KV // kvw)),
            pl.BlockSpec((1, w, kvw), lambda bi, c: (bi, jnp.maximum(c - 1, 0), COL_KV // kvw)),
            pl.BlockSpec((1, w, PAIR), lambda bi, c: (bi, c, 0)),
            pl.BlockSpec((1, w, PAIR), lambda bi, c: (bi, jnp.maximum(c - 1, 0), 0)),
            pl.BlockSpec((1, PAIR), lambda bi, c: (0, 0)),
            pl.BlockSpec((1, PAIR), lambda bi, c: (0, 0)),
            pl.BlockSpec((1, PAIR), lambda bi, c: (0, 0)),
        ],
        out_specs=pl.BlockSpec((1, w, BRANCH_DIM), lambda bi, c: (bi, c, 0)),
        out_shape=jax.ShapeDtypeStruct((b, s, BRANCH_DIM), BF16),
        compiler_params=_cp("parallel", "arbitrary"),
        name="attn",
    )(sinks, proj3, proj3, proj3, pos_lanes, pos_lanes, inv_freq, q_gain, k_gain)


def _merge_kernel(a_ref, b_ref, c_ref, g0_ref, g1_ref, g2_ref, w_ref, o_ref):
    acc = _sigmoid(g0_ref[...]) * _dot(a_ref[...], w_ref[0])
    acc = acc + _sigmoid(g1_ref[...]) * _dot(b_ref[...], w_ref[1])
    acc = acc + _sigmoid(g2_ref[...]) * _dot(c_ref[...], w_ref[2])
    o_ref[...] = acc.astype(o_ref.dtype)


def _merge(a_out, b_out, c_out, proj, w_branch_bf16, tm, tn):
    n = proj.shape[0]
    br = lambda: pl.BlockSpec((tm, BRANCH_DIM), lambda i, j: (i, 0))
    gate = lambda g: pl.BlockSpec((tm, tn), lambda i, j: (i, (COL_G + g * D_MODEL) // tn + j))
    return pl.pallas_call(
        _merge_kernel,
        grid=(n // tm, D_MODEL // tn),
        in_specs=[br(), br(), br(), gate(0), gate(1), gate(2),
                  pl.BlockSpec((N_BRANCHES, BRANCH_DIM, tn), lambda i, j: (0, 0, j))],
        out_specs=pl.BlockSpec((tm, tn), lambda i, j: (i, j)),
        out_shape=jax.ShapeDtypeStruct((n, D_MODEL), BF16),
        compiler_params=_cp("parallel", "arbitrary"),
        name="merge",
    )(a_out, b_out, c_out, proj, proj, proj, w_branch_bf16)


def _out_proj_kernel(m_ref, x_ref, w_ref, g_ref, wr_ref, x1_ref, hn_ref, lg_ref):
    x1 = x_ref[...] + _dot(m_ref[...], w_ref[...])
    x1_ref[...] = x1
    ms = jnp.mean(x1 * x1, axis=-1, keepdims=True)
    hn = x1 * lax.rsqrt(ms + NORM_EPS) * g_ref[...]
    hn_ref[...] = hn
    lg_ref[...] = _dot_nt(wr_ref[...], hn, HI)


def _out_proj(merged, x2d, w_out_bf16, ffn_gain, w_router_t, tm):
    n, d = x2d.shape
    return pl.pallas_call(
        _out_proj_kernel,
        grid=(n // tm,),
        in_specs=[
            pl.BlockSpec((tm, d), lambda i: (i, 0)),
            pl.BlockSpec((tm, d), lambda i: (i, 0)),
            pl.BlockSpec((d, d), lambda i: (0, 0)),
            pl.BlockSpec((1, d), lambda i: (0, 0)),
            pl.BlockSpec((ROUTER_ROWS, d), lambda i: (0, 0)),
        ],
        out_specs=[
            pl.BlockSpec((tm, d), lambda i: (i, 0)),
            pl.BlockSpec((tm, d), lambda i: (i, 0)),
            pl.BlockSpec((ROUTER_ROWS, tm), lambda i: (0, i)),
        ],
        out_shape=[
            jax.ShapeDtypeStruct((n, d), F32),
            jax.ShapeDtypeStruct((n, d), F32),
            jax.ShapeDtypeStruct((ROUTER_ROWS, n), F32),
        ],
        compiler_params=_cp("parallel"),
        name="out_proj",
    )(merged, x2d, w_out_bf16, ffn_gain.reshape(1, d), w_router_t)


def _experts_kernel(be_ref, x_ref, w1_ref, w3_ref, w2_ref, o_ref):
    del be_ref
    xb = x_ref[...].astype(BF16)
    h1 = _dot(xb, w1_ref[0])
    h3 = _dot(xb, w3_ref[0])
    act = (h1 * _sigmoid(h1)) * h3
    o_ref[...] = _dot(act.astype(BF16), w2_ref[0])


def _experts(xg, block_expert, w1, w3, w2):
    n_rows, d = xg.shape
    ff = w1.shape[-1]
    blk = MOE_BLOCK
    return pl.pallas_call(
        _experts_kernel,
        grid_spec=pltpu.PrefetchScalarGridSpec(
            num_scalar_prefetch=1,
            grid=(n_rows // blk,),
            in_specs=[
                pl.BlockSpec((blk, d), lambda i, be: (i, 0)),
                pl.BlockSpec((1, d, ff), lambda i, be: (be[i], 0, 0)),
                pl.BlockSpec((1, d, ff), lambda i, be: (be[i], 0, 0)),
                pl.BlockSpec((1, ff, d), lambda i, be: (be[i], 0, 0)),
            ],
            out_specs=pl.BlockSpec((blk, d), lambda i, be: (i, 0)),
        ),
        out_shape=jax.ShapeDtypeStruct((n_rows, d), F32),
        compiler_params=_cp("arbitrary"),
        name="experts",
    )(block_expert, xg, w1, w3, w2)


def _pack_w_in(w):
    d = w.shape[0]
    z32 = jnp.zeros((d, LORA_PAD - DECAY_LORA), w.dtype)
    c = BRANCH_DIM
    o_wd, o_ad, o_gd = 3 * c, 3 * c + DECAY_LORA, 3 * c + DECAY_LORA + ICLR_LORA
    o_u = o_gd + GATE_LORA
    o_q = o_u + c
    o_k = o_q + c
    o_g = o_k + 2 * ATTN_KV_DIM
    parts = [w[:, :o_wd], w[:, o_wd:o_ad], z32, w[:, o_ad:o_gd], z32, w[:, o_gd:o_u],
             w[:, o_u:o_q], w[:, o_k:o_g], w[:, o_q:o_k], w[:, o_g:]]
    return jnp.concatenate(parts, axis=1).astype(BF16)


def _pack_mu(mu):
    z32 = jnp.zeros((LORA_PAD - DECAY_LORA,), mu.dtype)
    c = BRANCH_DIM
    o_wd, o_ad, o_gd = 3 * c, 3 * c + DECAY_LORA, 3 * c + DECAY_LORA + ICLR_LORA
    return jnp.concatenate([mu[:o_wd], mu[o_wd:o_ad], z32, mu[o_ad:o_gd], z32, mu[o_gd:]]).reshape(1, Z_COLS)


def _pad_rows(w, rows):
    return jnp.concatenate([w, jnp.zeros((rows - w.shape[0], w.shape[1]), w.dtype)], axis=0)


def _router_weights(w_grp, w_exp):
    d = w_grp.shape[0]
    pad = jnp.zeros((d, 8 - N_GROUPS), w_grp.dtype)
    return jnp.concatenate([w_grp, pad, w_exp], axis=1).T


def _route_kernel(lg_ref, bias_ref, ids_ref, gcol_ref, cnt_ref, carry_ref):
    tm = lg_ref.shape[1]

    @pl.when(pl.program_id(0) == 0)
    def _():
        carry_ref[...] = jnp.zeros_like(carry_ref)

    lg = lg_ref[...] + bias_ref[...]
    row8 = lax.broadcasted_iota(jnp.int32, (8, tm), 0)
    row8f = row8.astype(F32)
    neg_inf = -jnp.inf

    def first_argmax(x):
        mx = jnp.max(x, axis=0, keepdims=True)
        return mx, jnp.min(jnp.where(x == mx, row8f, 8.0), axis=0, keepdims=True).astype(jnp.int32)

    grp_logits = jnp.where(row8 < N_GROUPS, lg[0:8], neg_inf)
    gmax, grp = first_argmax(grp_logits)
    p_grp = 1.0 / jnp.sum(jnp.exp(grp_logits - gmax), axis=0, keepdims=True)
    in_grp = lg[8:16]
    for g in range(1, N_GROUPS):
        in_grp = jnp.where(grp == g, lg[8 + 8 * g:16 + 8 * g], in_grp)
    m1, i1 = first_argmax(in_grp)
    rest = jnp.where(row8 == i1, neg_inf, in_grp)
    m2, i2 = first_argmax(rest)
    e2 = jnp.exp(m2 - m1)
    gate1 = p_grp / (1.0 + e2)
    gate2 = p_grp * e2 / (1.0 + e2)
    exp1 = grp * EXPERTS_PER_GROUP + i1
    exp2 = grp * EXPERTS_PER_GROUP + i2

    rows = lax.broadcasted_iota(jnp.int32, (N_EXPERTS, tm), 0)
    hot1 = (rows == exp1).astype(F32)
    hot2 = (rows == exp2).astype(F32)
    cnt = (hot1 + hot2).astype(BF16)
    src = lax.broadcasted_iota(jnp.int32, (tm, tm), 0)
    dst = lax.broadcasted_iota(jnp.int32, (tm, tm), 1)
    before = _dot(cnt, (src < dst).astype(BF16)) + carry_ref[...]
    rank1 = jnp.sum(hot1 * before, axis=0, keepdims=True).astype(jnp.int32)
    rank2 = jnp.sum(hot2 * before, axis=0, keepdims=True).astype(jnp.int32)
    carry_ref[...] += _dot(cnt, jnp.ones((tm, tm), BF16))
    cnt_ref[...] = carry_ref[...]

    ids_ref[...] = jnp.where(row8 == 0, exp1, jnp.where(row8 == 1, exp2,
                             jnp.where(row8 == 2, rank1, jnp.where(row8 == 3, rank2, 0))))
    row128 = lax.broadcasted_iota(jnp.int32, (LANES, tm), 0)
    gates_t = jnp.where(row128 == 0, gate1, jnp.where(row128 == 1, gate2, 0.0))
    gcol_ref[...] = gates_t.T


def _route(logits_t, bias_col, tm):
    n = logits_t.shape[1]
    bias = jnp.broadcast_to(bias_col, (ROUTER_ROWS, tm))
    return pl.pallas_call(
        _route_kernel,
        grid=(n // tm,),
        in_specs=[pl.BlockSpec((ROUTER_ROWS, tm), lambda i: (0, i)),
                  pl.BlockSpec((ROUTER_ROWS, tm), lambda i: (0, 0))],
        out_specs=[pl.BlockSpec((8, tm), lambda i: (0, i)),
                   pl.BlockSpec((tm, LANES), lambda i: (i, 0)),
                   pl.BlockSpec((N_EXPERTS, tm), lambda i: (0, 0))],
        out_shape=[jax.ShapeDtypeStruct((8, n), jnp.int32),
                   jax.ShapeDtypeStruct((n, LANES), F32),
                   jax.ShapeDtypeStruct((N_EXPERTS, tm), F32)],
        scratch_shapes=[pltpu.VMEM((N_EXPERTS, tm), F32)],
        compiler_params=_cp("arbitrary"),
        name="route",
    )(logits_t, bias)


def _row_copy(src_ref, src_row, dst_ref, dst_row, sem):
    return pltpu.make_async_copy(src_ref.at[pl.ds(src_row, 1)], dst_ref.at[pl.ds(dst_row, 1)], sem)


def _scatter_kernel(dest_ref, hn_ref, xg_in_ref, xg_ref, sem):
    del xg_in_ref
    tm = hn_ref.shape[0]
    n = pl.num_programs(0) * tm
    base = pl.program_id(0) * tm

    def issue(r, carry):
        for k in range(TOP_K):
            _row_copy(hn_ref, r, xg_ref, dest_ref[k * n + base + r], sem).start()
        return carry

    lax.fori_loop(0, tm, issue, 0)
    for _ in range(TOP_K):
        pltpu.make_async_copy(hn_ref, xg_ref.at[pl.ds(0, tm)], sem).wait()


def _scatter(dest_flat, hn, n_rows, tm):
    n, d = hn.shape
    return pl.pallas_call(
        _scatter_kernel,
        grid_spec=pltpu.PrefetchScalarGridSpec(
            num_scalar_prefetch=1,
            grid=(n // tm,),
            in_specs=[pl.BlockSpec((tm, d), lambda i, dest: (i, 0)),
                      pl.BlockSpec(memory_space=pl.ANY)],
            out_specs=pl.BlockSpec(memory_space=pl.ANY),
            scratch_shapes=[pltpu.SemaphoreType.DMA],
        ),
        out_shape=jax.ShapeDtypeStruct((n_rows, d), hn.dtype),
        input_output_aliases={2: 0},
        compiler_params=_cp("arbitrary"),
        name="moe_scatter",
    )(dest_flat, hn, jnp.zeros((n_rows, d), hn.dtype))


def _combine_kernel(dest_ref, x_ref, gcol_ref, yg_ref, o_ref, buf_ref, sem):
    tm = x_ref.shape[0]
    n = pl.num_programs(0) * tm
    base = pl.program_id(0) * tm

    def issue(r, carry):
        for k in range(TOP_K):
            _row_copy(yg_ref, dest_ref[k * n + base + r], buf_ref.at[k], r, sem).start()
        return carry

    lax.fori_loop(0, tm, issue, 0)
    for k in range(TOP_K):
        pltpu.make_async_copy(yg_ref.at[pl.ds(0, tm)], buf_ref.at[k], sem).wait()
    g = gcol_ref[...]
    o_ref[...] = x_ref[...] + g[:, 0:1] * buf_ref[0] + g[:, 1:2] * buf_ref[1]


def _combine(dest_flat, x1, gcol, yg, tm):
    n, d = x1.shape
    return pl.pallas_call(
        _combine_kernel,
        grid_spec=pltpu.PrefetchScalarGridSpec(
            num_scalar_prefetch=1,
            grid=(n // tm,),
            in_specs=[pl.BlockSpec((tm, d), lambda i, dest: (i, 0)),
                      pl.BlockSpec((tm, LANES), lambda i, dest: (i, 0)),
                      pl.BlockSpec(memory_space=pl.ANY)],
            out_specs=pl.BlockSpec((tm, d), lambda i, dest: (i, 0)),
            scratch_shapes=[pltpu.VMEM((TOP_K, tm, d), F32), pltpu.SemaphoreType.DMA],
        ),
        out_shape=jax.ShapeDtypeStruct((n, d), F32),
        compiler_params=_cp("arbitrary"),
        name="moe_combine",
    )(dest_flat, x1, gcol, yg)


def _moe(x1, hn, logits_t, bias_col, w1, w3, w2):
    n, d = hn.shape
    blk = MOE_BLOCK
    tm = _pick(n, 512)
    ids, gcol, cnt = _route(logits_t, bias_col, tm)
    counts = cnt[:, 0].astype(jnp.int32)
    padded = (counts + blk - 1) // blk * blk
    pends = jnp.cumsum(padded)
    pstarts = pends - padded
    dest_flat = (jnp.take(pstarts, ids[0:TOP_K]) + ids[TOP_K:2 * TOP_K]).reshape(-1)
    n_rows = n * TOP_K + N_EXPERTS * blk
    block_start = jnp.arange(n_rows // blk, dtype=jnp.int32) * blk
    block_expert = jnp.minimum(jnp.searchsorted(pends, block_start, side='right'), N_EXPERTS - 1).astype(jnp.int32)
    xg = _scatter(dest_flat, hn, n_rows, tm)
    yg = _experts(xg, block_expert, w1, w3, w2)
    return _combine(dest_flat, x1, gcol, yg, tm)


def _pick(n, pref):
    t = pref
    while n % t:
        t //= 2
    return t


def _layer(x2d, b, s, pos_lanes, inv_freq, p, rwkv_precision):
    n = x2d.shape[0]
    proj = _norm_matmul(x2d, p['attn_norm'], p['w_in'], _pick(n, 1024), 1024)
    proj3 = proj.reshape(b, s, P_COLS)
    a_out = _rwkv(proj3, p['mu'], p['w0'], p['a0'], p['k_k'], p['k_a'], p['r_k'], p['ln_g'], p['ln_b'],
                  p['w_up'], p['a_up'], p['g_up'], _pick(s, 256), rwkv_precision)
    b_out = _pool(proj3, p['pool_w'], p['pool_scale'], _pick(s, 512))
    c_out = _attn(proj3, pos_lanes, inv_freq, p['q_gain'], p['k_gain'], p['sinks'])
    merged = _merge(a_out.reshape(n, -1), b_out.reshape(n, -1), c_out.reshape(n, -1), proj,
                    p['w_branch'], _pick(n, 1024), 512)
    x1, hn, logits_t = _out_proj(merged, x2d, p['w_out'], p['ffn_norm'], p['w_router_t'], _pick(n, 512))
    return _moe(x1, hn, logits_t, p['router_bias'], p['w1'], p['w3'], p['w2'])


def kernel(x, positions, attn_norm, w_in, tmix_mu, rwkv_w0, rwkv_w_up, rwkv_a0, rwkv_a_up, rwkv_g_up, rwkv_k_k, rwkv_k_a, rwkv_r_k, rwkv_ln_g, rwkv_ln_b, pool_w, pool_scale, q_norm, k_norm, attn_sinks, w_branch, w_out, ffn_norm, router_grp_w, router_grp_b, router_exp_w, router_exp_b, expert_w1, expert_w3, expert_w2):
    b, s, d = x.shape
    n = b * s
    half = HEAD_DIM // 2
    inv_freq = ROPE_THETA ** (-jnp.arange(half, dtype=F32) / half)
    inv_freq = jnp.tile(inv_freq, PAIR // half).reshape(1, PAIR)
    pos_lanes = jnp.broadcast_to(positions.astype(F32)[..., None], (b, s, PAIR))
    x2d = x.reshape(n, d)
    for l in range(w_in.shape[0]):
        p = {
            'attn_norm': attn_norm[l],
            'w_in': _pack_w_in(w_in[l]),
            'mu': _pack_mu(tmix_mu[l]),
            'w0': rwkv_w0[l], 'a0': rwkv_a0[l], 'k_k': rwkv_k_k[l], 'k_a': rwkv_k_a[l],
            'r_k': rwkv_r_k[l].reshape(-1), 'ln_g': rwkv_ln_g[l], 'ln_b': rwkv_ln_b[l],
            'w_up': _pad_rows(rwkv_w_up[l], LORA_PAD), 'a_up': _pad_rows(rwkv_a_up[l], LORA_PAD),
            'g_up': rwkv_g_up[l],
            'pool_w': pool_w[l].astype(BF16), 'pool_scale': pool_scale[l],
            'q_gain': jnp.tile(q_norm[l], PAIR // HEAD_DIM).reshape(1, PAIR),
            'k_gain': jnp.tile(k_norm[l], PAIR // HEAD_DIM).reshape(1, PAIR),
            'sinks': attn_sinks[l],
            'w_branch': w_branch[l].astype(BF16), 'w_out': w_out[l].astype(BF16),
            'ffn_norm': ffn_norm[l],
            'w_router_t': _router_weights(router_grp_w[l], router_exp_w[l]),
            'router_bias': jnp.concatenate([router_grp_b[l], jnp.zeros((8 - N_GROUPS,), F32),
                                            router_exp_b[l]]).reshape(ROUTER_ROWS, 1),
            'w1': expert_w1[l].astype(BF16), 'w3': expert_w3[l].astype(BF16), 'w2': expert_w2[l].astype(BF16),
        }
        x2d = _layer(x2d, b, s, pos_lanes, inv_freq, p, None)
    return x2d.reshape(b, s, d)
```

```python
import functools

import numpy as np
import jax
import jax.numpy as jnp
from jax import lax
from jax.experimental import pallas as pl
from jax.experimental.pallas import tpu as pltpu

F32 = jnp.float32
BF16 = jnp.bfloat16
HI = lax.Precision.HIGHEST

D_MODEL = 2048
HEAD_DIM = 64
BRANCH_DIM = D_MODEL // 2
DECAY_LORA = 96
ICLR_LORA = 96
GATE_LORA = 256
LORA_PAD = 128
POOL_WINDOWS = (2, 4, 8, 16)
POOL_GROUP_DIM = BRANCH_DIM // len(POOL_WINDOWS)
POOL_HALO = 16
ATTN_Q_HEADS = BRANCH_DIM // HEAD_DIM
ATTN_KV_HEADS = 4
ATTN_KV_DIM = ATTN_KV_HEADS * HEAD_DIM
WINDOW = 128
ROPE_THETA = 10000.0
N_BRANCHES = 3
N_GROUPS = 4
EXPERTS_PER_GROUP = 8
N_EXPERTS = N_GROUPS * EXPERTS_PER_GROUP
TOP_K = 2
EXPERT_FF = 512
NORM_EPS = 1e-6
GN_EPS = 64e-5

LANES = 128
PAIR = 2 * HEAD_DIM
CHUNK = 64
ROUTER_ROWS = 40
MOE_BLOCK = 256

Z_COLS = 3 * BRANCH_DIM + 2 * LORA_PAD + GATE_LORA
COL_U = Z_COLS
COL_KV = COL_U + BRANCH_DIM
COL_Q = COL_KV + 2 * ATTN_KV_DIM
COL_G = COL_Q + BRANCH_DIM
P_COLS = COL_G + N_BRANCHES * D_MODEL

VMEM_LIMIT = 56 * 1024 * 1024


def _cp(*sem):
    return pltpu.CompilerParams(dimension_semantics=sem, vmem_limit_bytes=VMEM_LIMIT)


def _dot(a, b, precision=None):
    return jnp.dot(a, b, preferred_element_type=F32, precision=precision)


def _dot_nt(a, b, precision=None):
    return lax.dot_general(a, b, (((1,), (1,)), ((), ())), preferred_element_type=F32, precision=precision)


def _sigmoid(x):
    return 1.0 / (1.0 + jnp.exp(-x))


def _split_bf16(x):
    hi = x.astype(BF16)
    return hi, (x - hi.astype(F32)).astype(BF16)


def _dot_split_lhs(x, m):
    m = m.astype(BF16)
    return _dot(jnp.concatenate(_split_bf16(x), axis=1), jnp.concatenate([m, m], axis=0))


def _dot_split3(x, w):
    xh, xl = _split_bf16(x)
    wh, wl = _split_bf16(w)
    return _dot(jnp.concatenate([xh, xl, xh], axis=1), jnp.concatenate([wh, wh, wl], axis=0))


def _head_block_diag(scale):
    r = lax.broadcasted_iota(jnp.int32, (PAIR, PAIR), 0) // HEAD_DIM
    c = lax.broadcasted_iota(jnp.int32, (PAIR, PAIR), 1) // HEAD_DIM
    return jnp.where(r == c, scale, 0.0).astype(F32)


def _norm_matmul_kernel(x_ref, g_ref, w_ref, o_ref, h_ref):
    @pl.when(pl.program_id(1) == 0)
    def _():
        x = x_ref[...]
        ms = jnp.mean(x * x, axis=-1, keepdims=True)
        h_ref[...] = (x * lax.rsqrt(ms + NORM_EPS) * g_ref[...]).astype(BF16)

    o_ref[...] = _dot(h_ref[...], w_ref[...])


def _norm_matmul(x2d, gain, w_bf16, tm, tn):
    n, d = x2d.shape
    cols = w_bf16.shape[1]
    return pl.pallas_call(
        _norm_matmul_kernel,
        grid=(n // tm, cols // tn),
        in_specs=[
            pl.BlockSpec((tm, d), lambda i, j: (i, 0)),
            pl.BlockSpec((1, d), lambda i, j: (0, 0)),
            pl.BlockSpec((d, tn), lambda i, j: (0, j)),
        ],
        out_specs=pl.BlockSpec((tm, tn), lambda i, j: (i, j)),
        out_shape=jax.ShapeDtypeStruct((n, cols), F32),
        scratch_shapes=[pltpu.VMEM((tm, d), BF16)],
        compiler_params=_cp("arbitrary", "arbitrary"),
        name="norm_matmul",
    )(x2d, gain.reshape(1, d), w_bf16)


def _stack_heads(x, lane_is_a):
    return jnp.concatenate([jnp.where(lane_is_a, x, 0.0), jnp.where(lane_is_a, 0.0, x)], axis=0)


def _rwkv_kernel(zr_ref, zk_ref, zv_ref, zl_ref, mur_ref, muk_ref, muv_ref, mul_ref,
                 w0_ref, a0_ref, kkg_ref, ka_ref, rk_ref, lng_ref, lnb_ref,
                 wup_ref, aup_ref, gup_ref, o_ref,
                 h_ref, cr_ref, ck_ref, cv_ref, cl_ref, *, precision):
    t_rows = zr_ref.shape[1]

    @pl.when(pl.program_id(2) == 0)
    def _():
        h_ref[...] = jnp.zeros_like(h_ref)
        cr_ref[...] = jnp.zeros_like(cr_ref)
        ck_ref[...] = jnp.zeros_like(ck_ref)
        cv_ref[...] = jnp.zeros_like(cv_ref)
        cl_ref[...] = jnp.zeros_like(cl_ref)

    row = lax.broadcasted_iota(jnp.int32, (t_rows, 1), 0)

    def shifted(z_ref, carry_ref, mu_ref):
        z = z_ref[0]
        prev = jnp.where(row == 0, carry_ref[...], pltpu.roll(z, 1, 0))
        carry_ref[...] = z_ref[0, t_rows - 1:t_rows, :]
        return z + (prev - z) * mu_ref[...]

    r = shifted(zr_ref, cr_ref, mur_ref)
    k = shifted(zk_ref, ck_ref, muk_ref)
    v = shifted(zv_ref, cv_ref, muv_ref)
    zl = shifted(zl_ref, cl_ref, mul_ref)
    wd = zl[:, 0:LORA_PAD]
    ad = zl[:, LORA_PAD:2 * LORA_PAD]
    gd = zl[:, 2 * LORA_PAD:]

    head_sum = _head_block_diag(1.0)
    head_avg = _head_block_diag(1.0 / HEAD_DIM)

    w_pre = w0_ref[...] + _dot_split3(jnp.tanh(wd), wup_ref[...])
    neg = -w_pre
    softplus = jnp.maximum(neg, 0.0) + jnp.log(1.0 + jnp.exp(-jnp.abs(neg)))
    log_decay = -jnp.exp(-softplus - 0.5)
    a = _sigmoid(a0_ref[...] + _dot_split3(ad, aup_ref[...]))
    gate = _dot_split3(_sigmoid(gd), gup_ref[...])
    kk = k * kkg_ref[...]
    kk = kk * lax.rsqrt(jnp.maximum(_dot_split_lhs(kk * kk, head_sum), 1e-24))
    k = k * (1.0 + (a - 1.0) * ka_ref[...])
    bonus = _dot_split_lhs(r * k * rk_ref[...], head_sum) * v

    c = CHUNK
    ri = lax.broadcasted_iota(jnp.int32, (c, c), 0)
    ci = lax.broadcasted_iota(jnp.int32, (c, c), 1)
    tri_incl = (ci <= ri).astype(F32)
    tri2 = jnp.concatenate([tri_incl, tri_incl], axis=1).astype(BF16)
    lane_is_a = lax.broadcasted_iota(jnp.int32, (1, PAIR), 1) < HEAD_DIM
    rs = lax.broadcasted_iota(jnp.int32, (PAIR, PAIR), 0)
    cs = lax.broadcasted_iota(jnp.int32, (PAIR, PAIR), 1)
    strict = (cs % c) < (rs % c)
    incl = (cs % c) <= (rs % c)
    eye = rs == cs
    eye_f = eye.astype(F32)
    if precision is None:
        dot = lambda x, y: _dot(x.astype(BF16), y.astype(BF16))
        dot_nt = lambda x, y: _dot_nt(x.astype(BF16), y.astype(BF16))
    else:
        dot = functools.partial(_dot, precision=precision)
        dot_nt = functools.partial(_dot_nt, precision=precision)

    n_ch = t_rows // c
    sls = [slice(ch * c, (ch + 1) * c) for ch in range(n_ch)]
    cums = [_dot(tri2, jnp.concatenate(_split_bf16(log_decay[sl]), axis=0)) for sl in sls]
    pre = []
    for sl, cum in zip(sls, cums):
        lw = log_decay[sl]
        tot = cum[c - 1:c, :]
        p_in = jnp.exp(cum)
        p_ex = jnp.exp(cum - lw)
        q_inv = jnp.exp(-cum)
        q_end = jnp.exp(tot - cum)
        kk_c, a_c, k_c = kk[sl], a[sl], k[sl]
        beta = kk_c * a_c
        pre.append(dict(
            tot=tot,
            at_s=_stack_heads(-kk_c * p_ex, lane_is_a),
            rt_s=_stack_heads(r[sl] * p_in, lane_is_a),
            bh_s=_stack_heads(beta * q_inv, lane_is_a),
            kh_s=_stack_heads(k_c * q_inv, lane_is_a),
            be_s=_stack_heads(beta * q_end, lane_is_a),
            ke_s=_stack_heads(k_c * q_end, lane_is_a),
            v_s=_stack_heads(v[sl], lane_is_a)))
    gms = [dot_nt(jnp.concatenate([d['at_s'], d['rt_s']], axis=0),
                  jnp.concatenate([d['bh_s'], d['kh_s']], axis=0)) for d in pre]
    a_ab = [jnp.where(strict, g[0:PAIR, 0:PAIR], 0.0) for g in gms]
    a_ak = [jnp.where(strict, g[0:PAIR, PAIR:], 0.0) for g in gms]
    a_rb = [jnp.where(incl, g[PAIR:, 0:PAIR], 0.0) for g in gms]
    a_rk = [jnp.where(incl, g[PAIR:, PAIR:], 0.0) for g in gms]
    pw = [dot(x, x) for x in a_ab]
    akv = [dot(x, d['v_s']) for x, d in zip(a_ak, pre)]
    inv = [eye_f + x for x in a_ab]
    inv = [x + dot(x, p) for x, p in zip(inv, pw)]
    for _ in range(4):
        pw = [dot(p, p) for p in pw]
        inv = [x + dot(x, p) for x, p in zip(inv, pw)]
    wu = [dot(x, jnp.concatenate([d['at_s'], y], axis=1)) for x, d, y in zip(inv, pre, akv)]
    rb_wu = [dot(x, y) for x, y in zip(a_rb, wu)]
    be_wu = [dot(d['be_s'].T, y) for d, y in zip(pre, wu)]
    rkv = [dot(x, d['v_s']) for x, d in zip(a_rk, pre)]
    kev = [dot(d['ke_s'].T, d['v_s']) for d in pre]
    chunks = []
    for i in range(n_ch):
        rt2 = pre[i]['rt_s'] + rb_wu[i][:, 0:PAIR]
        y0 = rb_wu[i][:, PAIR:] + rkv[i]
        m = jnp.where(eye, jnp.exp(pre[i]['tot']), 0.0) + be_wu[i][:, 0:PAIR]
        h0 = be_wu[i][:, PAIR:] + kev[i]
        chunks.append((sls[i], rt2, y0, m, h0))

    h = h_ref[...]
    for sl, rt2, y0, m, h0 in chunks:
        y_s = dot(rt2, h) + y0
        h = dot(m, h) + h0

        y = y_s[0:c] + y_s[c:]
        mean = _dot_split_lhs(y, head_avg)
        yc = y - mean
        var = _dot_split_lhs(yc * yc, head_avg)
        gn = yc * lax.rsqrt(var + GN_EPS)
        out = (gn * lng_ref[...] + lnb_ref[...] + bonus[sl]) * gate[sl]
        o_ref[0, sl, :] = out.astype(o_ref.dtype)
    h_ref[...] = h


def _rwkv(proj3, mu_pack, w0, a0, k_k, k_a, r_k, ln_g, ln_b, w_up, a_up, g_up, t_rows, precision):
    b, s, _ = proj3.shape
    n_pairs = BRANCH_DIM // PAIR
    vec = lambda off: pl.BlockSpec((1, PAIR), lambda bi, p, c: (0, off + p))
    zcol = lambda off: pl.BlockSpec((1, t_rows, PAIR), lambda bi, p, c: (bi, c, off + p))
    lora_w = 2 * LORA_PAD + GATE_LORA
    lora_blk = (3 * BRANCH_DIM) // lora_w
    row = lambda x: x.reshape(1, BRANCH_DIM)
    kernel = functools.partial(_rwkv_kernel, precision=precision)
    return pl.pallas_call(
        kernel,
        grid=(b, n_pairs, s // t_rows),
        in_specs=[
            zcol(0), zcol(n_pairs), zcol(2 * n_pairs),
            pl.BlockSpec((1, t_rows, lora_w), lambda bi, p, c: (bi, c, lora_blk)),
            vec(0), vec(n_pairs), vec(2 * n_pairs),
            pl.BlockSpec((1, lora_w), lambda bi, p, c: (0, lora_blk)),
            vec(0), vec(0), vec(0), vec(0), vec(0), vec(0), vec(0),
            pl.BlockSpec((LORA_PAD, PAIR), lambda bi, p, c: (0, p)),
            pl.BlockSpec((LORA_PAD, PAIR), lambda bi, p, c: (0, p)),
            pl.BlockSpec((GATE_LORA, PAIR), lambda bi, p, c: (0, p)),
        ],
        out_specs=pl.BlockSpec((1, t_rows, PAIR), lambda bi, p, c: (bi, c, p)),
        out_shape=jax.ShapeDtypeStruct((b, s, BRANCH_DIM), BF16),
        scratch_shapes=[
            pltpu.VMEM((PAIR, PAIR), F32),
            pltpu.VMEM((1, PAIR), F32), pltpu.VMEM((1, PAIR), F32), pltpu.VMEM((1, PAIR), F32),
            pltpu.VMEM((1, lora_w), F32),
        ],
        compiler_params=_cp("arbitrary", "arbitrary", "arbitrary"),
        name="rwkv",
    )(proj3, proj3, proj3, proj3, mu_pack, mu_pack, mu_pack, mu_pack,
      row(w0), row(a0), row(k_k), row(k_a), row(r_k), row(ln_g), row(ln_b), w_up, a_up, g_up)


def _pool_kernel(u0_ref, u1_ref, u2_ref, u3_ref, h0_ref, h1_ref, h2_ref, h3_ref, w_ref, sc_ref, o_ref, buf_ref):
    t_rows = u0_ref.shape[1]
    first = pl.program_id(1) == 0
    t = pl.program_id(1) * t_rows + lax.broadcasted_iota(jnp.int32, (t_rows, 1), 0)
    gd = POOL_GROUP_DIM
    for gi, (m, u_ref, halo_ref) in enumerate(zip(POOL_WINDOWS, (u0_ref, u1_ref, u2_ref, u3_ref),
                                                  (h0_ref, h1_ref, h2_ref, h3_ref))):
        u = u_ref[0]
        buf_ref[0:POOL_HALO, :] = jnp.where(first, 0.0, halo_ref[0])
        buf_ref[POOL_HALO:, :] = u
        acc = u
        for sft in range(1, m):
            acc = acc + buf_ref[pl.ds(POOL_HALO - sft, t_rows), :]
        count = jnp.minimum(t + 1, m).astype(F32)
        pooled = acc / count - u
        y = _dot(pooled.astype(BF16), w_ref[gi])
        o_ref[0, :, gi * gd:(gi + 1) * gd] = (y * sc_ref[:, gi * gd:(gi + 1) * gd]).astype(o_ref.dtype)


def _pool(proj3, pool_w_bf16, pool_scale, t_rows):
    b, s, _ = proj3.shape
    gd = POOL_GROUP_DIM
    ublk = COL_U // gd
    hb = t_rows // POOL_HALO
    u_spec = lambda gi: pl.BlockSpec((1, t_rows, gd), lambda bi, c: (bi, c, ublk + gi))
    h_spec = lambda gi: pl.BlockSpec((1, POOL_HALO, gd), lambda bi, c: (bi, jnp.maximum(c * hb - 1, 0), ublk + gi))
    return pl.pallas_call(
        _pool_kernel,
        grid=(b, s // t_rows),
        in_specs=[u_spec(0), u_spec(1), u_spec(2), u_spec(3), h_spec(0), h_spec(1), h_spec(2), h_spec(3),
                  pl.BlockSpec((len(POOL_WINDOWS), gd, gd), lambda bi, c: (0, 0, 0)),
                  pl.BlockSpec((1, BRANCH_DIM), lambda bi, c: (0, 0))],
        out_specs=pl.BlockSpec((1, t_rows, BRANCH_DIM), lambda bi, c: (bi, c, 0)),
        out_shape=jax.ShapeDtypeStruct((b, s, BRANCH_DIM), BF16),
        scratch_shapes=[pltpu.VMEM((t_rows + POOL_HALO, gd), F32)],
        compiler_params=_cp("parallel", "arbitrary"),
        name="pool",
    )(proj3, proj3, proj3, proj3, proj3, proj3, proj3, proj3, pool_w_bf16, pool_scale.reshape(1, BRANCH_DIM))


def _attn_kernel(sink_ref, q_ref, kv_ref, kvh_ref, pos_ref, posh_ref, invf_ref, qg_ref, kg_ref, o_ref):
    w = WINDOW
    not_first = pl.program_id(1) > 0
    head_avg = _head_block_diag(1.0 / HEAD_DIM)
    lane = lax.broadcasted_iota(jnp.int32, (1, PAIR), 1)
    lane_is_a = lane < HEAD_DIM
    rope_lo = (lane % HEAD_DIM) < (HEAD_DIM // 2)

    def norm_rope(x, cos, sin, gain):
        ms = _dot_split_lhs(x * x, head_avg)
        x = x * lax.rsqrt(ms + NORM_EPS) * gain
        rot = jnp.where(rope_lo, -pltpu.roll(x, PAIR - HEAD_DIM // 2, 1), pltpu.roll(x, HEAD_DIM // 2, 1))
        return x * cos + rot * sin

    ang = pos_ref[0] * invf_ref[...]
    cos, sin = jnp.cos(ang), jnp.sin(ang)
    ang_h = posh_ref[0] * invf_ref[...]
    cos_k = jnp.concatenate([jnp.cos(ang_h), cos], axis=0)
    sin_k = jnp.concatenate([jnp.sin(ang_h), sin], axis=0)

    kv = jnp.concatenate([kvh_ref[0], kv_ref[0]], axis=0)
    qi = lax.broadcasted_iota(jnp.int32, (2 * w, 2 * w), 0) % w
    kj = lax.broadcasted_iota(jnp.int32, (2 * w, 2 * w), 1)
    rel = kj - qi
    valid = (rel >= 1) & (rel <= w) & ((kj >= w) | not_first)
    row_is_a = lax.broadcasted_iota(jnp.int32, (2 * w, 1), 0) < w

    for kb in range(ATTN_KV_DIM // PAIR):
        kn = norm_rope(kv[:, kb * PAIR:(kb + 1) * PAIR], cos_k, sin_k, kg_ref[...])
        vv = kv[:, ATTN_KV_DIM + kb * PAIR:ATTN_KV_DIM + (kb + 1) * PAIR]
        kn_sw = pltpu.roll(kn, HEAD_DIM, 1)
        vv_sw = pltpu.roll(vv, HEAD_DIM, 1)
        for half in range(2):
            g = 2 * kb + half
            if half == 0:
                k2 = jnp.where(lane_is_a, kn, kn_sw)
                v2 = jnp.where(lane_is_a, vv, vv_sw)
            else:
                k2 = jnp.where(lane_is_a, kn_sw, kn)
                v2 = jnp.where(lane_is_a, vv_sw, vv)
            k2 = k2.astype(BF16)
            v2 = v2.astype(BF16)
            for jp in range(2):
                qb = 2 * g + jp
                qn = norm_rope(q_ref[0, :, qb * PAIR:(qb + 1) * PAIR], cos, sin, qg_ref[...])
                qn = qn * (HEAD_DIM ** -0.5)
                qs = _stack_heads(qn, lane_is_a).astype(BF16)
                sc = _dot_nt(qs, k2)
                sc = jnp.where(valid, sc, -jnp.inf)
                sink = jnp.where(row_is_a, sink_ref[2 * qb], sink_ref[2 * qb + 1])
                mx = jnp.maximum(jnp.max(sc, axis=-1, keepdims=True), sink)
                e = jnp.exp(sc - mx)
                p = e / (jnp.sum(e, axis=-1, keepdims=True) + jnp.exp(sink - mx))
                o2 = _dot(p.astype(BF16), v2)
                o_ref[0, :, qb * PAIR:(qb + 1) * PAIR] = jnp.where(lane_is_a, o2[0:w], o2[w:]).astype(o_ref.dtype)


def _attn(proj3, pos_lanes, inv_freq, q_gain, k_gain, sinks):
    b, s, _ = proj3.shape
    w = WINDOW
    kvw = 2 * ATTN_KV_DIM
    return pl.pallas_call(
        _attn_kernel,
        grid=(b, s // w),
        in_specs=[
            pl.BlockSpec(memory_space=pltpu.SMEM),
            pl.BlockSpec((1, w, BRANCH_DIM), lambda bi, c: (bi, c, COL_Q // BRANCH_DIM)),
            pl.BlockSpec((1, w, kvw), lambda bi, c: (bi, c, COL_KV // kvw)),
            pl.BlockSpec((1, w, kvw), lambda bi, c: (bi, jnp.maximum(c - 1, 0), COL_KV // kvw)),
            pl.BlockSpec((1, w, PAIR), lambda bi, c: (bi, c, 0)),
            pl.BlockSpec((1, w, PAIR), lambda bi, c: (bi, jnp.maximum(c - 1, 0), 0)),
            pl.BlockSpec((1, PAIR), lambda bi, c: (0, 0)),
            pl.BlockSpec((1, PAIR), lambda bi, c: (0, 0)),
            pl.BlockSpec((1, PAIR), lambda bi, c: (0, 0)),
        ],
        out_specs=pl.BlockSpec((1, w, BRANCH_DIM), lambda bi, c: (bi, c, 0)),
        out_shape=jax.ShapeDtypeStruct((b, s, BRANCH_DIM), BF16),
        compiler_params=_cp("parallel", "arbitrary"),
        name="attn",
    )(sinks, proj3, proj3, proj3, pos_lanes, pos_lanes, inv_freq, q_gain, k_gain)


def _merge_kernel(a_ref, b_ref, c_ref, g0_ref, g1_ref, g2_ref, w_ref, o_ref):
    acc = _sigmoid(g0_ref[...]) * _dot(a_ref[...], w_ref[0])
    acc = acc + _sigmoid(g1_ref[...]) * _dot(b_ref[...], w_ref[1])
    acc = acc + _sigmoid(g2_ref[...]) * _dot(c_ref[...], w_ref[2])
    o_ref[...] = acc.astype(o_ref.dtype)


def _merge(a_out, b_out, c_out, proj, w_branch_bf16, tm, tn):
    n = proj.shape[0]
    br = lambda: pl.BlockSpec((tm, BRANCH_DIM), lambda i, j: (i, 0))
    gate = lambda g: pl.BlockSpec((tm, tn), lambda i, j: (i, (COL_G + g * D_MODEL) // tn + j))
    return pl.pallas_call(
        _merge_kernel,
        grid=(n // tm, D_MODEL // tn),
        in_specs=[br(), br(), br(), gate(0), gate(1), gate(2),
                  pl.BlockSpec((N_BRANCHES, BRANCH_DIM, tn), lambda i, j: (0, 0, j))],
        out_specs=pl.BlockSpec((tm, tn), lambda i, j: (i, j)),
        out_shape=jax.ShapeDtypeStruct((n, D_MODEL), BF16),
        compiler_params=_cp("parallel", "arbitrary"),
        name="merge",
    )(a_out, b_out, c_out, proj, proj, proj, w_branch_bf16)


def _out_proj_kernel(m_ref, x_ref, w_ref, g_ref, wr_ref, x1_ref, hn_ref, lg_ref):
    x1 = x_ref[...] + _dot(m_ref[...], w_ref[...])
    x1_ref[...] = x1
    ms = jnp.mean(x1 * x1, axis=-1, keepdims=True)
    hn = x1 * lax.rsqrt(ms + NORM_EPS) * g_ref[...]
    hn_ref[...] = hn
    lg_ref[...] = _dot_nt(wr_ref[...], hn, HI)


def _out_proj(merged, x2d, w_out_bf16, ffn_gain, w_router_t, tm):
    n, d = x2d.shape
    return pl.pallas_call(
        _out_proj_kernel,
        grid=(n // tm,),
        in_specs=[
            pl.BlockSpec((tm, d), lambda i: (i, 0)),
            pl.BlockSpec((tm, d), lambda i: (i, 0)),
            pl.BlockSpec((d, d), lambda i: (0, 0)),
            pl.BlockSpec((1, d), lambda i: (0, 0)),
            pl.BlockSpec((ROUTER_ROWS, d), lambda i: (0, 0)),
        ],
        out_specs=[
            pl.BlockSpec((tm, d), lambda i: (i, 0)),
            pl.BlockSpec((tm, d), lambda i: (i, 0)),
            pl.BlockSpec((ROUTER_ROWS, tm), lambda i: (0, i)),
        ],
        out_shape=[
            jax.ShapeDtypeStruct((n, d), F32),
            jax.ShapeDtypeStruct((n, d), F32),
            jax.ShapeDtypeStruct((ROUTER_ROWS, n), F32),
        ],
        compiler_params=_cp("parallel"),
        name="out_proj",
    )(merged, x2d, w_out_bf16, ffn_gain.reshape(1, d), w_router_t)


def _experts_kernel(be_ref, x_ref, w1_ref, w3_ref, w2_ref, o_ref):
    del be_ref
    xb = x_ref[...].astype(BF16)
    h1 = _dot(xb, w1_ref[0])
    h3 = _dot(xb, w3_ref[0])
    act = (h1 * _sigmoid(h1)) * h3
    o_ref[...] = _dot(act.astype(BF16), w2_ref[0])


def _experts(xg, block_expert, w1, w3, w2):
    n_rows, d = xg.shape
    ff = w1.shape[-1]
    blk = MOE_BLOCK
    return pl.pallas_call(
        _experts_kernel,
        grid_spec=pltpu.PrefetchScalarGridSpec(
            num_scalar_prefetch=1,
            grid=(n_rows // blk,),
            in_specs=[
                pl.BlockSpec((blk, d), lambda i, be: (i, 0)),
                pl.BlockSpec((1, d, ff), lambda i, be: (be[i], 0, 0)),
                pl.BlockSpec((1, d, ff), lambda i, be: (be[i], 0, 0)),
                pl.BlockSpec((1, ff, d), lambda i, be: (be[i], 0, 0)),
            ],
            out_specs=pl.BlockSpec((blk, d), lambda i, be: (i, 0)),
        ),
        out_shape=jax.ShapeDtypeStruct((n_rows, d), F32),
        compiler_params=_cp("arbitrary"),
        name="experts",
    )(block_expert, xg, w1, w3, w2)


def _pack_w_in(w):
    d = w.shape[0]
    z32 = jnp.zeros((d, LORA_PAD - DECAY_LORA), w.dtype)
    c = BRANCH_DIM
    o_wd, o_ad, o_gd = 3 * c, 3 * c + DECAY_LORA, 3 * c + DECAY_LORA + ICLR_LORA
    o_u = o_gd + GATE_LORA
    o_q = o_u + c
    o_k = o_q + c
    o_g = o_k + 2 * ATTN_KV_DIM
    parts = [w[:, :o_wd], w[:, o_wd:o_ad], z32, w[:, o_ad:o_gd], z32, w[:, o_gd:o_u],
             w[:, o_u:o_q], w[:, o_k:o_g], w[:, o_q:o_k], w[:, o_g:]]
    return jnp.concatenate(parts, axis=1).astype(BF16)


def _pack_mu(mu):
    z32 = jnp.zeros((LORA_PAD - DECAY_LORA,), mu.dtype)
    c = BRANCH_DIM
    o_wd, o_ad, o_gd = 3 * c, 3 * c + DECAY_LORA, 3 * c + DECAY_LORA + ICLR_LORA
    return jnp.concatenate([mu[:o_wd], mu[o_wd:o_ad], z32, mu[o_ad:o_gd], z32, mu[o_gd:]]).reshape(1, Z_COLS)


def _pad_rows(w, rows):
    return jnp.concatenate([w, jnp.zeros((rows - w.shape[0], w.shape[1]), w.dtype)], axis=0)


def _router_weights(w_grp, w_exp):
    d = w_grp.shape[0]
    pad = jnp.zeros((d, 8 - N_GROUPS), w_grp.dtype)
    return jnp.concatenate([w_grp, pad, w_exp], axis=1).T


def _route_kernel(lg_ref, bias_ref, ids_ref, gcol_ref, cnt_ref, carry_ref):
    tm = lg_ref.shape[1]

    @pl.when(pl.program_id(0) == 0)
    def _():
        carry_ref[...] = jnp.zeros_like(carry_ref)

    lg = lg_ref[...] + bias_ref[...]
    row8 = lax.broadcasted_iota(jnp.int32, (8, tm), 0)
    row8f = row8.astype(F32)
    neg_inf = -jnp.inf

    def first_argmax(x):
        mx = jnp.max(x, axis=0, keepdims=True)
        return mx, jnp.min(jnp.where(x == mx, row8f, 8.0), axis=0, keepdims=True).astype(jnp.int32)

    grp_logits = jnp.where(row8 < N_GROUPS, lg[0:8], neg_inf)
    gmax, grp = first_argmax(grp_logits)
    p_grp = 1.0 / jnp.sum(jnp.exp(grp_logits - gmax), axis=0, keepdims=True)
    in_grp = lg[8:16]
    for g in range(1, N_GROUPS):
        in_grp = jnp.where(grp == g, lg[8 + 8 * g:16 + 8 * g], in_grp)
    m1, i1 = first_argmax(in_grp)
    rest = jnp.where(row8 == i1, neg_inf, in_grp)
    m2, i2 = first_argmax(rest)
    e2 = jnp.exp(m2 - m1)
    gate1 = p_grp / (1.0 + e2)
    gate2 = p_grp * e2 / (1.0 + e2)
    exp1 = grp * EXPERTS_PER_GROUP + i1
    exp2 = grp * EXPERTS_PER_GROUP + i2

    rows = lax.broadcasted_iota(jnp.int32, (N_EXPERTS, tm), 0)
    hot1 = (rows == exp1).astype(F32)
    hot2 = (rows == exp2).astype(F32)
    cnt = (hot1 + hot2).astype(BF16)
    src = lax.broadcasted_iota(jnp.int32, (tm, tm), 0)
    dst = lax.broadcasted_iota(jnp.int32, (tm, tm), 1)
    before = _dot(cnt, (src < dst).astype(BF16)) + carry_ref[...]
    rank1 = jnp.sum(hot1 * before, axis=0, keepdims=True).astype(jnp.int32)
    rank2 = jnp.sum(hot2 * before, axis=0, keepdims=True).astype(jnp.int32)
    carry_ref[...] += _dot(cnt, jnp.ones((tm, tm), BF16))
    cnt_ref[...] = carry_ref[...]

    ids_ref[...] = jnp.where(row8 == 0, exp1, jnp.where(row8 == 1, exp2,
                             jnp.where(row8 == 2, rank1, jnp.where(row8 == 3, rank2, 0))))
    row128 = lax.broadcasted_iota(jnp.int32, (LANES, tm), 0)
    gates_t = jnp.where(row128 == 0, gate1, jnp.where(row128 == 1, gate2, 0.0))
    gcol_ref[...] = gates_t.T


def _route(logits_t, bias_col, tm):
    n = logits_t.shape[1]
    bias = jnp.broadcast_to(bias_col, (ROUTER_ROWS, tm))
    return pl.pallas_call(
        _route_kernel,
        grid=(n // tm,),
        in_specs=[pl.BlockSpec((ROUTER_ROWS, tm), lambda i: (0, i)),
                  pl.BlockSpec((ROUTER_ROWS, tm), lambda i: (0, 0))],
        out_specs=[pl.BlockSpec((8, tm), lambda i: (0, i)),
                   pl.BlockSpec((tm, LANES), lambda i: (i, 0)),
                   pl.BlockSpec((N_EXPERTS, tm), lambda i: (0, 0))],
        out_shape=[jax.ShapeDtypeStruct((8, n), jnp.int32),
                   jax.ShapeDtypeStruct((n, LANES), F32),
                   jax.ShapeDtypeStruct((N_EXPERTS, tm), F32)],
        scratch_shapes=[pltpu.VMEM((N_EXPERTS, tm), F32)],
        compiler_params=_cp("arbitrary"),
        name="route",
    )(logits_t, bias)


def _row_copy(src_ref, src_row, dst_ref, dst_row, sem):
    return pltpu.make_async_copy(src_ref.at[pl.ds(src_row, 1)], dst_ref.at[pl.ds(dst_row, 1)], sem)


def _scatter_kernel(dest_ref, hn_ref, xg_in_ref, xg_ref, sem):
    del xg_in_ref
    tm = hn_ref.shape[0]
    n = pl.num_programs(0) * tm
    base = pl.program_id(0) * tm

    def issue(r, carry):
        for k in range(TOP_K):
            _row_copy(hn_ref, r, xg_ref, dest_ref[k * n + base + r], sem).start()
        return carry

    lax.fori_loop(0, tm, issue, 0)
    for _ in range(TOP_K):
        pltpu.make_async_copy(hn_ref, xg_ref.at[pl.ds(0, tm)], sem).wait()


def _scatter(dest_flat, hn, n_rows, tm):
    n, d = hn.shape
    return pl.pallas_call(
        _scatter_kernel,
        grid_spec=pltpu.PrefetchScalarGridSpec(
            num_scalar_prefetch=1,
            grid=(n // tm,),
            in_specs=[pl.BlockSpec((tm, d), lambda i, dest: (i, 0)),
                      pl.BlockSpec(memory_space=pl.ANY)],
            out_specs=pl.BlockSpec(memory_space=pl.ANY),
            scratch_shapes=[pltpu.SemaphoreType.DMA],
        ),
        out_shape=jax.ShapeDtypeStruct((n_rows, d), hn.dtype),
        input_output_aliases={2: 0},
        compiler_params=_cp("arbitrary"),
        name="moe_scatter",
    )(dest_flat, hn, jnp.zeros((n_rows, d), hn.dtype))


def _combine_kernel(dest_ref, x_ref, gcol_ref, yg_ref, o_ref, buf_ref, sem):
    tm = x_ref.shape[0]
    n = pl.num_programs(0) * tm
    base = pl.program_id(0) * tm

    def issue(r, carry):
        for k in range(TOP_K):
            _row_copy(yg_ref, dest_ref[k * n + base + r], buf_ref.at[k], r, sem).start()
        return carry

    lax.fori_loop(0, tm, issue, 0)
    for k in range(TOP_K):
        pltpu.make_async_copy(yg_ref.at[pl.ds(0, tm)], buf_ref.at[k], sem).wait()
    g = gcol_ref[...]
    o_ref[...] = x_ref[...] + g[:, 0:1] * buf_ref[0] + g[:, 1:2] * buf_ref[1]


def _combine(dest_flat, x1, gcol, yg, tm):
    n, d = x1.shape
    return pl.pallas_call(
        _combine_kernel,
        grid_spec=pltpu.PrefetchScalarGridSpec(
            num_scalar_prefetch=1,
            grid=(n // tm,),
            in_specs=[pl.BlockSpec((tm, d), lambda i, dest: (i, 0)),
                      pl.BlockSpec((tm, LANES), lambda i, dest: (i, 0)),
                      pl.BlockSpec(memory_space=pl.ANY)],
            out_specs=pl.BlockSpec((tm, d), lambda i, dest: (i, 0)),
            scratch_shapes=[pltpu.VMEM((TOP_K, tm, d), F32), pltpu.SemaphoreType.DMA],
        ),
        out_shape=jax.ShapeDtypeStruct((n, d), F32),
        compiler_params=_cp("arbitrary"),
        name="moe_combine",
    )(dest_flat, x1, gcol, yg)


def _moe(x1, hn, logits_t, bias_col, w1, w3, w2):
    n, d = hn.shape
    blk = MOE_BLOCK
    tm = _pick(n, 512)
    ids, gcol, cnt = _route(logits_t, bias_col, tm)
    counts = cnt[:, 0].astype(jnp.int32)
    padded = (counts + blk - 1) // blk * blk
    pends = jnp.cumsum(padded)
    pstarts = pends - padded
    experts = jnp.arange(N_EXPERTS, dtype=jnp.int32)
    seg_start = jnp.sum(jnp.where(ids[0:TOP_K, :, None] == experts, pstarts, 0), axis=-1)
    dest_flat = (seg_start + ids[TOP_K:2 * TOP_K]).reshape(-1)
    n_rows = n * TOP_K + N_EXPERTS * blk
    block_start = jnp.arange(n_rows // blk, dtype=jnp.int32) * blk
    block_expert = jnp.minimum(jnp.sum((block_start[:, None] >= pends[None, :]).astype(jnp.int32), axis=1),
                               N_EXPERTS - 1)
    xg = _scatter(dest_flat, hn, n_rows, tm)
    yg = _experts(xg, block_expert, w1, w3, w2)
    return _combine(dest_flat, x1, gcol, yg, tm)


def _pick(n, pref):
    t = pref
    while n % t:
        t //= 2
    return t


def _layer(x2d, b, s, pos_lanes, inv_freq, p, rwkv_precision):
    n = x2d.shape[0]
    proj = _norm_matmul(x2d, p['attn_norm'], p['w_in'], _pick(n, 1024), 1024)
    proj3 = proj.reshape(b, s, P_COLS)
    a_out = _rwkv(proj3, p['mu'], p['w0'], p['a0'], p['k_k'], p['k_a'], p['r_k'], p['ln_g'], p['ln_b'],
                  p['w_up'], p['a_up'], p['g_up'], _pick(s, 512), rwkv_precision)
    b_out = _pool(proj3, p['pool_w'], p['pool_scale'], _pick(s, 512))
    c_out = _attn(proj3, pos_lanes, inv_freq, p['q_gain'], p['k_gain'], p['sinks'])
    merged = _merge(a_out.reshape(n, -1), b_out.reshape(n, -1), c_out.reshape(n, -1), proj,
                    p['w_branch'], _pick(n, 1024), 512)
    x1, hn, logits_t = _out_proj(merged, x2d, p['w_out'], p['ffn_norm'], p['w_router_t'], _pick(n, 512))
    return _moe(x1, hn, logits_t, p['router_bias'], p['w1'], p['w3'], p['w2'])


def kernel(x, positions, attn_norm, w_in, tmix_mu, rwkv_w0, rwkv_w_up, rwkv_a0, rwkv_a_up, rwkv_g_up, rwkv_k_k, rwkv_k_a, rwkv_r_k, rwkv_ln_g, rwkv_ln_b, pool_w, pool_scale, q_norm, k_norm, attn_sinks, w_branch, w_out, ffn_norm, router_grp_w, router_grp_b, router_exp_w, router_exp_b, expert_w1, expert_w3, expert_w2):
    b, s, d = x.shape
    n = b * s
    half = HEAD_DIM // 2
    inv_freq = ROPE_THETA ** (-jnp.arange(half, dtype=F32) / half)
    inv_freq = jnp.tile(inv_freq, PAIR // half).reshape(1, PAIR)
    pos_lanes = jnp.broadcast_to(positions.astype(F32)[..., None], (b, s, PAIR))
    x2d = x.reshape(n, d)
    for l in range(w_in.shape[0]):
        p = {
            'attn_norm': attn_norm[l],
            'w_in': _pack_w_in(w_in[l]),
            'mu': _pack_mu(tmix_mu[l]),
            'w0': rwkv_w0[l], 'a0': rwkv_a0[l], 'k_k': rwkv_k_k[l], 'k_a': rwkv_k_a[l],
            'r_k': rwkv_r_k[l].reshape(-1), 'ln_g': rwkv_ln_g[l], 'ln_b': rwkv_ln_b[l],
            'w_up': _pad_rows(rwkv_w_up[l], LORA_PAD), 'a_up': _pad_rows(rwkv_a_up[l], LORA_PAD),
            'g_up': rwkv_g_up[l],
            'pool_w': pool_w[l].astype(BF16), 'pool_scale': pool_scale[l],
            'q_gain': jnp.tile(q_norm[l], PAIR // HEAD_DIM).reshape(1, PAIR),
            'k_gain': jnp.tile(k_norm[l], PAIR // HEAD_DIM).reshape(1, PAIR),
            'sinks': attn_sinks[l],
            'w_branch': w_branch[l].astype(BF16), 'w_out': w_out[l].astype(BF16),
            'ffn_norm': ffn_norm[l],
            'w_router_t': _router_weights(router_grp_w[l], router_exp_w[l]),
            'router_bias': jnp.concatenate([router_grp_b[l], jnp.zeros((8 - N_GROUPS,), F32),
                                            router_exp_b[l]]).reshape(ROUTER_ROWS, 1),
            'w1': expert_w1[l].astype(BF16), 'w3': expert_w3[l].astype(BF16), 'w2': expert_w2[l].astype(BF16),
        }
        x2d = _layer(x2d, b, s, pos_lanes, inv_freq, p, None)
    return x2d.reshape(b, s, d)
```

```python
import functools

import numpy as np
import jax
import jax.numpy as jnp
from jax import lax
from jax.experimental import pallas as pl
from jax.experimental.pallas import tpu as pltpu

F32 = jnp.float32
BF16 = jnp.bfloat16
HI = lax.Precision.HIGHEST

D_MODEL = 2048
HEAD_DIM = 64
BRANCH_DIM = D_MODEL // 2
DECAY_LORA = 96
ICLR_LORA = 96
GATE_LORA = 256
LORA_PAD = 128
POOL_WINDOWS = (2, 4, 8, 16)
POOL_GROUP_DIM = BRANCH_DIM // len(POOL_WINDOWS)
POOL_HALO = 16
ATTN_Q_HEADS = BRANCH_DIM // HEAD_DIM
ATTN_KV_HEADS = 4
ATTN_KV_DIM = ATTN_KV_HEADS * HEAD_DIM
WINDOW = 128
ROPE_THETA = 10000.0
N_BRANCHES = 3
N_GROUPS = 4
EXPERTS_PER_GROUP = 8
N_EXPERTS = N_GROUPS * EXPERTS_PER_GROUP
TOP_K = 2
EXPERT_FF = 512
NORM_EPS = 1e-6
GN_EPS = 64e-5

LANES = 128
PAIR = 2 * HEAD_DIM
CHUNK = 64
ROUTER_ROWS = 40
MOE_BLOCK = 256

Z_COLS = 3 * BRANCH_DIM + 2 * LORA_PAD + GATE_LORA
COL_U = Z_COLS
COL_KV = COL_U + BRANCH_DIM
COL_Q = COL_KV + 2 * ATTN_KV_DIM
COL_G = COL_Q + BRANCH_DIM
P_COLS = COL_G + N_BRANCHES * D_MODEL

VMEM_LIMIT = 56 * 1024 * 1024


def _cp(*sem):
    return pltpu.CompilerParams(dimension_semantics=sem, vmem_limit_bytes=VMEM_LIMIT)


def _dot(a, b, precision=None):
    return jnp.dot(a, b, preferred_element_type=F32, precision=precision)


def _dot_nt(a, b, precision=None):
    return lax.dot_general(a, b, (((1,), (1,)), ((), ())), preferred_element_type=F32, precision=precision)


def _sigmoid(x):
    return 1.0 / (1.0 + jnp.exp(-x))


def _split_bf16(x):
    hi = x.astype(BF16)
    return hi, (x - hi.astype(F32)).astype(BF16)


def _dot_split_lhs(x, m):
    m = m.astype(BF16)
    return _dot(jnp.concatenate(_split_bf16(x), axis=1), jnp.concatenate([m, m], axis=0))


def _dot_split3(x, w):
    xh, xl = _split_bf16(x)
    wh, wl = _split_bf16(w)
    return _dot(jnp.concatenate([xh, xl, xh], axis=1), jnp.concatenate([wh, wh, wl], axis=0))


def _head_block_diag(scale):
    r = lax.broadcasted_iota(jnp.int32, (PAIR, PAIR), 0) // HEAD_DIM
    c = lax.broadcasted_iota(jnp.int32, (PAIR, PAIR), 1) // HEAD_DIM
    return jnp.where(r == c, scale, 0.0).astype(F32)


def _norm_matmul_kernel(x_ref, g_ref, w_ref, o_ref, h_ref):
    @pl.when(pl.program_id(1) == 0)
    def _():
        x = x_ref[...]
        ms = jnp.mean(x * x, axis=-1, keepdims=True)
        h_ref[...] = (x * lax.rsqrt(ms + NORM_EPS) * g_ref[...]).astype(BF16)

    o_ref[...] = _dot(h_ref[...], w_ref[...])


def _norm_matmul(x2d, gain, w_bf16, tm, tn):
    n, d = x2d.shape
    cols = w_bf16.shape[1]
    return pl.pallas_call(
        _norm_matmul_kernel,
        grid=(n // tm, cols // tn),
        in_specs=[
            pl.BlockSpec((tm, d), lambda i, j: (i, 0)),
            pl.BlockSpec((1, d), lambda i, j: (0, 0)),
            pl.BlockSpec((d, tn), lambda i, j: (0, j)),
        ],
        out_specs=pl.BlockSpec((tm, tn), lambda i, j: (i, j)),
        out_shape=jax.ShapeDtypeStruct((n, cols), F32),
        scratch_shapes=[pltpu.VMEM((tm, d), BF16)],
        compiler_params=_cp("arbitrary", "arbitrary"),
        name="norm_matmul",
    )(x2d, gain.reshape(1, d), w_bf16)


def _stack_heads(x, lane_is_a):
    return jnp.concatenate([jnp.where(lane_is_a, x, 0.0), jnp.where(lane_is_a, 0.0, x)], axis=0)


def _rwkv_kernel(zr_ref, zk_ref, zv_ref, zl_ref, mur_ref, muk_ref, muv_ref, mul_ref,
                 w0_ref, a0_ref, kkg_ref, ka_ref, rk_ref, lng_ref, lnb_ref,
                 wup_ref, aup_ref, gup_ref, o_ref,
                 h_ref, cr_ref, ck_ref, cv_ref, cl_ref, *, precision):
    t_rows = zr_ref.shape[1]

    @pl.when(pl.program_id(2) == 0)
    def _():
        h_ref[...] = jnp.zeros_like(h_ref)
        cr_ref[...] = jnp.zeros_like(cr_ref)
        ck_ref[...] = jnp.zeros_like(ck_ref)
        cv_ref[...] = jnp.zeros_like(cv_ref)
        cl_ref[...] = jnp.zeros_like(cl_ref)

    row = lax.broadcasted_iota(jnp.int32, (t_rows, 1), 0)

    def shifted(z_ref, carry_ref, mu_ref):
        z = z_ref[0]
        prev = jnp.where(row == 0, carry_ref[...], pltpu.roll(z, 1, 0))
        carry_ref[...] = z_ref[0, t_rows - 1:t_rows, :]
        return z + (prev - z) * mu_ref[...]

    r = shifted(zr_ref, cr_ref, mur_ref)
    k = shifted(zk_ref, ck_ref, muk_ref)
    v = shifted(zv_ref, cv_ref, muv_ref)
    zl = shifted(zl_ref, cl_ref, mul_ref)
    wd = zl[:, 0:LORA_PAD]
    ad = zl[:, LORA_PAD:2 * LORA_PAD]
    gd = zl[:, 2 * LORA_PAD:]

    head_sum = _head_block_diag(1.0)
    head_avg = _head_block_diag(1.0 / HEAD_DIM)

    w_pre = w0_ref[...] + _dot_split3(jnp.tanh(wd), wup_ref[...])
    neg = -w_pre
    softplus = jnp.maximum(neg, 0.0) + jnp.log(1.0 + jnp.exp(-jnp.abs(neg)))
    log_decay = -jnp.exp(-softplus - 0.5)
    a = _sigmoid(a0_ref[...] + _dot_split3(ad, aup_ref[...]))
    gate = _dot_split3(_sigmoid(gd), gup_ref[...])
    kk = k * kkg_ref[...]
    kk = kk * lax.rsqrt(jnp.maximum(_dot_split_lhs(kk * kk, head_sum), 1e-24))
    k = k * (1.0 + (a - 1.0) * ka_ref[...])
    bonus = _dot_split_lhs(r * k * rk_ref[...], head_sum) * v

    c = CHUNK
    ri = lax.broadcasted_iota(jnp.int32, (c, c), 0)
    ci = lax.broadcasted_iota(jnp.int32, (c, c), 1)
    tri_incl = (ci <= ri).astype(F32)
    tri2 = jnp.concatenate([tri_incl, tri_incl], axis=1).astype(BF16)
    lane_is_a = lax.broadcasted_iota(jnp.int32, (1, PAIR), 1) < HEAD_DIM
    rs = lax.broadcasted_iota(jnp.int32, (PAIR, PAIR), 0)
    cs = lax.broadcasted_iota(jnp.int32, (PAIR, PAIR), 1)
    strict = (cs % c) < (rs % c)
    incl = (cs % c) <= (rs % c)
    eye = rs == cs
    eye_f = eye.astype(F32)
    if precision is None:
        dot = lambda x, y: _dot(x.astype(BF16), y.astype(BF16))
        dot_nt = lambda x, y: _dot_nt(x.astype(BF16), y.astype(BF16))
    else:
        dot = functools.partial(_dot, precision=precision)
        dot_nt = functools.partial(_dot_nt, precision=precision)

    n_ch = t_rows // c
    sls = [slice(ch * c, (ch + 1) * c) for ch in range(n_ch)]
    cums = [_dot(tri2, jnp.concatenate(_split_bf16(log_decay[sl]), axis=0)) for sl in sls]
    pre = []
    for sl, cum in zip(sls, cums):
        lw = log_decay[sl]
        tot = cum[c - 1:c, :]
        p_in = jnp.exp(cum)
        p_ex = jnp.exp(cum - lw)
        q_inv = jnp.exp(-cum)
        q_end = jnp.exp(tot - cum)
        kk_c, a_c, k_c = kk[sl], a[sl], k[sl]
        beta = kk_c * a_c
        pre.append(dict(
            tot=tot,
            at_s=_stack_heads(-kk_c * p_ex, lane_is_a),
            rt_s=_stack_heads(r[sl] * p_in, lane_is_a),
            bh_s=_stack_heads(beta * q_inv, lane_is_a),
            kh_s=_stack_heads(k_c * q_inv, lane_is_a),
            be_s=_stack_heads(beta * q_end, lane_is_a),
            ke_s=_stack_heads(k_c * q_end, lane_is_a),
            v_s=_stack_heads(v[sl], lane_is_a)))
    gms = [dot_nt(jnp.concatenate([d['at_s'], d['rt_s']], axis=0),
                  jnp.concatenate([d['bh_s'], d['kh_s']], axis=0)) for d in pre]
    a_ab = [jnp.where(strict, g[0:PAIR, 0:PAIR], 0.0) for g in gms]
    a_ak = [jnp.where(strict, g[0:PAIR, PAIR:], 0.0) for g in gms]
    a_rb = [jnp.where(incl, g[PAIR:, 0:PAIR], 0.0) for g in gms]
    a_rk = [jnp.where(incl, g[PAIR:, PAIR:], 0.0) for g in gms]
    pw = [dot(x, x) for x in a_ab]
    akv = [dot(x, d['v_s']) for x, d in zip(a_ak, pre)]
    inv = [eye_f + x for x in a_ab]
    inv = [x + dot(x, p) for x, p in zip(inv, pw)]
    for _ in range(4):
        pw = [dot(p, p) for p in pw]
        inv = [x + dot(x, p) for x, p in zip(inv, pw)]
    wu = [dot(x, jnp.concatenate([d['at_s'], y], axis=1)) for x, d, y in zip(inv, pre, akv)]
    rb_wu = [dot(x, y) for x, y in zip(a_rb, wu)]
    be_wu = [dot(d['be_s'].T, y) for d, y in zip(pre, wu)]
    rkv = [dot(x, d['v_s']) for x, d in zip(a_rk, pre)]
    kev = [dot(d['ke_s'].T, d['v_s']) for d in pre]
    chunks = []
    for i in range(n_ch):
        rt2 = pre[i]['rt_s'] + rb_wu[i][:, 0:PAIR]
        y0 = rb_wu[i][:, PAIR:] + rkv[i]
        m = jnp.where(eye, jnp.exp(pre[i]['tot']), 0.0) + be_wu[i][:, 0:PAIR]
        h0 = be_wu[i][:, PAIR:] + kev[i]
        chunks.append((sls[i], rt2, y0, m, h0))

    h = h_ref[...]
    for sl, rt2, y0, m, h0 in chunks:
        y_s = dot(rt2, h) + y0
        h = dot(m, h) + h0

        y = y_s[0:c] + y_s[c:]
        mean = _dot_split_lhs(y, head_avg)
        yc = y - mean
        var = _dot_split_lhs(yc * yc, head_avg)
        gn = yc * lax.rsqrt(var + GN_EPS)
        out = (gn * lng_ref[...] + lnb_ref[...] + bonus[sl]) * gate[sl]
        o_ref[0, sl, :] = out.astype(o_ref.dtype)
    h_ref[...] = h


def _rwkv(proj3, mu_pack, w0, a0, k_k, k_a, r_k, ln_g, ln_b, w_up, a_up, g_up, t_rows, precision):
    b, s, _ = proj3.shape
    n_pairs = BRANCH_DIM // PAIR
    vec = lambda off: pl.BlockSpec((1, PAIR), lambda bi, p, c: (0, off + p))
    zcol = lambda off: pl.BlockSpec((1, t_rows, PAIR), lambda bi, p, c: (bi, c, off + p))
    lora_w = 2 * LORA_PAD + GATE_LORA
    lora_blk = (3 * BRANCH_DIM) // lora_w
    row = lambda x: x.reshape(1, BRANCH_DIM)
    kernel = functools.partial(_rwkv_kernel, precision=precision)
    return pl.pallas_call(
        kernel,
        grid=(b, n_pairs, s // t_rows),
        in_specs=[
            zcol(0), zcol(n_pairs), zcol(2 * n_pairs),
            pl.BlockSpec((1, t_rows, lora_w), lambda bi, p, c: (bi, c, lora_blk)),
            vec(0), vec(n_pairs), vec(2 * n_pairs),
            pl.BlockSpec((1, lora_w), lambda bi, p, c: (0, lora_blk)),
            vec(0), vec(0), vec(0), vec(0), vec(0), vec(0), vec(0),
            pl.BlockSpec((LORA_PAD, PAIR), lambda bi, p, c: (0, p)),
            pl.BlockSpec((LORA_PAD, PAIR), lambda bi, p, c: (0, p)),
            pl.BlockSpec((GATE_LORA, PAIR), lambda bi, p, c: (0, p)),
        ],
        out_specs=pl.BlockSpec((1, t_rows, PAIR), lambda bi, p, c: (bi, c, p)),
        out_shape=jax.ShapeDtypeStruct((b, s, BRANCH_DIM), BF16),
        scratch_shapes=[
            pltpu.VMEM((PAIR, PAIR), F32),
            pltpu.VMEM((1, PAIR), F32), pltpu.VMEM((1, PAIR), F32), pltpu.VMEM((1, PAIR), F32),
            pltpu.VMEM((1, lora_w), F32),
        ],
        compiler_params=_cp("arbitrary", "arbitrary", "arbitrary"),
        name="rwkv",
    )(proj3, proj3, proj3, proj3, mu_pack, mu_pack, mu_pack, mu_pack,
      row(w0), row(a0), row(k_k), row(k_a), row(r_k), row(ln_g), row(ln_b), w_up, a_up, g_up)


def _pool_kernel(u0_ref, u1_ref, u2_ref, u3_ref, h0_ref, h1_ref, h2_ref, h3_ref, w_ref, sc_ref, o_ref, buf_ref):
    t_rows = u0_ref.shape[1]
    first = pl.program_id(1) == 0
    t = pl.program_id(1) * t_rows + lax.broadcasted_iota(jnp.int32, (t_rows, 1), 0)
    gd = POOL_GROUP_DIM
    for gi, (m, u_ref, halo_ref) in enumerate(zip(POOL_WINDOWS, (u0_ref, u1_ref, u2_ref, u3_ref),
                                                  (h0_ref, h1_ref, h2_ref, h3_ref))):
        u = u_ref[0]
        buf_ref[0:POOL_HALO, :] = jnp.where(first, 0.0, halo_ref[0])
        buf_ref[POOL_HALO:, :] = u
        acc = u
        for sft in range(1, m):
            acc = acc + buf_ref[pl.ds(POOL_HALO - sft, t_rows), :]
        count = jnp.minimum(t + 1, m).astype(F32)
        pooled = acc / count - u
        y = _dot(pooled.astype(BF16), w_ref[gi])
        o_ref[0, :, gi * gd:(gi + 1) * gd] = (y * sc_ref[:, gi * gd:(gi + 1) * gd]).astype(o_ref.dtype)


def _pool(proj3, pool_w_bf16, pool_scale, t_rows):
    b, s, _ = proj3.shape
    gd = POOL_GROUP_DIM
    ublk = COL_U // gd
    hb = t_rows // POOL_HALO
    u_spec = lambda gi: pl.BlockSpec((1, t_rows, gd), lambda bi, c: (bi, c, ublk + gi))
    h_spec = lambda gi: pl.BlockSpec((1, POOL_HALO, gd), lambda bi, c: (bi, jnp.maximum(c * hb - 1, 0), ublk + gi))
    return pl.pallas_call(
        _pool_kernel,
        grid=(b, s // t_rows),
        in_specs=[u_spec(0), u_spec(1), u_spec(2), u_spec(3), h_spec(0), h_spec(1), h_spec(2), h_spec(3),
                  pl.BlockSpec((len(POOL_WINDOWS), gd, gd), lambda bi, c: (0, 0, 0)),
                  pl.BlockSpec((1, BRANCH_DIM), lambda bi, c: (0, 0))],
        out_specs=pl.BlockSpec((1, t_rows, BRANCH_DIM), lambda bi, c: (bi, c, 0)),
        out_shape=jax.ShapeDtypeStruct((b, s, BRANCH_DIM), BF16),
        scratch_shapes=[pltpu.VMEM((t_rows + POOL_HALO, gd), F32)],
        compiler_params=_cp("parallel", "arbitrary"),
        name="pool",
    )(proj3, proj3, proj3, proj3, proj3, proj3, proj3, proj3, pool_w_bf16, pool_scale.reshape(1, BRANCH_DIM))


def _rope_table_kernel(pos_ref, invf_ref, cos_ref, sin_ref):
    ang = pos_ref[...] * invf_ref[...]
    lane = lax.broadcasted_iota(jnp.int32, (1, PAIR), 1)
    rope_lo = (lane % HEAD_DIM) < (HEAD_DIM // 2)
    cos_ref[...] = jnp.cos(ang)
    sin_ref[...] = jnp.where(rope_lo, -jnp.sin(ang), jnp.sin(ang))


def _rope_tables(pos_lanes, inv_freq, tm):
    n = pos_lanes.shape[0]
    blk = pl.BlockSpec((tm, PAIR), lambda i: (i, 0))
    return pl.pallas_call(
        _rope_table_kernel,
        grid=(n // tm,),
        in_specs=[blk, pl.BlockSpec((1, PAIR), lambda i: (0, 0))],
        out_specs=[blk, blk],
        out_shape=[jax.ShapeDtypeStruct((n, PAIR), F32), jax.ShapeDtypeStruct((n, PAIR), F32)],
        compiler_params=_cp("parallel"),
        name="rope_tables",
    )(pos_lanes, inv_freq)


def _attn_kernel(sink_ref, q_ref, kv_ref, kvh_ref, cos_ref, sin_ref, cosh_ref, sinh_ref, qg_ref, kg_ref, o_ref):
    w = WINDOW
    tq = q_ref.shape[1]
    not_first = pl.program_id(1) > 0
    head_avg = _head_block_diag(1.0 / HEAD_DIM)
    lane = lax.broadcasted_iota(jnp.int32, (1, PAIR), 1)
    lane_is_a = lane < HEAD_DIM
    rope_lo = (lane % HEAD_DIM) < (HEAD_DIM // 2)

    def norm_rope(x, cos, sin, gain):
        ms = _dot_split_lhs(x * x, head_avg)
        x = x * lax.rsqrt(ms + NORM_EPS) * gain
        rot = jnp.where(rope_lo, pltpu.roll(x, PAIR - HEAD_DIM // 2, 1), pltpu.roll(x, HEAD_DIM // 2, 1))
        return x * cos + rot * sin

    cos, sin = cos_ref[0], sin_ref[0]
    cos_k = jnp.concatenate([cosh_ref[0], cos], axis=0)
    sin_k = jnp.concatenate([sinh_ref[0], sin], axis=0)
    kv = jnp.concatenate([kvh_ref[0], kv_ref[0]], axis=0)

    qi = lax.broadcasted_iota(jnp.int32, (2 * w, 2 * w), 0) % w
    kj = lax.broadcasted_iota(jnp.int32, (2 * w, 2 * w), 1)
    rel = kj - qi
    band = (rel >= 1) & (rel <= w)
    band_first = band & ((kj >= w) | not_first)
    row_is_a = lax.broadcasted_iota(jnp.int32, (2 * w, 1), 0) < w

    for kb in range(ATTN_KV_DIM // PAIR):
        kn = norm_rope(kv[:, kb * PAIR:(kb + 1) * PAIR], cos_k, sin_k, kg_ref[...])
        vv = kv[:, ATTN_KV_DIM + kb * PAIR:ATTN_KV_DIM + (kb + 1) * PAIR]
        kn_sw = pltpu.roll(kn, HEAD_DIM, 1)
        vv_sw = pltpu.roll(vv, HEAD_DIM, 1)
        for half in range(2):
            g = 2 * kb + half
            if half == 0:
                k2 = jnp.where(lane_is_a, kn, kn_sw)
                v2 = jnp.where(lane_is_a, vv, vv_sw)
            else:
                k2 = jnp.where(lane_is_a, kn_sw, kn)
                v2 = jnp.where(lane_is_a, vv_sw, vv)
            k2 = k2.astype(BF16)
            v2 = v2.astype(BF16)
            for jp in range(2):
                qb = 2 * g + jp
                qn = norm_rope(q_ref[0, :, qb * PAIR:(qb + 1) * PAIR], cos, sin, qg_ref[...])
                qn = qn * (HEAD_DIM ** -0.5)
                sink = jnp.where(row_is_a, sink_ref[2 * qb], sink_ref[2 * qb + 1])
                for sb in range(tq // w):
                    qs = _stack_heads(qn[sb * w:(sb + 1) * w], lane_is_a).astype(BF16)
                    sc = _dot_nt(qs, k2[sb * w:(sb + 2) * w])
                    sc = jnp.where(band_first if sb == 0 else band, sc, -jnp.inf)
                    mx = jnp.maximum(jnp.max(sc, axis=-1, keepdims=True), sink)
                    e = jnp.exp(sc - mx)
                    inv_den = 1.0 / (jnp.sum(e, axis=-1, keepdims=True) + jnp.exp(sink - mx))
                    o2 = _dot((e * inv_den).astype(BF16), v2[sb * w:(sb + 2) * w])
                    o_ref[0, sb * w:(sb + 1) * w, qb * PAIR:(qb + 1) * PAIR] = (
                        jnp.where(lane_is_a, o2[0:w], o2[w:]).astype(o_ref.dtype))


def _attn(proj3, cos_tab, sin_tab, q_gain, k_gain, sinks, tq):
    b, s, _ = proj3.shape
    w = WINDOW
    kvw = 2 * ATTN_KV_DIM
    hb = tq // w
    cur = lambda width, col: pl.BlockSpec((1, tq, width), lambda bi, c: (bi, c, col))
    halo = lambda width, col: pl.BlockSpec((1, w, width), lambda bi, c: (bi, jnp.maximum(c * hb - 1, 0), col))
    vec = pl.BlockSpec((1, PAIR), lambda bi, c: (0, 0))
    return pl.pallas_call(
        _attn_kernel,
        grid=(b, s // tq),
        in_specs=[
            pl.BlockSpec(memory_space=pltpu.SMEM),
            cur(BRANCH_DIM, COL_Q // BRANCH_DIM), cur(kvw, COL_KV // kvw), halo(kvw, COL_KV // kvw),
            cur(PAIR, 0), cur(PAIR, 0), halo(PAIR, 0), halo(PAIR, 0), vec, vec,
        ],
        out_specs=pl.BlockSpec((1, tq, BRANCH_DIM), lambda bi, c: (bi, c, 0)),
        out_shape=jax.ShapeDtypeStruct((b, s, BRANCH_DIM), BF16),
        compiler_params=_cp("parallel", "arbitrary"),
        name="attn",
    )(sinks, proj3, proj3, proj3, cos_tab, sin_tab, cos_tab, sin_tab, q_gain, k_gain)


def _merge_kernel(a_ref, b_ref, c_ref, g0_ref, g1_ref, g2_ref, w_ref, o_ref):
    acc = _sigmoid(g0_ref[...]) * _dot(a_ref[...], w_ref[0])
    acc = acc + _sigmoid(g1_ref[...]) * _dot(b_ref[...], w_ref[1])
    acc = acc + _sigmoid(g2_ref[...]) * _dot(c_ref[...], w_ref[2])
    o_ref[...] = acc.astype(o_ref.dtype)


def _merge(a_out, b_out, c_out, proj, w_branch_bf16, tm, tn):
    n = proj.shape[0]
    br = lambda: pl.BlockSpec((tm, BRANCH_DIM), lambda i, j: (i, 0))
    gate = lambda g: pl.BlockSpec((tm, tn), lambda i, j: (i, (COL_G + g * D_MODEL) // tn + j))
    return pl.pallas_call(
        _merge_kernel,
        grid=(n // tm, D_MODEL // tn),
        in_specs=[br(), br(), br(), gate(0), gate(1), gate(2),
                  pl.BlockSpec((N_BRANCHES, BRANCH_DIM, tn), lambda i, j: (0, 0, j))],
        out_specs=pl.BlockSpec((tm, tn), lambda i, j: (i, j)),
        out_shape=jax.ShapeDtypeStruct((n, D_MODEL), BF16),
        compiler_params=_cp("parallel", "arbitrary"),
        name="merge",
    )(a_out, b_out, c_out, proj, proj, proj, w_branch_bf16)


def _out_proj_kernel(m_ref, x_ref, w_ref, g_ref, wr_ref, x1_ref, hn_ref, lg_ref):
    x1 = x_ref[...] + _dot(m_ref[...], w_ref[...])
    x1_ref[...] = x1
    ms = jnp.mean(x1 * x1, axis=-1, keepdims=True)
    hn = x1 * lax.rsqrt(ms + NORM_EPS) * g_ref[...]
    hn_ref[...] = hn
    hh, hl = _split_bf16(hn)
    wh, wl = _split_bf16(wr_ref[...])
    lg_ref[...] = _dot_nt(wh, hh) + _dot_nt(wh, hl) + _dot_nt(wl, hh)


def _out_proj(merged, x2d, w_out_bf16, ffn_gain, w_router_t, tm):
    n, d = x2d.shape
    return pl.pallas_call(
        _out_proj_kernel,
        grid=(n // tm,),
        in_specs=[
            pl.BlockSpec((tm, d), lambda i: (i, 0)),
            pl.BlockSpec((tm, d), lambda i: (i, 0)),
            pl.BlockSpec((d, d), lambda i: (0, 0)),
            pl.BlockSpec((1, d), lambda i: (0, 0)),
            pl.BlockSpec((ROUTER_ROWS, d), lambda i: (0, 0)),
        ],
        out_specs=[
            pl.BlockSpec((tm, d), lambda i: (i, 0)),
            pl.BlockSpec((tm, d), lambda i: (i, 0)),
            pl.BlockSpec((ROUTER_ROWS, tm), lambda i: (0, i)),
        ],
        out_shape=[
            jax.ShapeDtypeStruct((n, d), F32),
            jax.ShapeDtypeStruct((n, d), F32),
            jax.ShapeDtypeStruct((ROUTER_ROWS, n), F32),
        ],
        compiler_params=_cp("parallel"),
        name="out_proj",
    )(merged, x2d, w_out_bf16, ffn_gain.reshape(1, d), w_router_t)


def _experts_kernel(be_ref, nu_ref, x_ref, w1_ref, w3_ref, w2_ref, o_ref, w1b_ref, w3b_ref, w2b_ref):
    i = pl.program_id(0)
    new_expert = (i == 0) | (be_ref[i] != be_ref[jnp.maximum(i - 1, 0)])

    @pl.when(new_expert)
    def _():
        w1b_ref[...] = w1_ref[0].astype(BF16)
        w3b_ref[...] = w3_ref[0].astype(BF16)
        w2b_ref[...] = w2_ref[0].astype(BF16)

    used = i < nu_ref[0]

    @pl.when(used)
    def _():
        xb = x_ref[...].astype(BF16)
        h1 = _dot(xb, w1b_ref[...])
        h3 = _dot(xb, w3b_ref[...])
        act = (h1 * _sigmoid(h1)) * h3
        o_ref[...] = _dot(act.astype(BF16), w2b_ref[...])

    @pl.when(jnp.logical_not(used))
    def _():
        o_ref[...] = jnp.zeros_like(o_ref)


def _experts(xg, block_expert, n_used, w1, w3, w2):
    n_rows, d = xg.shape
    ff = w1.shape[-1]
    blk = MOE_BLOCK
    return pl.pallas_call(
        _experts_kernel,
        grid_spec=pltpu.PrefetchScalarGridSpec(
            num_scalar_prefetch=2,
            grid=(n_rows // blk,),
            in_specs=[
                pl.BlockSpec((blk, d), lambda i, be, nu: (jnp.minimum(i, nu[0] - 1), 0)),
                pl.BlockSpec((1, d, ff), lambda i, be, nu: (be[i], 0, 0)),
                pl.BlockSpec((1, d, ff), lambda i, be, nu: (be[i], 0, 0)),
                pl.BlockSpec((1, ff, d), lambda i, be, nu: (be[i], 0, 0)),
            ],
            out_specs=pl.BlockSpec((blk, d), lambda i, be, nu: (i, 0)),
            scratch_shapes=[pltpu.VMEM((d, ff), BF16), pltpu.VMEM((d, ff), BF16), pltpu.VMEM((ff, d), BF16)],
        ),
        out_shape=jax.ShapeDtypeStruct((n_rows, d), F32),
        compiler_params=_cp("arbitrary"),
        name="experts",
    )(block_expert, n_used, xg, w1, w3, w2)


def _pack_w_in(w):
    d = w.shape[0]
    w = w.astype(BF16)
    z32 = jnp.zeros((d, LORA_PAD - DECAY_LORA), w.dtype)
    c = BRANCH_DIM
    o_wd, o_ad, o_gd = 3 * c, 3 * c + DECAY_LORA, 3 * c + DECAY_LORA + ICLR_LORA
    o_u = o_gd + GATE_LORA
    o_q = o_u + c
    o_k = o_q + c
    o_g = o_k + 2 * ATTN_KV_DIM
    parts = [w[:, :o_wd], w[:, o_wd:o_ad], z32, w[:, o_ad:o_gd], z32, w[:, o_gd:o_u],
             w[:, o_u:o_q], w[:, o_k:o_g], w[:, o_q:o_k], w[:, o_g:]]
    return jnp.concatenate(parts, axis=1)


def _pack_mu(mu):
    z32 = jnp.zeros((LORA_PAD - DECAY_LORA,), mu.dtype)
    c = BRANCH_DIM
    o_wd, o_ad, o_gd = 3 * c, 3 * c + DECAY_LORA, 3 * c + DECAY_LORA + ICLR_LORA
    return jnp.concatenate([mu[:o_wd], mu[o_wd:o_ad], z32, mu[o_ad:o_gd], z32, mu[o_gd:]]).reshape(1, Z_COLS)


def _pad_rows(w, rows):
    return jnp.concatenate([w, jnp.zeros((rows - w.shape[0], w.shape[1]), w.dtype)], axis=0)


def _router_weights(w_grp, w_exp):
    d = w_grp.shape[0]
    pad = jnp.zeros((d, 8 - N_GROUPS), w_grp.dtype)
    return jnp.concatenate([w_grp, pad, w_exp], axis=1).T


def _route_kernel(lg_ref, bias_ref, ids_ref, gcol_ref, cnt_ref, carry_ref):
    tm = lg_ref.shape[1]

    @pl.when(pl.program_id(0) == 0)
    def _():
        carry_ref[...] = jnp.zeros_like(carry_ref)

    lg = lg_ref[...] + bias_ref[...]
    row8 = lax.broadcasted_iota(jnp.int32, (8, tm), 0)
    row8f = row8.astype(F32)
    neg_inf = -jnp.inf

    def first_argmax(x):
        mx = jnp.max(x, axis=0, keepdims=True)
        return mx, jnp.min(jnp.where(x == mx, row8f, 8.0), axis=0, keepdims=True).astype(jnp.int32)

    grp_logits = jnp.where(row8 < N_GROUPS, lg[0:8], neg_inf)
    gmax, grp = first_argmax(grp_logits)
    p_grp = 1.0 / jnp.sum(jnp.exp(grp_logits - gmax), axis=0, keepdims=True)
    in_grp = lg[8:16]
    for g in range(1, N_GROUPS):
        in_grp = jnp.where(grp == g, lg[8 + 8 * g:16 + 8 * g], in_grp)
    m1, i1 = first_argmax(in_grp)
    rest = jnp.where(row8 == i1, neg_inf, in_grp)
    m2, i2 = first_argmax(rest)
    e2 = jnp.exp(m2 - m1)
    gate1 = p_grp / (1.0 + e2)
    gate2 = p_grp * e2 / (1.0 + e2)
    exp1 = grp * EXPERTS_PER_GROUP + i1
    exp2 = grp * EXPERTS_PER_GROUP + i2

    rows = lax.broadcasted_iota(jnp.int32, (N_EXPERTS, tm), 0)
    hot1 = (rows == exp1).astype(F32)
    hot2 = (rows == exp2).astype(F32)
    cnt = (hot1 + hot2).astype(BF16)
    src = lax.broadcasted_iota(jnp.int32, (tm, tm), 0)
    dst = lax.broadcasted_iota(jnp.int32, (tm, tm), 1)
    before = _dot(cnt, (src < dst).astype(BF16)) + carry_ref[...]
    rank1 = jnp.sum(hot1 * before, axis=0, keepdims=True).astype(jnp.int32)
    rank2 = jnp.sum(hot2 * before, axis=0, keepdims=True).astype(jnp.int32)
    carry_ref[...] += _dot(cnt, jnp.ones((tm, tm), BF16))
    cnt_ref[...] = carry_ref[...]

    ids_ref[...] = jnp.where(row8 == 0, exp1, jnp.where(row8 == 1, exp2,
                             jnp.where(row8 == 2, rank1, jnp.where(row8 == 3, rank2, 0))))
    row128 = lax.broadcasted_iota(jnp.int32, (LANES, tm), 0)
    gates_t = jnp.where(row128 == 0, gate1, jnp.where(row128 == 1, gate2, 0.0))
    gcol_ref[...] = gates_t.T


def _route(logits_t, bias_col, tm):
    n = logits_t.shape[1]
    bias = jnp.broadcast_to(bias_col, (ROUTER_ROWS, tm))
    return pl.pallas_call(
        _route_kernel,
        grid=(n // tm,),
        in_specs=[pl.BlockSpec((ROUTER_ROWS, tm), lambda i: (0, i)),
                  pl.BlockSpec((ROUTER_ROWS, tm), lambda i: (0, 0))],
        out_specs=[pl.BlockSpec((8, tm), lambda i: (0, i)),
                   pl.BlockSpec((tm, LANES), lambda i: (i, 0)),
                   pl.BlockSpec((N_EXPERTS, tm), lambda i: (0, 0))],
        out_shape=[jax.ShapeDtypeStruct((8, n), jnp.int32),
                   jax.ShapeDtypeStruct((n, LANES), F32),
                   jax.ShapeDtypeStruct((N_EXPERTS, tm), F32)],
        scratch_shapes=[pltpu.VMEM((N_EXPERTS, tm), F32)],
        compiler_params=_cp("arbitrary"),
        name="route",
    )(logits_t, bias)


def _row_copy(src_ref, src_row, dst_ref, dst_row, sem):
    return pltpu.make_async_copy(src_ref.at[pl.ds(src_row, 1)], dst_ref.at[pl.ds(dst_row, 1)], sem)


def _scatter_kernel(dest_ref, pad_start_ref, pad_len_ref, hn_ref, xg_ref, zero_ref, sem, pad_sem):
    tm = hn_ref.shape[0]
    n = pl.num_programs(0) * tm
    base = pl.program_id(0) * tm

    @pl.when(pl.program_id(0) == 0)
    def _():
        zero_ref[...] = jnp.zeros_like(zero_ref)

        def each_pad_row(fn):
            def per_expert(e, carry):
                lax.fori_loop(0, pad_len_ref[e], lambda j, c: fn(pad_start_ref[e] + j, c), 0)
                return carry
            lax.fori_loop(0, N_EXPERTS, per_expert, 0)

        def start(row, carry):
            _row_copy(zero_ref, 0, xg_ref, row, pad_sem).start()
            return carry

        def wait(row, carry):
            _row_copy(zero_ref, 0, xg_ref, row, pad_sem).wait()
            return carry

        each_pad_row(start)
        each_pad_row(wait)

    def issue(r, carry):
        for k in range(TOP_K):
            _row_copy(hn_ref, r, xg_ref, dest_ref[k * n + base + r], sem).start()
        return carry

    lax.fori_loop(0, tm, issue, 0)
    for _ in range(TOP_K):
        pltpu.make_async_copy(hn_ref, xg_ref.at[pl.ds(0, tm)], sem).wait()


def _scatter(dest_flat, pad_start, pad_len, hn, n_rows, tm):
    n, d = hn.shape
    return pl.pallas_call(
        _scatter_kernel,
        grid_spec=pltpu.PrefetchScalarGridSpec(
            num_scalar_prefetch=3,
            grid=(n // tm,),
            in_specs=[pl.BlockSpec((tm, d), lambda i, *_: (i, 0))],
            out_specs=pl.BlockSpec(memory_space=pl.ANY),
            scratch_shapes=[pltpu.VMEM((8, d), hn.dtype), pltpu.SemaphoreType.DMA, pltpu.SemaphoreType.DMA],
        ),
        out_shape=jax.ShapeDtypeStruct((n_rows, d), hn.dtype),
        compiler_params=_cp("arbitrary"),
        name="moe_scatter",
    )(dest_flat, pad_start, pad_len, hn)


def _combine_kernel(dest_ref, x_ref, gcol_ref, yg_ref, o_ref, buf_ref, sem):
    tm = x_ref.shape[0]
    n = pl.num_programs(0) * tm
    base = pl.program_id(0) * tm

    def issue(r, carry):
        for k in range(TOP_K):
            _row_copy(yg_ref, dest_ref[k * n + base + r], buf_ref.at[k], r, sem).start()
        return carry

    lax.fori_loop(0, tm, issue, 0)
    for k in range(TOP_K):
        pltpu.make_async_copy(yg_ref.at[pl.ds(0, tm)], buf_ref.at[k], sem).wait()
    g = gcol_ref[...]
    o_ref[...] = x_ref[...] + g[:, 0:1] * buf_ref[0] + g[:, 1:2] * buf_ref[1]


def _combine(dest_flat, x1, gcol, yg, tm):
    n, d = x1.shape
    return pl.pallas_call(
        _combine_kernel,
        grid_spec=pltpu.PrefetchScalarGridSpec(
            num_scalar_prefetch=1,
            grid=(n // tm,),
            in_specs=[pl.BlockSpec((tm, d), lambda i, dest: (i, 0)),
                      pl.BlockSpec((tm, LANES), lambda i, dest: (i, 0)),
                      pl.BlockSpec(memory_space=pl.ANY)],
            out_specs=pl.BlockSpec((tm, d), lambda i, dest: (i, 0)),
            scratch_shapes=[pltpu.VMEM((TOP_K, tm, d), F32), pltpu.SemaphoreType.DMA],
        ),
        out_shape=jax.ShapeDtypeStruct((n, d), F32),
        compiler_params=_cp("arbitrary"),
        name="moe_combine",
    )(dest_flat, x1, gcol, yg)


def _moe(x1, hn, logits_t, bias_col, w1, w3, w2):
    n, d = hn.shape
    blk = MOE_BLOCK
    tm = _pick(n, 512)
    ids, gcol, cnt = _route(logits_t, bias_col, tm)
    counts = cnt[:, 0].astype(jnp.int32)
    padded = (counts + blk - 1) // blk * blk
    pends = jnp.cumsum(padded)
    pstarts = pends - padded
    experts = jnp.arange(N_EXPERTS, dtype=jnp.int32)
    seg_start = jnp.sum(jnp.where(ids[0:TOP_K, :, None] == experts, pstarts, 0), axis=-1)
    dest_flat = (seg_start + ids[TOP_K:2 * TOP_K]).reshape(-1)
    n_rows = n * TOP_K + N_EXPERTS * blk
    block_start = jnp.arange(n_rows // blk, dtype=jnp.int32) * blk
    block_expert = jnp.minimum(jnp.sum((block_start[:, None] >= pends[None, :]).astype(jnp.int32), axis=1),
                               N_EXPERTS - 1)
    n_used = (pends[N_EXPERTS - 1:] // blk).astype(jnp.int32)
    xg = _scatter(dest_flat, pstarts + counts, padded - counts, hn, n_rows, tm)
    yg = _experts(xg, block_expert, n_used, w1, w3, w2)
    return _combine(dest_flat, x1, gcol, yg, tm)


def _pick(n, pref):
    t = pref
    while n % t:
        t //= 2
    return t


def _layer(x2d, b, s, cos_tab, sin_tab, p, rwkv_precision):
    n = x2d.shape[0]
    proj = _norm_matmul(x2d, p['attn_norm'], p['w_in'], _pick(n, 1024), 1024)
    proj3 = proj.reshape(b, s, P_COLS)
    a_out = _rwkv(proj3, p['mu'], p['w0'], p['a0'], p['k_k'], p['k_a'], p['r_k'], p['ln_g'], p['ln_b'],
                  p['w_up'], p['a_up'], p['g_up'], _pick(s, 512), rwkv_precision)
    b_out = _pool(proj3, p['pool_w'], p['pool_scale'], _pick(s, 512))
    c_out = _attn(proj3, cos_tab, sin_tab, p['q_gain'], p['k_gain'], p['sinks'], _pick(s, 256))
    merged = _merge(a_out.reshape(n, -1), b_out.reshape(n, -1), c_out.reshape(n, -1), proj,
                    p['w_branch'], _pick(n, 1024), 512)
    x1, hn, logits_t = _out_proj(merged, x2d, p['w_out'], p['ffn_norm'], p['w_router_t'], _pick(n, 512))
    return _moe(x1, hn, logits_t, p['router_bias'], p['w1'], p['w3'], p['w2'])


def kernel(x, positions, attn_norm, w_in, tmix_mu, rwkv_w0, rwkv_w_up, rwkv_a0, rwkv_a_up, rwkv_g_up, rwkv_k_k, rwkv_k_a, rwkv_r_k, rwkv_ln_g, rwkv_ln_b, pool_w, pool_scale, q_norm, k_norm, attn_sinks, w_branch, w_out, ffn_norm, router_grp_w, router_grp_b, router_exp_w, router_exp_b, expert_w1, expert_w3, expert_w2):
    b, s, d = x.shape
    n = b * s
    half = HEAD_DIM // 2
    inv_freq = ROPE_THETA ** (-jnp.arange(half, dtype=F32) / half)
    inv_freq = jnp.tile(inv_freq, PAIR // half).reshape(1, PAIR)
    pos_lanes = jnp.broadcast_to(positions.astype(F32).reshape(n, 1), (n, PAIR))
    cos_tab, sin_tab = _rope_tables(pos_lanes, inv_freq, _pick(n, 2048))
    cos_tab = cos_tab.reshape(b, s, PAIR)
    sin_tab = sin_tab.reshape(b, s, PAIR)
    x2d = x.reshape(n, d)
    for l in range(w_in.shape[0]):
        p = {
            'attn_norm': attn_norm[l],
            'w_in': _pack_w_in(w_in[l]),
            'mu': _pack_mu(tmix_mu[l]),
            'w0': rwkv_w0[l], 'a0': rwkv_a0[l], 'k_k': rwkv_k_k[l], 'k_a': rwkv_k_a[l],
            'r_k': rwkv_r_k[l].reshape(-1), 'ln_g': rwkv_ln_g[l], 'ln_b': rwkv_ln_b[l],
            'w_up': _pad_rows(rwkv_w_up[l], LORA_PAD), 'a_up': _pad_rows(rwkv_a_up[l], LORA_PAD),
            'g_up': rwkv_g_up[l],
            'pool_w': pool_w[l].astype(BF16), 'pool_scale': pool_scale[l],
            'q_gain': jnp.tile(q_norm[l], PAIR // HEAD_DIM).reshape(1, PAIR),
            'k_gain': jnp.tile(k_norm[l], PAIR // HEAD_DIM).reshape(1, PAIR),
            'sinks': attn_sinks[l],
            'w_branch': w_branch[l].astype(BF16), 'w_out': w_out[l].astype(BF16),
            'ffn_norm': ffn_norm[l],
            'w_router_t': _router_weights(router_grp_w[l], router_exp_w[l]),
            'router_bias': jnp.concatenate([router_grp_b[l], jnp.zeros((8 - N_GROUPS,), F32),
                                            router_exp_b[l]]).reshape(ROUTER_ROWS, 1),
            'w1': expert_w1[l], 'w3': expert_w3[l], 'w2': expert_w2[l],
        }
        x2d = _layer(x2d, b, s, cos_tab, sin_tab, p, None)
    return x2d.reshape(b, s, d)
```

```python
import functools

import numpy as np
import jax
import jax.numpy as jnp
from jax import lax
from jax.experimental import pallas as pl
from jax.experimental.pallas import tpu as pltpu

F32 = jnp.float32
BF16 = jnp.bfloat16
HI = lax.Precision.HIGHEST

D_MODEL = 2048
HEAD_DIM = 64
BRANCH_DIM = D_MODEL // 2
DECAY_LORA = 96
ICLR_LORA = 96
GATE_LORA = 256
LORA_PAD = 128
POOL_WINDOWS = (2, 4, 8, 16)
POOL_GROUP_DIM = BRANCH_DIM // len(POOL_WINDOWS)
POOL_HALO = 16
ATTN_Q_HEADS = BRANCH_DIM // HEAD_DIM
ATTN_KV_HEADS = 4
ATTN_KV_DIM = ATTN_KV_HEADS * HEAD_DIM
WINDOW = 128
ROPE_THETA = 10000.0
N_BRANCHES = 3
N_GROUPS = 4
EXPERTS_PER_GROUP = 8
N_EXPERTS = N_GROUPS * EXPERTS_PER_GROUP
TOP_K = 2
EXPERT_FF = 512
NORM_EPS = 1e-6
GN_EPS = 64e-5

LANES = 128
PAIR = 2 * HEAD_DIM
CHUNK = 64
RWKV_PAIRS_PER_STEP = 4
ROUTER_ROWS = 40
MOE_BLOCK = 256

Z_COLS = 3 * BRANCH_DIM + 2 * LORA_PAD + GATE_LORA
COL_U = Z_COLS
COL_KV = COL_U + BRANCH_DIM
COL_Q = COL_KV + 2 * ATTN_KV_DIM
COL_G = COL_Q + BRANCH_DIM
P_COLS = COL_G + N_BRANCHES * D_MODEL

VMEM_LIMIT = 56 * 1024 * 1024


def _cp(*sem):
    return pltpu.CompilerParams(dimension_semantics=sem, vmem_limit_bytes=VMEM_LIMIT)


def _dot(a, b, precision=None):
    return jnp.dot(a, b, preferred_element_type=F32, precision=precision)


def _dot_nt(a, b, precision=None):
    return lax.dot_general(a, b, (((1,), (1,)), ((), ())), preferred_element_type=F32, precision=precision)


def _sigmoid(x):
    return 1.0 / (1.0 + jnp.exp(-x))


def _split_bf16(x):
    hi = x.astype(BF16)
    return hi, (x - hi.astype(F32)).astype(BF16)


def _dot_split_lhs(x, m):
    m = m.astype(BF16)
    return _dot(jnp.concatenate(_split_bf16(x), axis=1), jnp.concatenate([m, m], axis=0))


def _dot_split3(x, w):
    xh, xl = _split_bf16(x)
    wh, wl = _split_bf16(w)
    return _dot(jnp.concatenate([xh, xl, xh], axis=1), jnp.concatenate([wh, wh, wl], axis=0))


def _head_block_diag(scale):
    r = lax.broadcasted_iota(jnp.int32, (PAIR, PAIR), 0) // HEAD_DIM
    c = lax.broadcasted_iota(jnp.int32, (PAIR, PAIR), 1) // HEAD_DIM
    return jnp.where(r == c, scale, 0.0).astype(F32)


def _norm_matmul_kernel(x_ref, g_ref, w_ref, o_ref, h_ref):
    @pl.when(pl.program_id(1) == 0)
    def _():
        x = x_ref[...]
        ms = jnp.mean(x * x, axis=-1, keepdims=True)
        h_ref[...] = (x * lax.rsqrt(ms + NORM_EPS) * g_ref[...]).astype(BF16)

    o_ref[...] = _dot(h_ref[...], w_ref[...])


def _norm_matmul(x2d, gain, w_bf16, tm, tn):
    n, d = x2d.shape
    cols = w_bf16.shape[1]
    return pl.pallas_call(
        _norm_matmul_kernel,
        grid=(n // tm, cols // tn),
        in_specs=[
            pl.BlockSpec((tm, d), lambda i, j: (i, 0)),
            pl.BlockSpec((1, d), lambda i, j: (0, 0)),
            pl.BlockSpec((d, tn), lambda i, j: (0, j)),
        ],
        out_specs=pl.BlockSpec((tm, tn), lambda i, j: (i, j)),
        out_shape=jax.ShapeDtypeStruct((n, cols), F32),
        scratch_shapes=[pltpu.VMEM((tm, d), BF16)],
        compiler_params=_cp("arbitrary", "arbitrary"),
        name="norm_matmul",
    )(x2d, gain.reshape(1, d), w_bf16)


def _stack_heads(x, lane_is_a):
    return jnp.concatenate([jnp.where(lane_is_a, x, 0.0), jnp.where(lane_is_a, 0.0, x)], axis=0)


def _rwkv_kernel(zr_ref, zk_ref, zv_ref, zl_ref, mur_ref, muk_ref, muv_ref, mul_ref,
                 w0_ref, a0_ref, kkg_ref, ka_ref, rk_ref, lng_ref, lnb_ref,
                 wup_ref, aup_ref, gup_ref, o_ref,
                 h_ref, cr_ref, ck_ref, cv_ref, cl_ref, *, precision):
    t_rows = zr_ref.shape[1]

    @pl.when(pl.program_id(2) == 0)
    def _():
        h_ref[...] = jnp.zeros_like(h_ref)
        cr_ref[...] = jnp.zeros_like(cr_ref)
        ck_ref[...] = jnp.zeros_like(ck_ref)
        cv_ref[...] = jnp.zeros_like(cv_ref)
        cl_ref[...] = jnp.zeros_like(cl_ref)

    row = lax.broadcasted_iota(jnp.int32, (t_rows, 1), 0)

    def shifted(z_ref, carry_ref, mu_ref):
        z = z_ref[0]
        prev = jnp.where(row == 0, carry_ref[...], pltpu.roll(z, 1, 0))
        carry_ref[...] = z_ref[0, t_rows - 1:t_rows, :]
        return z + (prev - z) * mu_ref[...]

    r = shifted(zr_ref, cr_ref, mur_ref)
    k = shifted(zk_ref, ck_ref, muk_ref)
    v = shifted(zv_ref, cv_ref, muv_ref)
    zl = shifted(zl_ref, cl_ref, mul_ref)
    wd = zl[:, 0:LORA_PAD]
    ad = zl[:, LORA_PAD:2 * LORA_PAD]
    gd = zl[:, 2 * LORA_PAD:]

    head_sum = _head_block_diag(1.0)
    head_avg = _head_block_diag(1.0 / HEAD_DIM)
    n_sub = zr_ref.shape[2] // PAIR
    lanes = [slice(j * PAIR, (j + 1) * PAIR) for j in range(n_sub)]

    def per_head(x, m):
        return jnp.concatenate([_dot_split_lhs(x[:, ln], m) for ln in lanes], axis=1)

    w_pre = w0_ref[...] + _dot_split3(jnp.tanh(wd), wup_ref[...])
    neg = -w_pre
    softplus = jnp.maximum(neg, 0.0) + jnp.log(1.0 + jnp.exp(-jnp.abs(neg)))
    log_decay = -jnp.exp(-softplus - 0.5)
    a = _sigmoid(a0_ref[...] + _dot_split3(ad, aup_ref[...]))
    gate = _dot_split3(_sigmoid(gd), gup_ref[...])
    kk = k * kkg_ref[...]
    kk = kk * lax.rsqrt(jnp.maximum(per_head(kk * kk, head_sum), 1e-24))
    k = k * (1.0 + (a - 1.0) * ka_ref[...])
    bonus = per_head(r * k * rk_ref[...], head_sum) * v

    c = CHUNK
    ri = lax.broadcasted_iota(jnp.int32, (c, c), 0)
    ci = lax.broadcasted_iota(jnp.int32, (c, c), 1)
    tri_incl = (ci <= ri).astype(F32)
    tri2 = jnp.concatenate([tri_incl, tri_incl], axis=1).astype(BF16)
    lane_is_a = lax.broadcasted_iota(jnp.int32, (1, PAIR), 1) < HEAD_DIM
    rs = lax.broadcasted_iota(jnp.int32, (PAIR, PAIR), 0)
    cs = lax.broadcasted_iota(jnp.int32, (PAIR, PAIR), 1)
    strict = (cs % c) < (rs % c)
    incl = (cs % c) <= (rs % c)
    eye = rs == cs
    eye_f = eye.astype(F32)
    if precision is None:
        dot = lambda x, y: _dot(x.astype(BF16), y.astype(BF16))
        dot_nt = lambda x, y: _dot_nt(x.astype(BF16), y.astype(BF16))
    else:
        dot = functools.partial(_dot, precision=precision)
        dot_nt = functools.partial(_dot_nt, precision=precision)

    n_ch = t_rows // c
    sls = [slice(ch * c, (ch + 1) * c) for ch in range(n_ch)]
    cums = [_dot(tri2, jnp.concatenate(_split_bf16(log_decay[sl]), axis=0)) for sl in sls]
    pre = []
    for sl, cum_all in zip(sls, cums):
        for ln in lanes:
            cum = cum_all[:, ln]
            lw = log_decay[sl, ln]
            tot = cum[c - 1:c, :]
            p_in = jnp.exp(cum)
            p_ex = jnp.exp(cum - lw)
            q_inv = jnp.exp(-cum)
            q_end = jnp.exp(tot - cum)
            kk_c, a_c, k_c = kk[sl, ln], a[sl, ln], k[sl, ln]
            beta = kk_c * a_c
            pre.append(dict(
                tot=tot,
                at_s=_stack_heads(-kk_c * p_ex, lane_is_a),
                rt_s=_stack_heads(r[sl, ln] * p_in, lane_is_a),
                bh_s=_stack_heads(beta * q_inv, lane_is_a),
                kh_s=_stack_heads(k_c * q_inv, lane_is_a),
                be_s=_stack_heads(beta * q_end, lane_is_a),
                ke_s=_stack_heads(k_c * q_end, lane_is_a),
                v_s=_stack_heads(v[sl, ln], lane_is_a)))
    gms = [dot_nt(jnp.concatenate([d['at_s'], d['rt_s']], axis=0),
                  jnp.concatenate([d['bh_s'], d['kh_s']], axis=0)) for d in pre]
    a_ab = [jnp.where(strict, g[0:PAIR, 0:PAIR], 0.0) for g in gms]
    a_ak = [jnp.where(strict, g[0:PAIR, PAIR:], 0.0) for g in gms]
    a_rb = [jnp.where(incl, g[PAIR:, 0:PAIR], 0.0) for g in gms]
    a_rk = [jnp.where(incl, g[PAIR:, PAIR:], 0.0) for g in gms]
    pw = [dot(x, x) for x in a_ab]
    xv = [dot(jnp.concatenate([ak, rk, d['ke_s'].T], axis=0), d['v_s'])
          for ak, rk, d in zip(a_ak, a_rk, pre)]
    inv = [eye_f + x for x in a_ab]
    for _ in range(4):
        both = [dot(jnp.concatenate([p, x], axis=0), p) for p, x in zip(pw, inv)]
        pw = [t[0:PAIR] for t in both]
        inv = [x + t[PAIR:] for x, t in zip(inv, both)]
    inv = [x + dot(x, p) for x, p in zip(inv, pw)]
    wu = [dot(x, jnp.concatenate([d['at_s'], t[0:PAIR]], axis=1)) for x, d, t in zip(inv, pre, xv)]
    xw = [dot(jnp.concatenate([rb, d['be_s'].T], axis=0), y) for rb, d, y in zip(a_rb, pre, wu)]
    chunks = []
    for i in range(len(pre)):
        rt2 = pre[i]['rt_s'] + xw[i][0:PAIR, 0:PAIR]
        y0 = xw[i][0:PAIR, PAIR:] + xv[i][PAIR:2 * PAIR]
        m = jnp.where(eye, jnp.exp(pre[i]['tot']), 0.0) + xw[i][PAIR:, 0:PAIR]
        h0 = xw[i][PAIR:, PAIR:] + xv[i][2 * PAIR:]
        chunks.append((jnp.concatenate([rt2, m], axis=0), y0, h0))

    hs = [h_ref[j] for j in range(n_sub)]
    ys = [[] for _ in range(n_sub)]
    for i, (rm, y0, h0) in enumerate(chunks):
        j = i % n_sub
        t = dot(rm, hs[j])
        y_s = t[0:PAIR] + y0
        hs[j] = t[PAIR:] + h0
        ys[j].append(y_s[0:c] + y_s[c:])
    for j in range(n_sub):
        h_ref[j] = hs[j]

    y = jnp.concatenate([jnp.concatenate(yj, axis=0) for yj in ys], axis=1)
    mean = per_head(y, head_avg)
    yc = y - mean
    var = per_head(yc * yc, head_avg)
    gn = yc * lax.rsqrt(var + GN_EPS)
    o_ref[0] = ((gn * lng_ref[...] + lnb_ref[...] + bonus) * gate).astype(o_ref.dtype)


def _rwkv(proj3, mu_pack, w0, a0, k_k, k_a, r_k, ln_g, ln_b, w_up, a_up, g_up, t_rows, precision):
    b, s, _ = proj3.shape
    wide = RWKV_PAIRS_PER_STEP * PAIR
    n_blk = BRANCH_DIM // wide
    vec = lambda off: pl.BlockSpec((1, wide), lambda bi, p, c: (0, off + p))
    zcol = lambda off: pl.BlockSpec((1, t_rows, wide), lambda bi, p, c: (bi, c, off + p))
    lora_w = 2 * LORA_PAD + GATE_LORA
    lora_blk = (3 * BRANCH_DIM) // lora_w
    row = lambda x: x.reshape(1, BRANCH_DIM)
    kernel = functools.partial(_rwkv_kernel, precision=precision)
    return pl.pallas_call(
        kernel,
        grid=(b, n_blk, s // t_rows),
        in_specs=[
            zcol(0), zcol(n_blk), zcol(2 * n_blk),
            pl.BlockSpec((1, t_rows, lora_w), lambda bi, p, c: (bi, c, lora_blk)),
            vec(0), vec(n_blk), vec(2 * n_blk),
            pl.BlockSpec((1, lora_w), lambda bi, p, c: (0, lora_blk)),
            vec(0), vec(0), vec(0), vec(0), vec(0), vec(0), vec(0),
            pl.BlockSpec((LORA_PAD, wide), lambda bi, p, c: (0, p)),
            pl.BlockSpec((LORA_PAD, wide), lambda bi, p, c: (0, p)),
            pl.BlockSpec((GATE_LORA, wide), lambda bi, p, c: (0, p)),
        ],
        out_specs=pl.BlockSpec((1, t_rows, wide), lambda bi, p, c: (bi, c, p)),
        out_shape=jax.ShapeDtypeStruct((b, s, BRANCH_DIM), BF16),
        scratch_shapes=[
            pltpu.VMEM((RWKV_PAIRS_PER_STEP, PAIR, PAIR), F32),
            pltpu.VMEM((1, wide), F32), pltpu.VMEM((1, wide), F32), pltpu.VMEM((1, wide), F32),
            pltpu.VMEM((1, lora_w), F32),
        ],
        compiler_params=_cp("arbitrary", "arbitrary", "arbitrary"),
        name="rwkv",
    )(proj3, proj3, proj3, proj3, mu_pack, mu_pack, mu_pack, mu_pack,
      row(w0), row(a0), row(k_k), row(k_a), row(r_k), row(ln_g), row(ln_b), w_up, a_up, g_up)


def _pool_kernel(u0_ref, u1_ref, u2_ref, u3_ref, h0_ref, h1_ref, h2_ref, h3_ref, w_ref, sc_ref, o_ref, buf_ref):
    t_rows = u0_ref.shape[1]
    first = pl.program_id(1) == 0
    t = pl.program_id(1) * t_rows + lax.broadcasted_iota(jnp.int32, (t_rows, 1), 0)
    gd = POOL_GROUP_DIM
    for gi, (m, u_ref, halo_ref) in enumerate(zip(POOL_WINDOWS, (u0_ref, u1_ref, u2_ref, u3_ref),
                                                  (h0_ref, h1_ref, h2_ref, h3_ref))):
        u = u_ref[0]
        buf_ref[0:POOL_HALO, :] = jnp.where(first, 0.0, halo_ref[0])
        buf_ref[POOL_HALO:, :] = u
        acc = u
        for sft in range(1, m):
            acc = acc + buf_ref[pl.ds(POOL_HALO - sft, t_rows), :]
        count = jnp.minimum(t + 1, m).astype(F32)
        pooled = acc / count - u
        y = _dot(pooled.astype(BF16), w_ref[gi])
        o_ref[0, :, gi * gd:(gi + 1) * gd] = (y * sc_ref[:, gi * gd:(gi + 1) * gd]).astype(o_ref.dtype)


def _pool(proj3, pool_w_bf16, pool_scale, t_rows):
    b, s, _ = proj3.shape
    gd = POOL_GROUP_DIM
    ublk = COL_U // gd
    hb = t_rows // POOL_HALO
    u_spec = lambda gi: pl.BlockSpec((1, t_rows, gd), lambda bi, c: (bi, c, ublk + gi))
    h_spec = lambda gi: pl.BlockSpec((1, POOL_HALO, gd), lambda bi, c: (bi, jnp.maximum(c * hb - 1, 0), ublk + gi))
    return pl.pallas_call(
        _pool_kernel,
        grid=(b, s // t_rows),
        in_specs=[u_spec(0), u_spec(1), u_spec(2), u_spec(3), h_spec(0), h_spec(1), h_spec(2), h_spec(3),
                  pl.BlockSpec((len(POOL_WINDOWS), gd, gd), lambda bi, c: (0, 0, 0)),
                  pl.BlockSpec((1, BRANCH_DIM), lambda bi, c: (0, 0))],
        out_specs=pl.BlockSpec((1, t_rows, BRANCH_DIM), lambda bi, c: (bi, c, 0)),
        out_shape=jax.ShapeDtypeStruct((b, s, BRANCH_DIM), BF16),
        scratch_shapes=[pltpu.VMEM((t_rows + POOL_HALO, gd), F32)],
        compiler_params=_cp("parallel", "arbitrary"),
        name="pool",
    )(proj3, proj3, proj3, proj3, proj3, proj3, proj3, proj3, pool_w_bf16, pool_scale.reshape(1, BRANCH_DIM))


def _rope_table_kernel(pos_ref, invf_ref, cos_ref, sin_ref):
    ang = pos_ref[...] * invf_ref[...]
    lane = lax.broadcasted_iota(jnp.int32, (1, PAIR), 1)
    rope_lo = (lane % HEAD_DIM) < (HEAD_DIM // 2)
    cos_ref[...] = jnp.cos(ang)
    sin_ref[...] = jnp.where(rope_lo, -jnp.sin(ang), jnp.sin(ang))


def _rope_tables(pos_lanes, inv_freq, tm):
    n = pos_lanes.shape[0]
    blk = pl.BlockSpec((tm, PAIR), lambda i: (i, 0))
    return pl.pallas_call(
        _rope_table_kernel,
        grid=(n // tm,),
        in_specs=[blk, pl.BlockSpec((1, PAIR), lambda i: (0, 0))],
        out_specs=[blk, blk],
        out_shape=[jax.ShapeDtypeStruct((n, PAIR), F32), jax.ShapeDtypeStruct((n, PAIR), F32)],
        compiler_params=_cp("parallel"),
        name="rope_tables",
    )(pos_lanes, inv_freq)


def _attn_kernel(sink_ref, q_ref, kv_ref, kvh_ref, cos_ref, sin_ref, cosh_ref, sinh_ref, qg_ref, kg_ref, o_ref):
    w = WINDOW
    tq = q_ref.shape[1]
    not_first = pl.program_id(1) > 0
    head_avg = _head_block_diag(1.0 / HEAD_DIM)
    lane = lax.broadcasted_iota(jnp.int32, (1, PAIR), 1)
    lane_is_a = lane < HEAD_DIM
    rope_lo = (lane % HEAD_DIM) < (HEAD_DIM // 2)

    def norm_rope(x, cos, sin, gain):
        ms = _dot_split_lhs(x * x, head_avg)
        x = x * lax.rsqrt(ms + NORM_EPS) * gain
        rot = jnp.where(rope_lo, pltpu.roll(x, PAIR - HEAD_DIM // 2, 1), pltpu.roll(x, HEAD_DIM // 2, 1))
        return x * cos + rot * sin

    cos, sin = cos_ref[0], sin_ref[0]
    cos_k = jnp.concatenate([cosh_ref[0], cos], axis=0)
    sin_k = jnp.concatenate([sinh_ref[0], sin], axis=0)
    kv = jnp.concatenate([kvh_ref[0], kv_ref[0]], axis=0)

    qi = lax.broadcasted_iota(jnp.int32, (2 * w, 2 * w), 0) % w
    kj = lax.broadcasted_iota(jnp.int32, (2 * w, 2 * w), 1)
    rel = kj - qi
    band = (rel >= 1) & (rel <= w)
    band_first = band & ((kj >= w) | not_first)
    row_is_a = lax.broadcasted_iota(jnp.int32, (2 * w, 1), 0) < w

    for kb in range(ATTN_KV_DIM // PAIR):
        kn = norm_rope(kv[:, kb * PAIR:(kb + 1) * PAIR], cos_k, sin_k, kg_ref[...])
        vv = kv[:, ATTN_KV_DIM + kb * PAIR:ATTN_KV_DIM + (kb + 1) * PAIR]
        kn_sw = pltpu.roll(kn, HEAD_DIM, 1)
        vv_sw = pltpu.roll(vv, HEAD_DIM, 1)
        for half in range(2):
            g = 2 * kb + half
            if half == 0:
                k2 = jnp.where(lane_is_a, kn, kn_sw)
                v2 = jnp.where(lane_is_a, vv, vv_sw)
            else:
                k2 = jnp.where(lane_is_a, kn_sw, kn)
                v2 = jnp.where(lane_is_a, vv_sw, vv)
            k2 = k2.astype(BF16)
            v2 = v2.astype(BF16)
            for jp in range(2):
                qb = 2 * g + jp
                qn = norm_rope(q_ref[0, :, qb * PAIR:(qb + 1) * PAIR], cos, sin, qg_ref[...])
                qn = qn * (HEAD_DIM ** -0.5)
                sink = jnp.where(row_is_a, sink_ref[2 * qb], sink_ref[2 * qb + 1])
                for sb in range(tq // w):
                    qs = _stack_heads(qn[sb * w:(sb + 1) * w], lane_is_a).astype(BF16)
                    sc = _dot_nt(qs, k2[sb * w:(sb + 2) * w])
                    sc = jnp.where(band_first if sb == 0 else band, sc, -jnp.inf)
                    mx = jnp.maximum(jnp.max(sc, axis=-1, keepdims=True), sink)
                    e = jnp.exp(sc - mx)
                    inv_den = 1.0 / (jnp.sum(e, axis=-1, keepdims=True) + jnp.exp(sink - mx))
                    o2 = _dot((e * inv_den).astype(BF16), v2[sb * w:(sb + 2) * w])
                    o_ref[0, sb * w:(sb + 1) * w, qb * PAIR:(qb + 1) * PAIR] = (
                        jnp.where(lane_is_a, o2[0:w], o2[w:]).astype(o_ref.dtype))


def _attn(proj3, cos_tab, sin_tab, q_gain, k_gain, sinks, tq):
    b, s, _ = proj3.shape
    w = WINDOW
    kvw = 2 * ATTN_KV_DIM
    hb = tq // w
    cur = lambda width, col: pl.BlockSpec((1, tq, width), lambda bi, c: (bi, c, col))
    halo = lambda width, col: pl.BlockSpec((1, w, width), lambda bi, c: (bi, jnp.maximum(c * hb - 1, 0), col))
    vec = pl.BlockSpec((1, PAIR), lambda bi, c: (0, 0))
    return pl.pallas_call(
        _attn_kernel,
        grid=(b, s // tq),
        in_specs=[
            pl.BlockSpec(memory_space=pltpu.SMEM),
            cur(BRANCH_DIM, COL_Q // BRANCH_DIM), cur(kvw, COL_KV // kvw), halo(kvw, COL_KV // kvw),
            cur(PAIR, 0), cur(PAIR, 0), halo(PAIR, 0), halo(PAIR, 0), vec, vec,
        ],
        out_specs=pl.BlockSpec((1, tq, BRANCH_DIM), lambda bi, c: (bi, c, 0)),
        out_shape=jax.ShapeDtypeStruct((b, s, BRANCH_DIM), BF16),
        compiler_params=_cp("parallel", "arbitrary"),
        name="attn",
    )(sinks, proj3, proj3, proj3, cos_tab, sin_tab, cos_tab, sin_tab, q_gain, k_gain)


def _merge_kernel(a_ref, b_ref, c_ref, g0_ref, g1_ref, g2_ref, w_ref, o_ref):
    acc = _sigmoid(g0_ref[...]) * _dot(a_ref[...], w_ref[0])
    acc = acc + _sigmoid(g1_ref[...]) * _dot(b_ref[...], w_ref[1])
    acc = acc + _sigmoid(g2_ref[...]) * _dot(c_ref[...], w_ref[2])
    o_ref[...] = acc.astype(o_ref.dtype)


def _merge(a_out, b_out, c_out, proj, w_branch_bf16, tm, tn):
    n = proj.shape[0]
    br = lambda: pl.BlockSpec((tm, BRANCH_DIM), lambda i, j: (i, 0))
    gate = lambda g: pl.BlockSpec((tm, tn), lambda i, j: (i, (COL_G + g * D_MODEL) // tn + j))
    return pl.pallas_call(
        _merge_kernel,
        grid=(n // tm, D_MODEL // tn),
        in_specs=[br(), br(), br(), gate(0), gate(1), gate(2),
                  pl.BlockSpec((N_BRANCHES, BRANCH_DIM, tn), lambda i, j: (0, 0, j))],
        out_specs=pl.BlockSpec((tm, tn), lambda i, j: (i, j)),
        out_shape=jax.ShapeDtypeStruct((n, D_MODEL), BF16),
        compiler_params=_cp("parallel", "arbitrary"),
        name="merge",
    )(a_out, b_out, c_out, proj, proj, proj, w_branch_bf16)


def _out_proj_kernel(m_ref, x_ref, w_ref, g_ref, wr_ref, x1_ref, hn_ref, lg_ref):
    x1 = x_ref[...] + _dot(m_ref[...], w_ref[...])
    x1_ref[...] = x1
    ms = jnp.mean(x1 * x1, axis=-1, keepdims=True)
    hn = x1 * lax.rsqrt(ms + NORM_EPS) * g_ref[...]
    hn_ref[...] = hn
    hh, hl = _split_bf16(hn)
    wh, wl = _split_bf16(wr_ref[...])
    lg_ref[...] = _dot_nt(wh, hh) + _dot_nt(wh, hl) + _dot_nt(wl, hh)


def _out_proj(merged, x2d, w_out_bf16, ffn_gain, w_router_t, tm):
    n, d = x2d.shape
    return pl.pallas_call(
        _out_proj_kernel,
        grid=(n // tm,),
        in_specs=[
            pl.BlockSpec((tm, d), lambda i: (i, 0)),
            pl.BlockSpec((tm, d), lambda i: (i, 0)),
            pl.BlockSpec((d, d), lambda i: (0, 0)),
            pl.BlockSpec((1, d), lambda i: (0, 0)),
            pl.BlockSpec((ROUTER_ROWS, d), lambda i: (0, 0)),
        ],
        out_specs=[
            pl.BlockSpec((tm, d), lambda i: (i, 0)),
            pl.BlockSpec((tm, d), lambda i: (i, 0)),
            pl.BlockSpec((ROUTER_ROWS, tm), lambda i: (0, i)),
        ],
        out_shape=[
            jax.ShapeDtypeStruct((n, d), F32),
            jax.ShapeDtypeStruct((n, d), F32),
            jax.ShapeDtypeStruct((ROUTER_ROWS, n), F32),
        ],
        compiler_params=_cp("parallel"),
        name="out_proj",
    )(merged, x2d, w_out_bf16, ffn_gain.reshape(1, d), w_router_t)


def _experts_kernel(be_ref, nu_ref, x_ref, w1_ref, w3_ref, w2_ref, o_ref, w1b_ref, w3b_ref, w2b_ref):
    i = pl.program_id(0)
    new_expert = (i == 0) | (be_ref[i] != be_ref[jnp.maximum(i - 1, 0)])

    @pl.when(new_expert)
    def _():
        w1b_ref[...] = w1_ref[0].astype(BF16)
        w3b_ref[...] = w3_ref[0].astype(BF16)
        w2b_ref[...] = w2_ref[0].astype(BF16)

    used = i < nu_ref[0]

    @pl.when(used)
    def _():
        xb = x_ref[...].astype(BF16)
        h1 = _dot(xb, w1b_ref[...])
        h3 = _dot(xb, w3b_ref[...])
        act = (h1 * _sigmoid(h1)) * h3
        o_ref[...] = _dot(act.astype(BF16), w2b_ref[...])

    @pl.when(jnp.logical_not(used))
    def _():
        o_ref[...] = jnp.zeros_like(o_ref)


def _experts(xg, block_expert, n_used, w1, w3, w2):
    n_rows, d = xg.shape
    ff = w1.shape[-1]
    blk = MOE_BLOCK
    return pl.pallas_call(
        _experts_kernel,
        grid_spec=pltpu.PrefetchScalarGridSpec(
            num_scalar_prefetch=2,
            grid=(n_rows // blk,),
            in_specs=[
                pl.BlockSpec((blk, d), lambda i, be, nu: (jnp.minimum(i, nu[0] - 1), 0)),
                pl.BlockSpec((1, d, ff), lambda i, be, nu: (be[i], 0, 0)),
                pl.BlockSpec((1, d, ff), lambda i, be, nu: (be[i], 0, 0)),
                pl.BlockSpec((1, ff, d), lambda i, be, nu: (be[i], 0, 0)),
            ],
            out_specs=pl.BlockSpec((blk, d), lambda i, be, nu: (i, 0)),
            scratch_shapes=[pltpu.VMEM((d, ff), BF16), pltpu.VMEM((d, ff), BF16), pltpu.VMEM((ff, d), BF16)],
        ),
        out_shape=jax.ShapeDtypeStruct((n_rows, d), F32),
        compiler_params=_cp("arbitrary"),
        name="experts",
    )(block_expert, n_used, xg, w1, w3, w2)


def _pack_segments():
    c = BRANCH_DIM
    o_ad, o_gd = 3 * c + DECAY_LORA, 3 * c + DECAY_LORA + ICLR_LORA
    o_u = o_gd + GATE_LORA
    o_q = o_u + c
    o_k = o_q + c
    o_g = o_k + 2 * ATTN_KV_DIM
    segs = [(0, 0, LANES),
            (3 * c, 3 * c, DECAY_LORA),
            (3 * c + LORA_PAD, o_ad, ICLR_LORA),
            (3 * c + 2 * LORA_PAD, o_gd, LANES),
            (COL_U, o_u, LANES), (COL_KV, o_k, LANES), (COL_Q, o_q, LANES), (COL_G, o_g, LANES)]
    return [(p // LANES, src, keep) for p, src, keep in segs]


def _pack_w_in_kernel(a_ref, b_ref, o_ref):
    blk = pl.program_id(0)
    lane = lax.broadcasted_iota(jnp.int32, (1, LANES), 1)
    segs = _pack_segments()
    for si, (first, src, keep) in enumerate(segs):
        last = segs[si + 1][0] if si + 1 < len(segs) else P_COLS // LANES
        shift = (-src) % LANES

        @pl.when((blk >= first) & (blk < last))
        def _(shift=shift, keep=keep):
            if shift == 0:
                x = a_ref[...]
            else:
                x = jnp.where(lane < shift, pltpu.roll(a_ref[...], shift, 1), pltpu.roll(b_ref[...], shift, 1))
            if keep < LANES:
                x = jnp.where(lane < keep, x, 0.0)
            o_ref[...] = x.astype(o_ref.dtype)


def _pack_w_in(w):
    d, cols = w.shape
    segs = _pack_segments()
    last_src = (cols - 1) // LANES

    def src_block(blk):
        q = blk
        for first, src, _ in segs:
            q = jnp.where(blk >= first, blk - first + src // LANES, q)
        return q

    return pl.pallas_call(
        _pack_w_in_kernel,
        grid=(P_COLS // LANES,),
        in_specs=[pl.BlockSpec((d, LANES), lambda i: (0, src_block(i))),
                  pl.BlockSpec((d, LANES), lambda i: (0, jnp.minimum(src_block(i) + 1, last_src)))],
        out_specs=pl.BlockSpec((d, LANES), lambda i: (0, i)),
        out_shape=jax.ShapeDtypeStruct((d, P_COLS), BF16),
        compiler_params=_cp("parallel"),
        name="pack_w_in",
    )(w, w)


def _pack_mu(mu):
    z32 = jnp.zeros((LORA_PAD - DECAY_LORA,), mu.dtype)
    c = BRANCH_DIM
    o_wd, o_ad, o_gd = 3 * c, 3 * c + DECAY_LORA, 3 * c + DECAY_LORA + ICLR_LORA
    return jnp.concatenate([mu[:o_wd], mu[o_wd:o_ad], z32, mu[o_ad:o_gd], z32, mu[o_gd:]]).reshape(1, Z_COLS)


def _pad_rows(w, rows):
    return jnp.concatenate([w, jnp.zeros((rows - w.shape[0], w.shape[1]), w.dtype)], axis=0)


def _router_weights(w_grp, w_exp):
    d = w_grp.shape[0]
    pad = jnp.zeros((d, 8 - N_GROUPS), w_grp.dtype)
    return jnp.concatenate([w_grp, pad, w_exp], axis=1).T


def _route_kernel(lg_ref, bias_ref, ids_ref, gcol_ref, cnt_ref, carry_ref):
    tm = lg_ref.shape[1]

    @pl.when(pl.program_id(0) == 0)
    def _():
        carry_ref[...] = jnp.zeros_like(carry_ref)

    lg = lg_ref[...] + bias_ref[...]
    row8 = lax.broadcasted_iota(jnp.int32, (8, tm), 0)
    row8f = row8.astype(F32)
    neg_inf = -jnp.inf

    def first_argmax(x):
        mx = jnp.max(x, axis=0, keepdims=True)
        return mx, jnp.min(jnp.where(x == mx, row8f, 8.0), axis=0, keepdims=True).astype(jnp.int32)

    grp_logits = jnp.where(row8 < N_GROUPS, lg[0:8], neg_inf)
    gmax, grp = first_argmax(grp_logits)
    p_grp = 1.0 / jnp.sum(jnp.exp(grp_logits - gmax), axis=0, keepdims=True)
    in_grp = lg[8:16]
    for g in range(1, N_GROUPS):
        in_grp = jnp.where(grp == g, lg[8 + 8 * g:16 + 8 * g], in_grp)
    m1, i1 = first_argmax(in_grp)
    rest = jnp.where(row8 == i1, neg_inf, in_grp)
    m2, i2 = first_argmax(rest)
    e2 = jnp.exp(m2 - m1)
    gate1 = p_grp / (1.0 + e2)
    gate2 = p_grp * e2 / (1.0 + e2)
    exp1 = grp * EXPERTS_PER_GROUP + i1
    exp2 = grp * EXPERTS_PER_GROUP + i2

    rows = lax.broadcasted_iota(jnp.int32, (N_EXPERTS, tm), 0)
    hot1 = (rows == exp1).astype(F32)
    hot2 = (rows == exp2).astype(F32)
    cnt = (hot1 + hot2).astype(BF16)
    src = lax.broadcasted_iota(jnp.int32, (tm, tm), 0)
    dst = lax.broadcasted_iota(jnp.int32, (tm, tm), 1)
    before = _dot(cnt, (src < dst).astype(BF16)) + carry_ref[...]
    rank1 = jnp.sum(hot1 * before, axis=0, keepdims=True).astype(jnp.int32)
    rank2 = jnp.sum(hot2 * before, axis=0, keepdims=True).astype(jnp.int32)
    carry_ref[...] += _dot(cnt, jnp.ones((tm, tm), BF16))
    cnt_ref[...] = carry_ref[...]

    ids_ref[...] = jnp.where(row8 == 0, exp1, jnp.where(row8 == 1, exp2,
                             jnp.where(row8 == 2, rank1, jnp.where(row8 == 3, rank2, 0))))
    row128 = lax.broadcasted_iota(jnp.int32, (LANES, tm), 0)
    gates_t = jnp.where(row128 == 0, gate1, jnp.where(row128 == 1, gate2, 0.0))
    gcol_ref[...] = gates_t.T


def _route(logits_t, bias_col, tm):
    n = logits_t.shape[1]
    bias = jnp.broadcast_to(bias_col, (ROUTER_ROWS, tm))
    return pl.pallas_call(
        _route_kernel,
        grid=(n // tm,),
        in_specs=[pl.BlockSpec((ROUTER_ROWS, tm), lambda i: (0, i)),
                  pl.BlockSpec((ROUTER_ROWS, tm), lambda i: (0, 0))],
        out_specs=[pl.BlockSpec((8, tm), lambda i: (0, i)),
                   pl.BlockSpec((tm, LANES), lambda i: (i, 0)),
                   pl.BlockSpec((N_EXPERTS, tm), lambda i: (0, 0))],
        out_shape=[jax.ShapeDtypeStruct((8, n), jnp.int32),
                   jax.ShapeDtypeStruct((n, LANES), F32),
                   jax.ShapeDtypeStruct((N_EXPERTS, tm), F32)],
        scratch_shapes=[pltpu.VMEM((N_EXPERTS, tm), F32)],
        compiler_params=_cp("arbitrary"),
        name="route",
    )(logits_t, bias)


def _row_copy(src_ref, src_row, dst_ref, dst_row, sem):
    return pltpu.make_async_copy(src_ref.at[pl.ds(src_row, 1)], dst_ref.at[pl.ds(dst_row, 1)], sem)


def _scatter_kernel(dest_ref, pad_start_ref, pad_len_ref, hn_ref, xg_ref, zero_ref, sem, pad_sem):
    tm = hn_ref.shape[0]
    n = pl.num_programs(0) * tm
    base = pl.program_id(0) * tm

    @pl.when(pl.program_id(0) == 0)
    def _():
        zero_ref[...] = jnp.zeros_like(zero_ref)

        def each_pad_row(fn):
            def per_expert(e, carry):
                lax.fori_loop(0, pad_len_ref[e], lambda j, c: fn(pad_start_ref[e] + j, c), 0)
                return carry
            lax.fori_loop(0, pad_len_ref.shape[0], per_expert, 0)

        def start(row, carry):
            _row_copy(zero_ref, 0, xg_ref, row, pad_sem).start()
            return carry

        def wait(row, carry):
            _row_copy(zero_ref, 0, xg_ref, row, pad_sem).wait()
            return carry

        each_pad_row(start)
        each_pad_row(wait)

    def issue(r, carry):
        for k in range(TOP_K):
            _row_copy(hn_ref, r, xg_ref, dest_ref[k * n + base + r], sem).start()
        return carry

    lax.fori_loop(0, tm, issue, 0)
    for _ in range(TOP_K):
        pltpu.make_async_copy(hn_ref, xg_ref.at[pl.ds(0, tm)], sem).wait()


def _scatter(dest_flat, pad_start, pad_len, hn, n_rows, tm):
    n, d = hn.shape
    return pl.pallas_call(
        _scatter_kernel,
        grid_spec=pltpu.PrefetchScalarGridSpec(
            num_scalar_prefetch=3,
            grid=(n // tm,),
            in_specs=[pl.BlockSpec((tm, d), lambda i, *_: (i, 0))],
            out_specs=pl.BlockSpec(memory_space=pl.ANY),
            scratch_shapes=[pltpu.VMEM((8, d), hn.dtype), pltpu.SemaphoreType.DMA, pltpu.SemaphoreType.DMA],
        ),
        out_shape=jax.ShapeDtypeStruct((n_rows, d), hn.dtype),
        compiler_params=_cp("arbitrary"),
        name="moe_scatter",
    )(dest_flat, pad_start, pad_len, hn)


def _combine_kernel(dest_ref, x_ref, gcol_ref, yg_ref, o_ref, buf_ref, sem):
    tm = x_ref.shape[0]
    n = pl.num_programs(0) * tm
    base = pl.program_id(0) * tm

    def issue(r, carry):
        for k in range(TOP_K):
            _row_copy(yg_ref, dest_ref[k * n + base + r], buf_ref.at[k], r, sem).start()
        return carry

    lax.fori_loop(0, tm, issue, 0)
    for k in range(TOP_K):
        pltpu.make_async_copy(yg_ref.at[pl.ds(0, tm)], buf_ref.at[k], sem).wait()
    g = gcol_ref[...]
    o_ref[...] = x_ref[...] + g[:, 0:1] * buf_ref[0] + g[:, 1:2] * buf_ref[1]


def _combine(dest_flat, x1, gcol, yg, tm):
    n, d = x1.shape
    return pl.pallas_call(
        _combine_kernel,
        grid_spec=pltpu.PrefetchScalarGridSpec(
            num_scalar_prefetch=1,
            grid=(n // tm,),
            in_specs=[pl.BlockSpec((tm, d), lambda i, dest: (i, 0)),
                      pl.BlockSpec((tm, LANES), lambda i, dest: (i, 0)),
                      pl.BlockSpec(memory_space=pl.ANY)],
            out_specs=pl.BlockSpec((tm, d), lambda i, dest: (i, 0)),
            scratch_shapes=[pltpu.VMEM((TOP_K, tm, d), F32), pltpu.SemaphoreType.DMA],
        ),
        out_shape=jax.ShapeDtypeStruct((n, d), F32),
        compiler_params=_cp("arbitrary"),
        name="moe_combine",
    )(dest_flat, x1, gcol, yg)


def _moe(x1, hn, logits_t, bias_col, w1, w3, w2):
    n, d = hn.shape
    blk = MOE_BLOCK
    tm = _pick(n, 512)
    ids, gcol, cnt = _route(logits_t, bias_col, tm)
    counts = cnt[:, 0].astype(jnp.int32)
    padded = (counts + blk - 1) // blk * blk
    pends = jnp.cumsum(padded)
    pstarts = pends - padded
    experts = jnp.arange(N_EXPERTS, dtype=jnp.int32)
    seg_start = jnp.sum(jnp.where(ids[0:TOP_K, :, None] == experts, pstarts, 0), axis=-1)
    dest_flat = (seg_start + ids[TOP_K:2 * TOP_K]).reshape(-1)
    n_rows = n * TOP_K + N_EXPERTS * blk
    block_start = jnp.arange(n_rows // blk, dtype=jnp.int32) * blk
    block_expert = jnp.minimum(jnp.sum((block_start[:, None] >= pends[None, :]).astype(jnp.int32), axis=1),
                               N_EXPERTS - 1)
    n_used = (pends[N_EXPERTS - 1:] // blk).astype(jnp.int32)
    pad_start = jnp.concatenate([pstarts + counts, pends[N_EXPERTS - 1:]])
    pad_len = jnp.concatenate([padded - counts, n_rows - pends[N_EXPERTS - 1:]])
    xg = _scatter(dest_flat, pad_start, pad_len, hn, n_rows, tm)
    yg = _experts(xg, block_expert, n_used, w1, w3, w2)
    return _combine(dest_flat, x1, gcol, yg, tm)


def _pick(n, pref):
    t = pref
    while n % t:
        t //= 2
    return t


def _layer(x2d, b, s, cos_tab, sin_tab, p, rwkv_precision):
    n = x2d.shape[0]
    proj = _norm_matmul(x2d, p['attn_norm'], p['w_in'], _pick(n, 1024), 1024)
    proj3 = proj.reshape(b, s, P_COLS)
    a_out = _rwkv(proj3, p['mu'], p['w0'], p['a0'], p['k_k'], p['k_a'], p['r_k'], p['ln_g'], p['ln_b'],
                  p['w_up'], p['a_up'], p['g_up'], _pick(s, 512), rwkv_precision)
    b_out = _pool(proj3, p['pool_w'], p['pool_scale'], _pick(s, 512))
    c_out = _attn(proj3, cos_tab, sin_tab, p['q_gain'], p['k_gain'], p['sinks'], _pick(s, 256))
    merged = _merge(a_out.reshape(n, -1), b_out.reshape(n, -1), c_out.reshape(n, -1), proj,
                    p['w_branch'], _pick(n, 1024), 512)
    x1, hn, logits_t = _out_proj(merged, x2d, p['w_out'], p['ffn_norm'], p['w_router_t'], _pick(n, 512))
    return _moe(x1, hn, logits_t, p['router_bias'], p['w1'], p['w3'], p['w2'])


def kernel(x, positions, attn_norm, w_in, tmix_mu, rwkv_w0, rwkv_w_up, rwkv_a0, rwkv_a_up, rwkv_g_up, rwkv_k_k, rwkv_k_a, rwkv_r_k, rwkv_ln_g, rwkv_ln_b, pool_w, pool_scale, q_norm, k_norm, attn_sinks, w_branch, w_out, ffn_norm, router_grp_w, router_grp_b, router_exp_w, router_exp_b, expert_w1, expert_w3, expert_w2):
    b, s, d = x.shape
    n = b * s
    half = HEAD_DIM // 2
    inv_freq = ROPE_THETA ** (-jnp.arange(half, dtype=F32) / half)
    inv_freq = jnp.tile(inv_freq, PAIR // half).reshape(1, PAIR)
    pos_lanes = jnp.broadcast_to(positions.astype(F32).reshape(n, 1), (n, PAIR))
    cos_tab, sin_tab = _rope_tables(pos_lanes, inv_freq, _pick(n, 2048))
    cos_tab = cos_tab.reshape(b, s, PAIR)
    sin_tab = sin_tab.reshape(b, s, PAIR)
    x2d = x.reshape(n, d)
    for l in range(w_in.shape[0]):
        p = {
            'attn_norm': attn_norm[l],
            'w_in': _pack_w_in(w_in[l]),
            'mu': _pack_mu(tmix_mu[l]),
            'w0': rwkv_w0[l], 'a0': rwkv_a0[l], 'k_k': rwkv_k_k[l], 'k_a': rwkv_k_a[l],
            'r_k': rwkv_r_k[l].reshape(-1), 'ln_g': rwkv_ln_g[l], 'ln_b': rwkv_ln_b[l],
            'w_up': _pad_rows(rwkv_w_up[l], LORA_PAD), 'a_up': _pad_rows(rwkv_a_up[l], LORA_PAD),
            'g_up': rwkv_g_up[l],
            'pool_w': pool_w[l].astype(BF16), 'pool_scale': pool_scale[l],
            'q_gain': jnp.tile(q_norm[l], PAIR // HEAD_DIM).reshape(1, PAIR),
            'k_gain': jnp.tile(k_norm[l], PAIR // HEAD_DIM).reshape(1, PAIR),
            'sinks': attn_sinks[l],
            'w_branch': w_branch[l].astype(BF16), 'w_out': w_out[l].astype(BF16),
            'ffn_norm': ffn_norm[l],
            'w_router_t': _router_weights(router_grp_w[l], router_exp_w[l]),
            'router_bias': jnp.concatenate([router_grp_b[l], jnp.zeros((8 - N_GROUPS,), F32),
                                            router_exp_b[l]]).reshape(ROUTER_ROWS, 1),
            'w1': expert_w1[l], 'w3': expert_w3[l], 'w2': expert_w2[l],
        }
        x2d = _layer(x2d, b, s, cos_tab, sin_tab, p, None)
    return x2d.reshape(b, s, d)
```

```python
import functools

import numpy as np
import jax
import jax.numpy as jnp
from jax import lax
from jax.experimental import pallas as pl
from jax.experimental.pallas import tpu as pltpu

F32 = jnp.float32
BF16 = jnp.bfloat16
HI = lax.Precision.HIGHEST

D_MODEL = 2048
HEAD_DIM = 64
BRANCH_DIM = D_MODEL // 2
DECAY_LORA = 96
ICLR_LORA = 96
GATE_LORA = 256
LORA_PAD = 128
POOL_WINDOWS = (2, 4, 8, 16)
POOL_GROUP_DIM = BRANCH_DIM // len(POOL_WINDOWS)
POOL_HALO = 16
ATTN_Q_HEADS = BRANCH_DIM // HEAD_DIM
ATTN_KV_HEADS = 4
ATTN_KV_DIM = ATTN_KV_HEADS * HEAD_DIM
WINDOW = 128
ROPE_THETA = 10000.0
N_BRANCHES = 3
N_GROUPS = 4
EXPERTS_PER_GROUP = 8
N_EXPERTS = N_GROUPS * EXPERTS_PER_GROUP
TOP_K = 2
EXPERT_FF = 512
NORM_EPS = 1e-6
GN_EPS = 64e-5

LANES = 128
PAIR = 2 * HEAD_DIM
CHUNK = 64
RWKV_PAIRS_PER_STEP = 4
ROUTER_ROWS = 40
MOE_BLOCK = 256
ISSUE_UNROLL = 8

Z_COLS = 3 * BRANCH_DIM + 2 * LORA_PAD + GATE_LORA
COL_U = Z_COLS
COL_KV = COL_U + BRANCH_DIM
COL_Q = COL_KV + 2 * ATTN_KV_DIM
COL_G = COL_Q + BRANCH_DIM
P_COLS = COL_G + N_BRANCHES * D_MODEL

VMEM_LIMIT = 56 * 1024 * 1024


def _cp(*sem):
    return pltpu.CompilerParams(dimension_semantics=sem, vmem_limit_bytes=VMEM_LIMIT)


def _dot(a, b, precision=None):
    return jnp.dot(a, b, preferred_element_type=F32, precision=precision)


def _dot_nt(a, b, precision=None):
    return lax.dot_general(a, b, (((1,), (1,)), ((), ())), preferred_element_type=F32, precision=precision)


def _sigmoid(x):
    return 1.0 / (1.0 + jnp.exp(-x))


def _split_bf16(x):
    hi = x.astype(BF16)
    return hi, (x - hi.astype(F32)).astype(BF16)


def _dot_split_lhs(x, m):
    m = m.astype(BF16)
    return _dot(jnp.concatenate(_split_bf16(x), axis=1), jnp.concatenate([m, m], axis=0))


def _dot_split3(x, w):
    xh, xl = _split_bf16(x)
    wh, wl = _split_bf16(w)
    return _dot(jnp.concatenate([xh, xl, xh], axis=1), jnp.concatenate([wh, wh, wl], axis=0))


def _head_block_diag(scale):
    r = lax.broadcasted_iota(jnp.int32, (PAIR, PAIR), 0) // HEAD_DIM
    c = lax.broadcasted_iota(jnp.int32, (PAIR, PAIR), 1) // HEAD_DIM
    return jnp.where(r == c, scale, 0.0).astype(F32)


def _norm_matmul_kernel(x_ref, g_ref, w_ref, o_ref, h_ref):
    @pl.when(pl.program_id(1) == 0)
    def _():
        x = x_ref[...]
        ms = jnp.mean(x * x, axis=-1, keepdims=True)
        h_ref[...] = (x * lax.rsqrt(ms + NORM_EPS) * g_ref[...]).astype(BF16)

    o_ref[...] = _dot(h_ref[...], w_ref[...])


def _norm_matmul(x2d, gain, w_bf16, tm, tn):
    n, d = x2d.shape
    cols = w_bf16.shape[1]
    return pl.pallas_call(
        _norm_matmul_kernel,
        grid=(n // tm, cols // tn),
        in_specs=[
            pl.BlockSpec((tm, d), lambda i, j: (i, 0)),
            pl.BlockSpec((1, d), lambda i, j: (0, 0)),
            pl.BlockSpec((d, tn), lambda i, j: (0, j)),
        ],
        out_specs=pl.BlockSpec((tm, tn), lambda i, j: (i, j)),
        out_shape=jax.ShapeDtypeStruct((n, cols), F32),
        scratch_shapes=[pltpu.VMEM((tm, d), BF16)],
        compiler_params=_cp("arbitrary", "arbitrary"),
        name="norm_matmul",
    )(x2d, gain.reshape(1, d), w_bf16)


def _stack_heads(x, lane_is_a):
    return jnp.concatenate([jnp.where(lane_is_a, x, 0.0), jnp.where(lane_is_a, 0.0, x)], axis=0)


def _rwkv_kernel(zr_ref, zk_ref, zv_ref, zl_ref, mur_ref, muk_ref, muv_ref, mul_ref,
                 w0_ref, a0_ref, kkg_ref, ka_ref, rk_ref, lng_ref, lnb_ref,
                 wup_ref, aup_ref, gup_ref, o_ref,
                 h_ref, cr_ref, ck_ref, cv_ref, cl_ref, *, precision):
    t_rows = zr_ref.shape[1]

    @pl.when(pl.program_id(2) == 0)
    def _():
        h_ref[...] = jnp.zeros_like(h_ref)
        cr_ref[...] = jnp.zeros_like(cr_ref)
        ck_ref[...] = jnp.zeros_like(ck_ref)
        cv_ref[...] = jnp.zeros_like(cv_ref)
        cl_ref[...] = jnp.zeros_like(cl_ref)

    row = lax.broadcasted_iota(jnp.int32, (t_rows, 1), 0)

    def shifted(z_ref, carry_ref, mu_ref):
        z = z_ref[0]
        prev = jnp.where(row == 0, carry_ref[...], pltpu.roll(z, 1, 0))
        carry_ref[...] = z_ref[0, t_rows - 1:t_rows, :]
        return z + (prev - z) * mu_ref[...]

    r = shifted(zr_ref, cr_ref, mur_ref)
    k = shifted(zk_ref, ck_ref, muk_ref)
    v = shifted(zv_ref, cv_ref, muv_ref)
    zl = shifted(zl_ref, cl_ref, mul_ref)
    wd = zl[:, 0:LORA_PAD]
    ad = zl[:, LORA_PAD:2 * LORA_PAD]
    gd = zl[:, 2 * LORA_PAD:]

    head_sum = _head_block_diag(1.0)
    head_avg = _head_block_diag(1.0 / HEAD_DIM)
    n_sub = zr_ref.shape[2] // PAIR
    lanes = [slice(j * PAIR, (j + 1) * PAIR) for j in range(n_sub)]

    def per_head(x, m):
        return jnp.concatenate([_dot_split_lhs(x[:, ln], m) for ln in lanes], axis=1)

    w_pre = w0_ref[...] + _dot_split3(jnp.tanh(wd), wup_ref[...])
    neg = -w_pre
    softplus = jnp.maximum(neg, 0.0) + jnp.log(1.0 + jnp.exp(-jnp.abs(neg)))
    log_decay = -jnp.exp(-softplus - 0.5)
    a = _sigmoid(a0_ref[...] + _dot_split3(ad, aup_ref[...]))
    gate = _dot_split3(_sigmoid(gd), gup_ref[...])
    kk = k * kkg_ref[...]
    kk = kk * lax.rsqrt(jnp.maximum(per_head(kk * kk, head_sum), 1e-24))
    k = k * (1.0 + (a - 1.0) * ka_ref[...])
    bonus = per_head(r * k * rk_ref[...], head_sum) * v

    c = CHUNK
    ri = lax.broadcasted_iota(jnp.int32, (c, c), 0)
    ci = lax.broadcasted_iota(jnp.int32, (c, c), 1)
    tri_incl = (ci <= ri).astype(F32)
    tri2 = jnp.concatenate([tri_incl, tri_incl], axis=1).astype(BF16)
    lane_is_a = lax.broadcasted_iota(jnp.int32, (1, PAIR), 1) < HEAD_DIM
    rs = lax.broadcasted_iota(jnp.int32, (PAIR, PAIR), 0)
    cs = lax.broadcasted_iota(jnp.int32, (PAIR, PAIR), 1)
    strict = (cs % c) < (rs % c)
    incl = (cs % c) <= (rs % c)
    eye = rs == cs
    eye_f = eye.astype(F32)
    if precision is None:
        dot = lambda x, y: _dot(x.astype(BF16), y.astype(BF16))
        dot_nt = lambda x, y: _dot_nt(x.astype(BF16), y.astype(BF16))
    else:
        dot = functools.partial(_dot, precision=precision)
        dot_nt = functools.partial(_dot_nt, precision=precision)

    n_ch = t_rows // c
    sls = [slice(ch * c, (ch + 1) * c) for ch in range(n_ch)]
    cums = [_dot(tri2, jnp.concatenate(_split_bf16(log_decay[sl]), axis=0)) for sl in sls]
    pre = []
    for sl, cum_all in zip(sls, cums):
        for ln in lanes:
            cum = cum_all[:, ln]
            lw = log_decay[sl, ln]
            tot = cum[c - 1:c, :]
            p_in = jnp.exp(cum)
            p_ex = jnp.exp(cum - lw)
            q_inv = jnp.exp(-cum)
            q_end = jnp.exp(tot - cum)
            kk_c, a_c, k_c = kk[sl, ln], a[sl, ln], k[sl, ln]
            beta = kk_c * a_c
            pre.append(dict(
                tot=tot,
                at_s=_stack_heads(-kk_c * p_ex, lane_is_a),
                rt_s=_stack_heads(r[sl, ln] * p_in, lane_is_a),
                bh_s=_stack_heads(beta * q_inv, lane_is_a),
                kh_s=_stack_heads(k_c * q_inv, lane_is_a),
                be_s=_stack_heads(beta * q_end, lane_is_a),
                ke_s=_stack_heads(k_c * q_end, lane_is_a),
                v_s=_stack_heads(v[sl, ln], lane_is_a)))
    gms = [dot_nt(jnp.concatenate([d['at_s'], d['rt_s']], axis=0),
                  jnp.concatenate([d['bh_s'], d['kh_s']], axis=0)) for d in pre]
    a_ab = [jnp.where(strict, g[0:PAIR, 0:PAIR], 0.0) for g in gms]
    a_ak = [jnp.where(strict, g[0:PAIR, PAIR:], 0.0) for g in gms]
    a_rb = [jnp.where(incl, g[PAIR:, 0:PAIR], 0.0) for g in gms]
    a_rk = [jnp.where(incl, g[PAIR:, PAIR:], 0.0) for g in gms]
    pw = [dot(x, x) for x in a_ab]
    xv = [dot(jnp.concatenate([ak, rk, d['ke_s'].T], axis=0), d['v_s'])
          for ak, rk, d in zip(a_ak, a_rk, pre)]
    inv = [eye_f + x for x in a_ab]
    for _ in range(4):
        both = [dot(jnp.concatenate([p, x], axis=0), p) for p, x in zip(pw, inv)]
        pw = [t[0:PAIR] for t in both]
        inv = [x + t[PAIR:] for x, t in zip(inv, both)]
    inv = [x + dot(x, p) for x, p in zip(inv, pw)]
    wu = [dot(x, jnp.concatenate([d['at_s'], t[0:PAIR]], axis=1)) for x, d, t in zip(inv, pre, xv)]
    xw = [dot(jnp.concatenate([rb, d['be_s'].T], axis=0), y) for rb, d, y in zip(a_rb, pre, wu)]
    chunks = []
    for i in range(len(pre)):
        rt2 = pre[i]['rt_s'] + xw[i][0:PAIR, 0:PAIR]
        y0 = xw[i][0:PAIR, PAIR:] + xv[i][PAIR:2 * PAIR]
        m = jnp.where(eye, jnp.exp(pre[i]['tot']), 0.0) + xw[i][PAIR:, 0:PAIR]
        h0 = xw[i][PAIR:, PAIR:] + xv[i][2 * PAIR:]
        chunks.append((jnp.concatenate([rt2, m], axis=0), y0, h0))

    hs = [h_ref[j] for j in range(n_sub)]
    ys = [[] for _ in range(n_sub)]
    for i, (rm, y0, h0) in enumerate(chunks):
        j = i % n_sub
        t = dot(rm, hs[j])
        y_s = t[0:PAIR] + y0
        hs[j] = t[PAIR:] + h0
        ys[j].append(y_s[0:c] + y_s[c:])
    for j in range(n_sub):
        h_ref[j] = hs[j]

    y = jnp.concatenate([jnp.concatenate(yj, axis=0) for yj in ys], axis=1)
    mean = per_head(y, head_avg)
    yc = y - mean
    var = per_head(yc * yc, head_avg)
    gn = yc * lax.rsqrt(var + GN_EPS)
    o_ref[0] = ((gn * lng_ref[...] + lnb_ref[...] + bonus) * gate).astype(o_ref.dtype)


def _rwkv(proj3, mu_pack, w0, a0, k_k, k_a, r_k, ln_g, ln_b, w_up, a_up, g_up, t_rows, precision):
    b, s, _ = proj3.shape
    wide = RWKV_PAIRS_PER_STEP * PAIR
    n_blk = BRANCH_DIM // wide
    vec = lambda off: pl.BlockSpec((1, wide), lambda bi, p, c: (0, off + p))
    zcol = lambda off: pl.BlockSpec((1, t_rows, wide), lambda bi, p, c: (bi, c, off + p))
    lora_w = 2 * LORA_PAD + GATE_LORA
    lora_blk = (3 * BRANCH_DIM) // lora_w
    row = lambda x: x.reshape(1, BRANCH_DIM)
    kernel = functools.partial(_rwkv_kernel, precision=precision)
    return pl.pallas_call(
        kernel,
        grid=(b, n_blk, s // t_rows),
        in_specs=[
            zcol(0), zcol(n_blk), zcol(2 * n_blk),
            pl.BlockSpec((1, t_rows, lora_w), lambda bi, p, c: (bi, c, lora_blk)),
            vec(0), vec(n_blk), vec(2 * n_blk),
            pl.BlockSpec((1, lora_w), lambda bi, p, c: (0, lora_blk)),
            vec(0), vec(0), vec(0), vec(0), vec(0), vec(0), vec(0),
            pl.BlockSpec((LORA_PAD, wide), lambda bi, p, c: (0, p)),
            pl.BlockSpec((LORA_PAD, wide), lambda bi, p, c: (0, p)),
            pl.BlockSpec((GATE_LORA, wide), lambda bi, p, c: (0, p)),
        ],
        out_specs=pl.BlockSpec((1, t_rows, wide), lambda bi, p, c: (bi, c, p)),
        out_shape=jax.ShapeDtypeStruct((b, s, BRANCH_DIM), BF16),
        scratch_shapes=[
            pltpu.VMEM((RWKV_PAIRS_PER_STEP, PAIR, PAIR), F32),
            pltpu.VMEM((1, wide), F32), pltpu.VMEM((1, wide), F32), pltpu.VMEM((1, wide), F32),
            pltpu.VMEM((1, lora_w), F32),
        ],
        compiler_params=_cp("arbitrary", "arbitrary", "arbitrary"),
        name="rwkv",
    )(proj3, proj3, proj3, proj3, mu_pack, mu_pack, mu_pack, mu_pack,
      row(w0), row(a0), row(k_k), row(k_a), row(r_k), row(ln_g), row(ln_b), w_up, a_up, g_up)


def _pool_kernel(u0_ref, u1_ref, u2_ref, u3_ref, h0_ref, h1_ref, h2_ref, h3_ref, w_ref, sc_ref, o_ref, buf_ref):
    t_rows = u0_ref.shape[1]
    first = pl.program_id(1) == 0
    t = pl.program_id(1) * t_rows + lax.broadcasted_iota(jnp.int32, (t_rows, 1), 0)
    gd = POOL_GROUP_DIM
    for gi, (m, u_ref, halo_ref) in enumerate(zip(POOL_WINDOWS, (u0_ref, u1_ref, u2_ref, u3_ref),
                                                  (h0_ref, h1_ref, h2_ref, h3_ref))):
        u = u_ref[0]
        buf_ref[0:POOL_HALO, :] = jnp.where(first, 0.0, halo_ref[0])
        buf_ref[POOL_HALO:, :] = u
        acc = u
        for sft in range(1, m):
            acc = acc + buf_ref[pl.ds(POOL_HALO - sft, t_rows), :]
        count = jnp.minimum(t + 1, m).astype(F32)
        pooled = acc / count - u
        y = _dot(pooled.astype(BF16), w_ref[gi])
        o_ref[0, :, gi * gd:(gi + 1) * gd] = (y * sc_ref[:, gi * gd:(gi + 1) * gd]).astype(o_ref.dtype)


def _pool(proj3, pool_w_bf16, pool_scale, t_rows):
    b, s, _ = proj3.shape
    gd = POOL_GROUP_DIM
    ublk = COL_U // gd
    hb = t_rows // POOL_HALO
    u_spec = lambda gi: pl.BlockSpec((1, t_rows, gd), lambda bi, c: (bi, c, ublk + gi))
    h_spec = lambda gi: pl.BlockSpec((1, POOL_HALO, gd), lambda bi, c: (bi, jnp.maximum(c * hb - 1, 0), ublk + gi))
    return pl.pallas_call(
        _pool_kernel,
        grid=(b, s // t_rows),
        in_specs=[u_spec(0), u_spec(1), u_spec(2), u_spec(3), h_spec(0), h_spec(1), h_spec(2), h_spec(3),
                  pl.BlockSpec((len(POOL_WINDOWS), gd, gd), lambda bi, c: (0, 0, 0)),
                  pl.BlockSpec((1, BRANCH_DIM), lambda bi, c: (0, 0))],
        out_specs=pl.BlockSpec((1, t_rows, BRANCH_DIM), lambda bi, c: (bi, c, 0)),
        out_shape=jax.ShapeDtypeStruct((b, s, BRANCH_DIM), BF16),
        scratch_shapes=[pltpu.VMEM((t_rows + POOL_HALO, gd), F32)],
        compiler_params=_cp("parallel", "arbitrary"),
        name="pool",
    )(proj3, proj3, proj3, proj3, proj3, proj3, proj3, proj3, pool_w_bf16, pool_scale.reshape(1, BRANCH_DIM))


def _rope_table_kernel(pos_ref, invf_ref, cos_ref, sin_ref):
    ang = pos_ref[...] * invf_ref[...]
    lane = lax.broadcasted_iota(jnp.int32, (1, PAIR), 1)
    rope_lo = (lane % HEAD_DIM) < (HEAD_DIM // 2)
    cos_ref[...] = jnp.cos(ang)
    sin_ref[...] = jnp.where(rope_lo, -jnp.sin(ang), jnp.sin(ang))


def _rope_tables(pos_lanes, inv_freq, tm):
    n = pos_lanes.shape[0]
    blk = pl.BlockSpec((tm, PAIR), lambda i: (i, 0))
    return pl.pallas_call(
        _rope_table_kernel,
        grid=(n // tm,),
        in_specs=[blk, pl.BlockSpec((1, PAIR), lambda i: (0, 0))],
        out_specs=[blk, blk],
        out_shape=[jax.ShapeDtypeStruct((n, PAIR), F32), jax.ShapeDtypeStruct((n, PAIR), F32)],
        compiler_params=_cp("parallel"),
        name="rope_tables",
    )(pos_lanes, inv_freq)


def _attn_kernel(sink_ref, q_ref, kv_ref, kvh_ref, cos_ref, sin_ref, cosh_ref, sinh_ref, qg_ref, kg_ref, o_ref):
    w = WINDOW
    tq = q_ref.shape[1]
    not_first = pl.program_id(1) > 0
    head_avg = _head_block_diag(1.0 / HEAD_DIM)
    lane = lax.broadcasted_iota(jnp.int32, (1, PAIR), 1)
    lane_is_a = lane < HEAD_DIM
    rope_lo = (lane % HEAD_DIM) < (HEAD_DIM // 2)

    def norm_rope(x, cos, sin, gain):
        ms = _dot_split_lhs(x * x, head_avg)
        x = x * lax.rsqrt(ms + NORM_EPS) * gain
        rot = jnp.where(rope_lo, pltpu.roll(x, PAIR - HEAD_DIM // 2, 1), pltpu.roll(x, HEAD_DIM // 2, 1))
        return x * cos + rot * sin

    cos, sin = cos_ref[0], sin_ref[0]
    cos_k = jnp.concatenate([cosh_ref[0], cos], axis=0)
    sin_k = jnp.concatenate([sinh_ref[0], sin], axis=0)
    kv = jnp.concatenate([kvh_ref[0], kv_ref[0]], axis=0)

    qi = lax.broadcasted_iota(jnp.int32, (2 * w, 2 * w), 0) % w
    kj = lax.broadcasted_iota(jnp.int32, (2 * w, 2 * w), 1)
    rel = kj - qi
    band = (rel >= 1) & (rel <= w)
    band_first = band & ((kj >= w) | not_first)
    row_is_a = lax.broadcasted_iota(jnp.int32, (2 * w, 1), 0) < w

    for kb in range(ATTN_KV_DIM // PAIR):
        kn = norm_rope(kv[:, kb * PAIR:(kb + 1) * PAIR], cos_k, sin_k, kg_ref[...])
        vv = kv[:, ATTN_KV_DIM + kb * PAIR:ATTN_KV_DIM + (kb + 1) * PAIR]
        kn_sw = pltpu.roll(kn, HEAD_DIM, 1)
        vv_sw = pltpu.roll(vv, HEAD_DIM, 1)
        for half in range(2):
            g = 2 * kb + half
            if half == 0:
                k2 = jnp.where(lane_is_a, kn, kn_sw)
                v2 = jnp.where(lane_is_a, vv, vv_sw)
            else:
                k2 = jnp.where(lane_is_a, kn_sw, kn)
                v2 = jnp.where(lane_is_a, vv_sw, vv)
            k2 = k2.astype(BF16)
            v2 = v2.astype(BF16)
            for jp in range(2):
                qb = 2 * g + jp
                qn = norm_rope(q_ref[0, :, qb * PAIR:(qb + 1) * PAIR], cos, sin, qg_ref[...])
                qn = qn * (HEAD_DIM ** -0.5)
                sink = jnp.where(row_is_a, sink_ref[2 * qb], sink_ref[2 * qb + 1])
                for sb in range(tq // w):
                    qs = _stack_heads(qn[sb * w:(sb + 1) * w], lane_is_a).astype(BF16)
                    sc = _dot_nt(qs, k2[sb * w:(sb + 2) * w])
                    sc = jnp.where(band_first if sb == 0 else band, sc, -jnp.inf)
                    mx = jnp.maximum(jnp.max(sc, axis=-1, keepdims=True), sink)
                    e = jnp.exp(sc - mx)
                    inv_den = 1.0 / (jnp.sum(e, axis=-1, keepdims=True) + jnp.exp(sink - mx))
                    o2 = _dot((e * inv_den).astype(BF16), v2[sb * w:(sb + 2) * w])
                    o_ref[0, sb * w:(sb + 1) * w, qb * PAIR:(qb + 1) * PAIR] = (
                        jnp.where(lane_is_a, o2[0:w], o2[w:]).astype(o_ref.dtype))


def _attn(proj3, cos_tab, sin_tab, q_gain, k_gain, sinks, tq):
    b, s, _ = proj3.shape
    w = WINDOW
    kvw = 2 * ATTN_KV_DIM
    hb = tq // w
    cur = lambda width, col: pl.BlockSpec((1, tq, width), lambda bi, c: (bi, c, col))
    halo = lambda width, col: pl.BlockSpec((1, w, width), lambda bi, c: (bi, jnp.maximum(c * hb - 1, 0), col))
    vec = pl.BlockSpec((1, PAIR), lambda bi, c: (0, 0))
    return pl.pallas_call(
        _attn_kernel,
        grid=(b, s // tq),
        in_specs=[
            pl.BlockSpec(memory_space=pltpu.SMEM),
            cur(BRANCH_DIM, COL_Q // BRANCH_DIM), cur(kvw, COL_KV // kvw), halo(kvw, COL_KV // kvw),
            cur(PAIR, 0), cur(PAIR, 0), halo(PAIR, 0), halo(PAIR, 0), vec, vec,
        ],
        out_specs=pl.BlockSpec((1, tq, BRANCH_DIM), lambda bi, c: (bi, c, 0)),
        out_shape=jax.ShapeDtypeStruct((b, s, BRANCH_DIM), BF16),
        compiler_params=_cp("parallel", "arbitrary"),
        name="attn",
    )(sinks, proj3, proj3, proj3, cos_tab, sin_tab, cos_tab, sin_tab, q_gain, k_gain)


def _merge_kernel(a_ref, b_ref, c_ref, g0_ref, g1_ref, g2_ref, w_ref, o_ref):
    acc = _sigmoid(g0_ref[...]) * _dot(a_ref[...], w_ref[0])
    acc = acc + _sigmoid(g1_ref[...]) * _dot(b_ref[...], w_ref[1])
    acc = acc + _sigmoid(g2_ref[...]) * _dot(c_ref[...], w_ref[2])
    o_ref[...] = acc.astype(o_ref.dtype)


def _merge(a_out, b_out, c_out, proj, w_branch_bf16, tm, tn):
    n = proj.shape[0]
    br = lambda: pl.BlockSpec((tm, BRANCH_DIM), lambda i, j: (i, 0))
    gate = lambda g: pl.BlockSpec((tm, tn), lambda i, j: (i, (COL_G + g * D_MODEL) // tn + j))
    return pl.pallas_call(
        _merge_kernel,
        grid=(n // tm, D_MODEL // tn),
        in_specs=[br(), br(), br(), gate(0), gate(1), gate(2),
                  pl.BlockSpec((N_BRANCHES, BRANCH_DIM, tn), lambda i, j: (0, 0, j))],
        out_specs=pl.BlockSpec((tm, tn), lambda i, j: (i, j)),
        out_shape=jax.ShapeDtypeStruct((n, D_MODEL), BF16),
        compiler_params=_cp("parallel", "arbitrary"),
        name="merge",
    )(a_out, b_out, c_out, proj, proj, proj, w_branch_bf16)


def _out_proj_kernel(m_ref, x_ref, w_ref, g_ref, wr_ref, x1_ref, hn_ref, lg_ref):
    x1 = x_ref[...] + _dot(m_ref[...], w_ref[...])
    x1_ref[...] = x1
    ms = jnp.mean(x1 * x1, axis=-1, keepdims=True)
    hn = x1 * lax.rsqrt(ms + NORM_EPS) * g_ref[...]
    hn_ref[...] = hn
    hh, hl = _split_bf16(hn)
    wh, wl = _split_bf16(wr_ref[...])
    lg_ref[...] = _dot_nt(wh, hh) + _dot_nt(wh, hl) + _dot_nt(wl, hh)


def _out_proj(merged, x2d, w_out_bf16, ffn_gain, w_router_t, tm):
    n, d = x2d.shape
    return pl.pallas_call(
        _out_proj_kernel,
        grid=(n // tm,),
        in_specs=[
            pl.BlockSpec((tm, d), lambda i: (i, 0)),
            pl.BlockSpec((tm, d), lambda i: (i, 0)),
            pl.BlockSpec((d, d), lambda i: (0, 0)),
            pl.BlockSpec((1, d), lambda i: (0, 0)),
            pl.BlockSpec((ROUTER_ROWS, d), lambda i: (0, 0)),
        ],
        out_specs=[
            pl.BlockSpec((tm, d), lambda i: (i, 0)),
            pl.BlockSpec((tm, d), lambda i: (i, 0)),
            pl.BlockSpec((ROUTER_ROWS, tm), lambda i: (0, i)),
        ],
        out_shape=[
            jax.ShapeDtypeStruct((n, d), F32),
            jax.ShapeDtypeStruct((n, d), F32),
            jax.ShapeDtypeStruct((ROUTER_ROWS, n), F32),
        ],
        compiler_params=_cp("parallel"),
        name="out_proj",
    )(merged, x2d, w_out_bf16, ffn_gain.reshape(1, d), w_router_t)


def _experts_kernel(be_ref, nu_ref, x_ref, w1_ref, w3_ref, w2_ref, o_ref, w1b_ref, w3b_ref, w2b_ref):
    i = pl.program_id(0)
    new_expert = (i == 0) | (be_ref[i] != be_ref[jnp.maximum(i - 1, 0)])

    @pl.when(new_expert)
    def _():
        w1b_ref[...] = w1_ref[0, 0].astype(BF16)
        w3b_ref[...] = w3_ref[0, 0].astype(BF16)
        w2b_ref[...] = w2_ref[0, 0].astype(BF16)

    used = i < nu_ref[0]

    @pl.when(used)
    def _():
        xb = x_ref[...].astype(BF16)
        h1 = _dot(xb, w1b_ref[...])
        h3 = _dot(xb, w3b_ref[...])
        act = (h1 * _sigmoid(h1)) * h3
        o_ref[...] = _dot(act.astype(BF16), w2b_ref[...])

    @pl.when(jnp.logical_not(used))
    def _():
        o_ref[...] = jnp.zeros_like(o_ref)


def _experts(xg, block_expert, n_used, w1, w3, w2, layer):
    n_rows, d = xg.shape
    ff = w1.shape[-1]
    blk = MOE_BLOCK
    return pl.pallas_call(
        _experts_kernel,
        grid_spec=pltpu.PrefetchScalarGridSpec(
            num_scalar_prefetch=2,
            grid=(n_rows // blk,),
            in_specs=[
                pl.BlockSpec((blk, d), lambda i, be, nu: (jnp.minimum(i, nu[0] - 1), 0)),
                pl.BlockSpec((1, 1, d, ff), lambda i, be, nu: (layer, be[i], 0, 0)),
                pl.BlockSpec((1, 1, d, ff), lambda i, be, nu: (layer, be[i], 0, 0)),
                pl.BlockSpec((1, 1, ff, d), lambda i, be, nu: (layer, be[i], 0, 0)),
            ],
            out_specs=pl.BlockSpec((blk, d), lambda i, be, nu: (i, 0)),
            scratch_shapes=[pltpu.VMEM((d, ff), BF16), pltpu.VMEM((d, ff), BF16), pltpu.VMEM((ff, d), BF16)],
        ),
        out_shape=jax.ShapeDtypeStruct((n_rows, d), F32),
        compiler_params=_cp("arbitrary"),
        name="experts",
    )(block_expert, n_used, xg, w1, w3, w2)


def _pack_segments():
    c = BRANCH_DIM
    o_ad, o_gd = 3 * c + DECAY_LORA, 3 * c + DECAY_LORA + ICLR_LORA
    o_u = o_gd + GATE_LORA
    o_q = o_u + c
    o_k = o_q + c
    o_g = o_k + 2 * ATTN_KV_DIM
    segs = [(0, 0, LANES),
            (3 * c, 3 * c, DECAY_LORA),
            (3 * c + LORA_PAD, o_ad, ICLR_LORA),
            (3 * c + 2 * LORA_PAD, o_gd, LANES),
            (COL_U, o_u, LANES), (COL_KV, o_k, LANES), (COL_Q, o_q, LANES), (COL_G, o_g, LANES)]
    return [(p // LANES, src, keep) for p, src, keep in segs]


def _pack_w_in_kernel(a_ref, b_ref, o_ref):
    blk = pl.program_id(0)
    lane = lax.broadcasted_iota(jnp.int32, (1, LANES), 1)
    segs = _pack_segments()
    for si, (first, src, keep) in enumerate(segs):
        last = segs[si + 1][0] if si + 1 < len(segs) else P_COLS // LANES
        shift = (-src) % LANES

        @pl.when((blk >= first) & (blk < last))
        def _(shift=shift, keep=keep):
            if shift == 0:
                x = a_ref[0]
            else:
                x = jnp.where(lane < shift, pltpu.roll(a_ref[0], shift, 1), pltpu.roll(b_ref[0], shift, 1))
            if keep < LANES:
                x = jnp.where(lane < keep, x, 0.0)
            o_ref[...] = x.astype(o_ref.dtype)


def _pack_w_in(w, layer):
    _, d, cols = w.shape
    segs = _pack_segments()
    last_src = (cols - 1) // LANES

    def src_block(blk):
        q = blk
        for first, src, _ in segs:
            q = jnp.where(blk >= first, blk - first + src // LANES, q)
        return q

    return pl.pallas_call(
        _pack_w_in_kernel,
        grid=(P_COLS // LANES,),
        in_specs=[pl.BlockSpec((1, d, LANES), lambda i: (layer, 0, src_block(i))),
                  pl.BlockSpec((1, d, LANES), lambda i: (layer, 0, jnp.minimum(src_block(i) + 1, last_src)))],
        out_specs=pl.BlockSpec((d, LANES), lambda i: (0, i)),
        out_shape=jax.ShapeDtypeStruct((d, P_COLS), BF16),
        compiler_params=_cp("parallel"),
        name="pack_w_in",
    )(w, w)


def _pack_mu(mu):
    z32 = jnp.zeros((LORA_PAD - DECAY_LORA,), mu.dtype)
    c = BRANCH_DIM
    o_wd, o_ad, o_gd = 3 * c, 3 * c + DECAY_LORA, 3 * c + DECAY_LORA + ICLR_LORA
    return jnp.concatenate([mu[:o_wd], mu[o_wd:o_ad], z32, mu[o_ad:o_gd], z32, mu[o_gd:]]).reshape(1, Z_COLS)


def _pad_rows(w, rows):
    return jnp.concatenate([w, jnp.zeros((rows - w.shape[0], w.shape[1]), w.dtype)], axis=0)


def _router_weights(w_grp, w_exp):
    d = w_grp.shape[0]
    pad = jnp.zeros((d, 8 - N_GROUPS), w_grp.dtype)
    return jnp.concatenate([w_grp, pad, w_exp], axis=1).T


def _route_kernel(lg_ref, bias_ref, ids_ref, gcol_ref, cnt_ref, carry_ref):
    tm = lg_ref.shape[1]

    @pl.when(pl.program_id(0) == 0)
    def _():
        carry_ref[...] = jnp.zeros_like(carry_ref)

    lg = lg_ref[...] + bias_ref[...]
    row8 = lax.broadcasted_iota(jnp.int32, (8, tm), 0)
    row8f = row8.astype(F32)
    neg_inf = -jnp.inf

    def first_argmax(x):
        mx = jnp.max(x, axis=0, keepdims=True)
        return mx, jnp.min(jnp.where(x == mx, row8f, 8.0), axis=0, keepdims=True).astype(jnp.int32)

    grp_logits = jnp.where(row8 < N_GROUPS, lg[0:8], neg_inf)
    gmax, grp = first_argmax(grp_logits)
    p_grp = 1.0 / jnp.sum(jnp.exp(grp_logits - gmax), axis=0, keepdims=True)
    in_grp = lg[8:16]
    for g in range(1, N_GROUPS):
        in_grp = jnp.where(grp == g, lg[8 + 8 * g:16 + 8 * g], in_grp)
    m1, i1 = first_argmax(in_grp)
    rest = jnp.where(row8 == i1, neg_inf, in_grp)
    m2, i2 = first_argmax(rest)
    e2 = jnp.exp(m2 - m1)
    gate1 = p_grp / (1.0 + e2)
    gate2 = p_grp * e2 / (1.0 + e2)
    exp1 = grp * EXPERTS_PER_GROUP + i1
    exp2 = grp * EXPERTS_PER_GROUP + i2

    rows = lax.broadcasted_iota(jnp.int32, (N_EXPERTS, tm), 0)
    hot1 = (rows == exp1).astype(F32)
    hot2 = (rows == exp2).astype(F32)
    cnt = (hot1 + hot2).astype(BF16)
    src = lax.broadcasted_iota(jnp.int32, (tm, tm), 0)
    dst = lax.broadcasted_iota(jnp.int32, (tm, tm), 1)
    before = _dot(cnt, (src < dst).astype(BF16)) + carry_ref[...]
    rank1 = jnp.sum(hot1 * before, axis=0, keepdims=True).astype(jnp.int32)
    rank2 = jnp.sum(hot2 * before, axis=0, keepdims=True).astype(jnp.int32)
    carry_ref[...] += _dot(cnt, jnp.ones((tm, tm), BF16))
    cnt_ref[...] = carry_ref[...]

    ids_ref[...] = jnp.where(row8 == 0, exp1, jnp.where(row8 == 1, exp2,
                             jnp.where(row8 == 2, rank1, jnp.where(row8 == 3, rank2, 0))))
    row128 = lax.broadcasted_iota(jnp.int32, (LANES, tm), 0)
    gates_t = jnp.where(row128 == 0, gate1, jnp.where(row128 == 1, gate2, 0.0))
    gcol_ref[...] = gates_t.T


def _route(logits_t, bias_col, tm):
    n = logits_t.shape[1]
    bias = jnp.broadcast_to(bias_col, (ROUTER_ROWS, tm))
    return pl.pallas_call(
        _route_kernel,
        grid=(n // tm,),
        in_specs=[pl.BlockSpec((ROUTER_ROWS, tm), lambda i: (0, i)),
                  pl.BlockSpec((ROUTER_ROWS, tm), lambda i: (0, 0))],
        out_specs=[pl.BlockSpec((8, tm), lambda i: (0, i)),
                   pl.BlockSpec((tm, LANES), lambda i: (i, 0)),
                   pl.BlockSpec((N_EXPERTS, tm), lambda i: (0, 0))],
        out_shape=[jax.ShapeDtypeStruct((8, n), jnp.int32),
                   jax.ShapeDtypeStruct((n, LANES), F32),
                   jax.ShapeDtypeStruct((N_EXPERTS, tm), F32)],
        scratch_shapes=[pltpu.VMEM((N_EXPERTS, tm), F32)],
        compiler_params=_cp("arbitrary"),
        name="route",
    )(logits_t, bias)


def _row_copy(src_ref, src_row, dst_ref, dst_row, sem):
    return pltpu.make_async_copy(src_ref.at[pl.ds(src_row, 1)], dst_ref.at[pl.ds(dst_row, 1)], sem)


def _scatter_kernel(dest_ref, pad_start_ref, pad_len_ref, hn_ref, xg_ref, zero_ref, sems, pad_sem, *, tm):
    n = hn_ref.shape[0]
    base = pl.program_id(0) * tm

    @pl.when(pl.program_id(0) == 0)
    def _():
        zero_ref[...] = jnp.zeros_like(zero_ref)

        def each_pad_row(fn):
            def per_expert(e, carry):
                lax.fori_loop(0, pad_len_ref[e], lambda j, c: fn(pad_start_ref[e] + j, c), 0)
                return carry
            lax.fori_loop(0, pad_len_ref.shape[0], per_expert, 0)

        def start(row, carry):
            _row_copy(zero_ref, 0, xg_ref, row, pad_sem).start()
            return carry

        def wait(row, carry):
            _row_copy(zero_ref, 0, xg_ref, row, pad_sem).wait()
            return carry

        each_pad_row(start)
        each_pad_row(wait)

    step = pl.program_id(0)
    slot = step % 2

    def issue(r, carry):
        for k in range(TOP_K):
            _row_copy(hn_ref, base + r, xg_ref, dest_ref[k * n + base + r], sems.at[slot]).start()
        return carry

    def wait_step(sl):
        for _ in range(TOP_K):
            pltpu.make_async_copy(hn_ref.at[pl.ds(0, tm)], xg_ref.at[pl.ds(0, tm)], sems.at[sl]).wait()

    lax.fori_loop(0, tm, issue, 0, unroll=ISSUE_UNROLL)

    @pl.when(step > 0)
    def _():
        wait_step(1 - slot)

    @pl.when(step == pl.num_programs(0) - 1)
    def _():
        wait_step(slot)


def _scatter(dest_flat, pad_start, pad_len, hn, n_rows, tm):
    n, d = hn.shape
    return pl.pallas_call(
        functools.partial(_scatter_kernel, tm=tm),
        grid_spec=pltpu.PrefetchScalarGridSpec(
            num_scalar_prefetch=3,
            grid=(n // tm,),
            in_specs=[pl.BlockSpec(memory_space=pl.ANY)],
            out_specs=pl.BlockSpec(memory_space=pl.ANY),
            scratch_shapes=[pltpu.VMEM((8, d), hn.dtype), pltpu.SemaphoreType.DMA((2,)), pltpu.SemaphoreType.DMA],
        ),
        out_shape=jax.ShapeDtypeStruct((n_rows, d), hn.dtype),
        compiler_params=_cp("arbitrary"),
        name="moe_scatter",
    )(dest_flat, pad_start, pad_len, hn)


def _combine_kernel(dest_ref, x_ref, gcol_ref, yg_ref, o_ref, buf_ref, sems):
    tm = x_ref.shape[0]
    n_steps = pl.num_programs(0)
    n = n_steps * tm
    step = pl.program_id(0)
    slot = step % 2

    def gather(for_step, into):
        base = for_step * tm

        def issue(r, carry):
            for k in range(TOP_K):
                _row_copy(yg_ref, dest_ref[k * n + base + r], buf_ref.at[into, k], r, sems.at[into]).start()
            return carry

        lax.fori_loop(0, tm, issue, 0, unroll=ISSUE_UNROLL)

    @pl.when(step == 0)
    def _():
        gather(step, slot)

    @pl.when(step + 1 < n_steps)
    def _():
        gather(step + 1, 1 - slot)

    for k in range(TOP_K):
        pltpu.make_async_copy(yg_ref.at[pl.ds(0, tm)], buf_ref.at[slot, k], sems.at[slot]).wait()
    g = gcol_ref[...]
    o_ref[...] = x_ref[...] + g[:, 0:1] * buf_ref[slot, 0] + g[:, 1:2] * buf_ref[slot, 1]


def _combine(dest_flat, x1, gcol, yg, tm):
    n, d = x1.shape
    return pl.pallas_call(
        _combine_kernel,
        grid_spec=pltpu.PrefetchScalarGridSpec(
            num_scalar_prefetch=1,
            grid=(n // tm,),
            in_specs=[pl.BlockSpec((tm, d), lambda i, dest: (i, 0)),
                      pl.BlockSpec((tm, LANES), lambda i, dest: (i, 0)),
                      pl.BlockSpec(memory_space=pl.ANY)],
            out_specs=pl.BlockSpec((tm, d), lambda i, dest: (i, 0)),
            scratch_shapes=[pltpu.VMEM((2, TOP_K, tm, d), F32), pltpu.SemaphoreType.DMA((2,))],
        ),
        out_shape=jax.ShapeDtypeStruct((n, d), F32),
        compiler_params=_cp("arbitrary"),
        name="moe_combine",
    )(dest_flat, x1, gcol, yg)


def _moe(x1, hn, logits_t, bias_col, w1, w3, w2, layer):
    n, d = hn.shape
    blk = MOE_BLOCK
    tm = _pick(n, 512)
    ids, gcol, cnt = _route(logits_t, bias_col, tm)
    counts = cnt[:, 0].astype(jnp.int32)
    padded = (counts + blk - 1) // blk * blk
    pends = jnp.cumsum(padded)
    pstarts = pends - padded
    experts = jnp.arange(N_EXPERTS, dtype=jnp.int32)
    seg_start = jnp.sum(jnp.where(ids[0:TOP_K, :, None] == experts, pstarts, 0), axis=-1)
    dest_flat = (seg_start + ids[TOP_K:2 * TOP_K]).reshape(-1)
    n_rows = n * TOP_K + N_EXPERTS * blk
    block_start = jnp.arange(n_rows // blk, dtype=jnp.int32) * blk
    block_expert = jnp.minimum(jnp.sum((block_start[:, None] >= pends[None, :]).astype(jnp.int32), axis=1),
                               N_EXPERTS - 1)
    n_used = (pends[N_EXPERTS - 1:] // blk).astype(jnp.int32)
    pad_start = jnp.concatenate([pstarts + counts, pends[N_EXPERTS - 1:]])
    pad_len = jnp.concatenate([padded - counts, n_rows - pends[N_EXPERTS - 1:]])
    xg = _scatter(dest_flat, pad_start, pad_len, hn, n_rows, tm)
    yg = _experts(xg, block_expert, n_used, w1, w3, w2, layer)
    return _combine(dest_flat, x1, gcol, yg, tm)


def _pick(n, pref):
    t = pref
    while n % t:
        t //= 2
    return t


def _layer(x2d, b, s, cos_tab, sin_tab, p, rwkv_precision):
    n = x2d.shape[0]
    proj = _norm_matmul(x2d, p['attn_norm'], p['w_in'], _pick(n, 1024), 1024)
    proj3 = proj.reshape(b, s, P_COLS)
    a_out = _rwkv(proj3, p['mu'], p['w0'], p['a0'], p['k_k'], p['k_a'], p['r_k'], p['ln_g'], p['ln_b'],
                  p['w_up'], p['a_up'], p['g_up'], _pick(s, 512), rwkv_precision)
    b_out = _pool(proj3, p['pool_w'], p['pool_scale'], _pick(s, 512))
    c_out = _attn(proj3, cos_tab, sin_tab, p['q_gain'], p['k_gain'], p['sinks'], _pick(s, 256))
    merged = _merge(a_out.reshape(n, -1), b_out.reshape(n, -1), c_out.reshape(n, -1), proj,
                    p['w_branch'], _pick(n, 1024), 512)
    x1, hn, logits_t = _out_proj(merged, x2d, p['w_out'], p['ffn_norm'], p['w_router_t'], _pick(n, 512))
    return _moe(x1, hn, logits_t, p['router_bias'], p['w1'], p['w3'], p['w2'], p['layer'])


def kernel(x, positions, attn_norm, w_in, tmix_mu, rwkv_w0, rwkv_w_up, rwkv_a0, rwkv_a_up, rwkv_g_up, rwkv_k_k, rwkv_k_a, rwkv_r_k, rwkv_ln_g, rwkv_ln_b, pool_w, pool_scale, q_norm, k_norm, attn_sinks, w_branch, w_out, ffn_norm, router_grp_w, router_grp_b, router_exp_w, router_exp_b, expert_w1, expert_w3, expert_w2):
    b, s, d = x.shape
    n = b * s
    half = HEAD_DIM // 2
    inv_freq = ROPE_THETA ** (-jnp.arange(half, dtype=F32) / half)
    inv_freq = jnp.tile(inv_freq, PAIR // half).reshape(1, PAIR)
    pos_lanes = jnp.broadcast_to(positions.astype(F32).reshape(n, 1), (n, PAIR))
    cos_tab, sin_tab = _rope_tables(pos_lanes, inv_freq, _pick(n, 2048))
    cos_tab = cos_tab.reshape(b, s, PAIR)
    sin_tab = sin_tab.reshape(b, s, PAIR)
    x2d = x.reshape(n, d)
    for l in range(w_in.shape[0]):
        p = {
            'attn_norm': attn_norm[l],
            'layer': l,
            'w_in': _pack_w_in(w_in, l),
            'mu': _pack_mu(tmix_mu[l]),
            'w0': rwkv_w0[l], 'a0': rwkv_a0[l], 'k_k': rwkv_k_k[l], 'k_a': rwkv_k_a[l],
            'r_k': rwkv_r_k[l].reshape(-1), 'ln_g': rwkv_ln_g[l], 'ln_b': rwkv_ln_b[l],
            'w_up': _pad_rows(rwkv_w_up[l], LORA_PAD), 'a_up': _pad_rows(rwkv_a_up[l], LORA_PAD),
            'g_up': rwkv_g_up[l],
            'pool_w': pool_w[l].astype(BF16), 'pool_scale': pool_scale[l],
            'q_gain': jnp.tile(q_norm[l], PAIR // HEAD_DIM).reshape(1, PAIR),
            'k_gain': jnp.tile(k_norm[l], PAIR // HEAD_DIM).reshape(1, PAIR),
            'sinks': attn_sinks[l],
            'w_branch': w_branch[l].astype(BF16), 'w_out': w_out[l].astype(BF16),
            'ffn_norm': ffn_norm[l],
            'w_router_t': _router_weights(router_grp_w[l], router_exp_w[l]),
            'router_bias': jnp.concatenate([router_grp_b[l], jnp.zeros((8 - N_GROUPS,), F32),
                                            router_exp_b[l]]).reshape(ROUTER_ROWS, 1),
            'w1': expert_w1, 'w3': expert_w3, 'w2': expert_w2,
        }
        x2d = _layer(x2d, b, s, cos_tab, sin_tab, p, None)
    return x2d.reshape(b, s, d)
```

```python
import functools

import numpy as np
import jax
import jax.numpy as jnp
from jax import lax
from jax.experimental import pallas as pl
from jax.experimental.pallas import tpu as pltpu

F32 = jnp.float32
BF16 = jnp.bfloat16
HI = lax.Precision.HIGHEST

D_MODEL = 2048
HEAD_DIM = 64
BRANCH_DIM = D_MODEL // 2
DECAY_LORA = 96
ICLR_LORA = 96
GATE_LORA = 256
LORA_PAD = 128
POOL_WINDOWS = (2, 4, 8, 16)
POOL_GROUP_DIM = BRANCH_DIM // len(POOL_WINDOWS)
POOL_HALO = 16
ATTN_Q_HEADS = BRANCH_DIM // HEAD_DIM
ATTN_KV_HEADS = 4
ATTN_KV_DIM = ATTN_KV_HEADS * HEAD_DIM
WINDOW = 128
ROPE_THETA = 10000.0
N_BRANCHES = 3
N_GROUPS = 4
EXPERTS_PER_GROUP = 8
N_EXPERTS = N_GROUPS * EXPERTS_PER_GROUP
TOP_K = 2
EXPERT_FF = 512
NORM_EPS = 1e-6
GN_EPS = 64e-5

LANES = 128
PAIR = 2 * HEAD_DIM
CHUNK = 64
RWKV_PAIRS_PER_STEP = 4
ROUTER_ROWS = 40
MOE_BLOCK = 256
ISSUE_UNROLL = 8

Z_COLS = 3 * BRANCH_DIM + 2 * LORA_PAD + GATE_LORA
COL_U = Z_COLS
COL_KV = COL_U + BRANCH_DIM
COL_Q = COL_KV + 2 * ATTN_KV_DIM
COL_G = COL_Q + BRANCH_DIM
P_COLS = COL_G + N_BRANCHES * D_MODEL

VMEM_LIMIT = 56 * 1024 * 1024


def _cp(*sem):
    return pltpu.CompilerParams(dimension_semantics=sem, vmem_limit_bytes=VMEM_LIMIT)


def _dot(a, b, precision=None):
    return jnp.dot(a, b, preferred_element_type=F32, precision=precision)


def _dot_nt(a, b, precision=None):
    return lax.dot_general(a, b, (((1,), (1,)), ((), ())), preferred_element_type=F32, precision=precision)


def _sigmoid(x):
    return 1.0 / (1.0 + jnp.exp(-x))


def _split_bf16(x):
    hi = x.astype(BF16)
    return hi, (x - hi.astype(F32)).astype(BF16)


def _dot_split_lhs(x, m):
    m = m.astype(BF16)
    return _dot(jnp.concatenate(_split_bf16(x), axis=1), jnp.concatenate([m, m], axis=0))


def _dot_split3(x, w):
    xh, xl = _split_bf16(x)
    wh, wl = _split_bf16(w)
    return _dot(jnp.concatenate([xh, xl, xh], axis=1), jnp.concatenate([wh, wh, wl], axis=0))


def _head_block_diag(scale):
    r = lax.broadcasted_iota(jnp.int32, (PAIR, PAIR), 0) // HEAD_DIM
    c = lax.broadcasted_iota(jnp.int32, (PAIR, PAIR), 1) // HEAD_DIM
    return jnp.where(r == c, scale, 0.0).astype(F32)


def _norm_matmul_kernel(x_ref, g_ref, w_ref, o_ref, h_ref):
    @pl.when(pl.program_id(1) == 0)
    def _():
        x = x_ref[...]
        ms = jnp.mean(x * x, axis=-1, keepdims=True)
        h_ref[...] = (x * lax.rsqrt(ms + NORM_EPS) * g_ref[...]).astype(BF16)

    o_ref[...] = _dot(h_ref[...], w_ref[...])


def _norm_matmul(x2d, gain, w_bf16, tm, tn):
    n, d = x2d.shape
    cols = w_bf16.shape[1]
    return pl.pallas_call(
        _norm_matmul_kernel,
        grid=(n // tm, cols // tn),
        in_specs=[
            pl.BlockSpec((tm, d), lambda i, j: (i, 0)),
            pl.BlockSpec((1, d), lambda i, j: (0, 0)),
            pl.BlockSpec((d, tn), lambda i, j: (0, j)),
        ],
        out_specs=pl.BlockSpec((tm, tn), lambda i, j: (i, j)),
        out_shape=jax.ShapeDtypeStruct((n, cols), F32),
        scratch_shapes=[pltpu.VMEM((tm, d), BF16)],
        compiler_params=_cp("arbitrary", "arbitrary"),
        name="norm_matmul",
    )(x2d, gain.reshape(1, d), w_bf16)


def _stack_heads(x, lane_is_a):
    return jnp.concatenate([jnp.where(lane_is_a, x, 0.0), jnp.where(lane_is_a, 0.0, x)], axis=0)


def _rwkv_kernel(zr_ref, zk_ref, zv_ref, zl_ref, mur_ref, muk_ref, muv_ref, mul_ref,
                 w0_ref, a0_ref, kkg_ref, ka_ref, rk_ref, lng_ref, lnb_ref,
                 wup_ref, aup_ref, gup_ref, o_ref,
                 h_ref, cr_ref, ck_ref, cv_ref, cl_ref, *, precision):
    t_rows = zr_ref.shape[1]

    @pl.when(pl.program_id(2) == 0)
    def _():
        h_ref[...] = jnp.zeros_like(h_ref)
        cr_ref[...] = jnp.zeros_like(cr_ref)
        ck_ref[...] = jnp.zeros_like(ck_ref)
        cv_ref[...] = jnp.zeros_like(cv_ref)
        cl_ref[...] = jnp.zeros_like(cl_ref)

    row = lax.broadcasted_iota(jnp.int32, (t_rows, 1), 0)

    def shifted(z_ref, carry_ref, mu_ref):
        z = z_ref[0]
        prev = jnp.where(row == 0, carry_ref[...], pltpu.roll(z, 1, 0))
        carry_ref[...] = z_ref[0, t_rows - 1:t_rows, :]
        return z + (prev - z) * mu_ref[...]

    r = shifted(zr_ref, cr_ref, mur_ref)
    k = shifted(zk_ref, ck_ref, muk_ref)
    v = shifted(zv_ref, cv_ref, muv_ref)
    zl = shifted(zl_ref, cl_ref, mul_ref)
    wd = zl[:, 0:LORA_PAD]
    ad = zl[:, LORA_PAD:2 * LORA_PAD]
    gd = zl[:, 2 * LORA_PAD:]

    head_sum = _head_block_diag(1.0)
    head_avg = _head_block_diag(1.0 / HEAD_DIM)
    n_sub = zr_ref.shape[2] // PAIR
    lanes = [slice(j * PAIR, (j + 1) * PAIR) for j in range(n_sub)]

    def per_head(x, m):
        return jnp.concatenate([_dot_split_lhs(x[:, ln], m) for ln in lanes], axis=1)

    w_pre = w0_ref[...] + _dot_split3(jnp.tanh(wd), wup_ref[...])
    neg = -w_pre
    softplus = jnp.maximum(neg, 0.0) + jnp.log(1.0 + jnp.exp(-jnp.abs(neg)))
    log_decay = -jnp.exp(-softplus - 0.5)
    a = _sigmoid(a0_ref[...] + _dot_split3(ad, aup_ref[...]))
    gate = _dot_split3(_sigmoid(gd), gup_ref[...])
    kk = k * kkg_ref[...]
    kk = kk * lax.rsqrt(jnp.maximum(per_head(kk * kk, head_sum), 1e-24))
    k = k * (1.0 + (a - 1.0) * ka_ref[...])
    bonus = per_head(r * k * rk_ref[...], head_sum) * v

    c = CHUNK
    ri = lax.broadcasted_iota(jnp.int32, (c, c), 0)
    ci = lax.broadcasted_iota(jnp.int32, (c, c), 1)
    tri_incl = (ci <= ri).astype(F32)
    tri2 = jnp.concatenate([tri_incl, tri_incl], axis=1).astype(BF16)
    lane_is_a = lax.broadcasted_iota(jnp.int32, (1, PAIR), 1) < HEAD_DIM
    rs = lax.broadcasted_iota(jnp.int32, (PAIR, PAIR), 0)
    cs = lax.broadcasted_iota(jnp.int32, (PAIR, PAIR), 1)
    strict = (cs % c) < (rs % c)
    incl = (cs % c) <= (rs % c)
    eye = rs == cs
    eye_f = eye.astype(F32)
    if precision is None:
        dot = lambda x, y: _dot(x.astype(BF16), y.astype(BF16))
        dot_nt = lambda x, y: _dot_nt(x.astype(BF16), y.astype(BF16))
    else:
        dot = functools.partial(_dot, precision=precision)
        dot_nt = functools.partial(_dot_nt, precision=precision)

    n_ch = t_rows // c
    sls = [slice(ch * c, (ch + 1) * c) for ch in range(n_ch)]
    cums = [_dot(tri2, jnp.concatenate(_split_bf16(log_decay[sl]), axis=0)) for sl in sls]
    pre = []
    for sl, cum_all in zip(sls, cums):
        for ln in lanes:
            cum = cum_all[:, ln]
            lw = log_decay[sl, ln]
            tot = cum[c - 1:c, :]
            p_in = jnp.exp(cum)
            p_ex = jnp.exp(cum - lw)
            q_inv = jnp.exp(-cum)
            q_end = jnp.exp(tot - cum)
            kk_c, a_c, k_c = kk[sl, ln], a[sl, ln], k[sl, ln]
            beta = kk_c * a_c
            pre.append(dict(
                tot=tot,
                at_s=_stack_heads(-kk_c * p_ex, lane_is_a),
                rt_s=_stack_heads(r[sl, ln] * p_in, lane_is_a),
                bh_s=_stack_heads(beta * q_inv, lane_is_a),
                kh_s=_stack_heads(k_c * q_inv, lane_is_a),
                be_s=_stack_heads(beta * q_end, lane_is_a),
                ke_s=_stack_heads(k_c * q_end, lane_is_a),
                v_s=_stack_heads(v[sl, ln], lane_is_a)))
    gms = [dot_nt(jnp.concatenate([d['at_s'], d['rt_s']], axis=0),
                  jnp.concatenate([d['bh_s'], d['kh_s']], axis=0)) for d in pre]
    a_ab = [jnp.where(strict, g[0:PAIR, 0:PAIR], 0.0) for g in gms]
    a_ak = [jnp.where(strict, g[0:PAIR, PAIR:], 0.0) for g in gms]
    a_rb = [jnp.where(incl, g[PAIR:, 0:PAIR], 0.0) for g in gms]
    a_rk = [jnp.where(incl, g[PAIR:, PAIR:], 0.0) for g in gms]
    pw = [dot(x, x) for x in a_ab]
    xv = [dot(jnp.concatenate([ak, rk, d['ke_s'].T], axis=0), d['v_s'])
          for ak, rk, d in zip(a_ak, a_rk, pre)]
    inv = [eye_f + x for x in a_ab]
    for _ in range(4):
        both = [dot(jnp.concatenate([p, x], axis=0), p) for p, x in zip(pw, inv)]
        pw = [t[0:PAIR] for t in both]
        inv = [x + t[PAIR:] for x, t in zip(inv, both)]
    inv = [x + dot(x, p) for x, p in zip(inv, pw)]
    wu = [dot(x, jnp.concatenate([d['at_s'], t[0:PAIR]], axis=1)) for x, d, t in zip(inv, pre, xv)]
    xw = [dot(jnp.concatenate([rb, d['be_s'].T], axis=0), y) for rb, d, y in zip(a_rb, pre, wu)]
    chunks = []
    for i in range(len(pre)):
        rt2 = pre[i]['rt_s'] + xw[i][0:PAIR, 0:PAIR]
        y0 = xw[i][0:PAIR, PAIR:] + xv[i][PAIR:2 * PAIR]
        m = jnp.where(eye, jnp.exp(pre[i]['tot']), 0.0) + xw[i][PAIR:, 0:PAIR]
        h0 = xw[i][PAIR:, PAIR:] + xv[i][2 * PAIR:]
        chunks.append((jnp.concatenate([rt2, m], axis=0), y0, h0))

    hs = [h_ref[j] for j in range(n_sub)]
    ys = [[] for _ in range(n_sub)]
    for i, (rm, y0, h0) in enumerate(chunks):
        j = i % n_sub
        t = dot(rm, hs[j])
        y_s = t[0:PAIR] + y0
        hs[j] = t[PAIR:] + h0
        ys[j].append(y_s[0:c] + y_s[c:])
    for j in range(n_sub):
        h_ref[j] = hs[j]

    y = jnp.concatenate([jnp.concatenate(yj, axis=0) for yj in ys], axis=1)
    mean = per_head(y, head_avg)
    yc = y - mean
    var = per_head(yc * yc, head_avg)
    gn = yc * lax.rsqrt(var + GN_EPS)
    o_ref[0] = ((gn * lng_ref[...] + lnb_ref[...] + bonus) * gate).astype(o_ref.dtype)


def _rwkv(proj3, mu_pack, w0, a0, k_k, k_a, r_k, ln_g, ln_b, w_up, a_up, g_up, t_rows, precision):
    b, s, _ = proj3.shape
    wide = RWKV_PAIRS_PER_STEP * PAIR
    n_blk = BRANCH_DIM // wide
    vec = lambda off: pl.BlockSpec((1, wide), lambda bi, p, c: (0, off + p))
    zcol = lambda off: pl.BlockSpec((1, t_rows, wide), lambda bi, p, c: (bi, c, off + p))
    lora_w = 2 * LORA_PAD + GATE_LORA
    lora_blk = (3 * BRANCH_DIM) // lora_w
    row = lambda x: x.reshape(1, BRANCH_DIM)
    kernel = functools.partial(_rwkv_kernel, precision=precision)
    return pl.pallas_call(
        kernel,
        grid=(b, n_blk, s // t_rows),
        in_specs=[
            zcol(0), zcol(n_blk), zcol(2 * n_blk),
            pl.BlockSpec((1, t_rows, lora_w), lambda bi, p, c: (bi, c, lora_blk)),
            vec(0), vec(n_blk), vec(2 * n_blk),
            pl.BlockSpec((1, lora_w), lambda bi, p, c: (0, lora_blk)),
            vec(0), vec(0), vec(0), vec(0), vec(0), vec(0), vec(0),
            pl.BlockSpec((LORA_PAD, wide), lambda bi, p, c: (0, p)),
            pl.BlockSpec((LORA_PAD, wide), lambda bi, p, c: (0, p)),
            pl.BlockSpec((GATE_LORA, wide), lambda bi, p, c: (0, p)),
        ],
        out_specs=pl.BlockSpec((1, t_rows, wide), lambda bi, p, c: (bi, c, p)),
        out_shape=jax.ShapeDtypeStruct((b, s, BRANCH_DIM), BF16),
        scratch_shapes=[
            pltpu.VMEM((RWKV_PAIRS_PER_STEP, PAIR, PAIR), F32),
            pltpu.VMEM((1, wide), F32), pltpu.VMEM((1, wide), F32), pltpu.VMEM((1, wide), F32),
            pltpu.VMEM((1, lora_w), F32),
        ],
        compiler_params=_cp("arbitrary", "arbitrary", "arbitrary"),
        name="rwkv",
    )(proj3, proj3, proj3, proj3, mu_pack, mu_pack, mu_pack, mu_pack,
      row(w0), row(a0), row(k_k), row(k_a), row(r_k), row(ln_g), row(ln_b), w_up, a_up, g_up)


def _pool_kernel(u0_ref, u1_ref, u2_ref, u3_ref, h0_ref, h1_ref, h2_ref, h3_ref, w_ref, sc_ref, o_ref, buf_ref):
    t_rows = u0_ref.shape[1]
    first = pl.program_id(1) == 0
    t = pl.program_id(1) * t_rows + lax.broadcasted_iota(jnp.int32, (t_rows, 1), 0)
    gd = POOL_GROUP_DIM
    for gi, (m, u_ref, halo_ref) in enumerate(zip(POOL_WINDOWS, (u0_ref, u1_ref, u2_ref, u3_ref),
                                                  (h0_ref, h1_ref, h2_ref, h3_ref))):
        u = u_ref[0]
        buf_ref[0:POOL_HALO, :] = jnp.where(first, 0.0, halo_ref[0])
        buf_ref[POOL_HALO:, :] = u
        acc = u
        for sft in range(1, m):
            acc = acc + buf_ref[pl.ds(POOL_HALO - sft, t_rows), :]
        count = jnp.minimum(t + 1, m).astype(F32)
        pooled = acc / count - u
        y = _dot(pooled.astype(BF16), w_ref[gi])
        o_ref[0, :, gi * gd:(gi + 1) * gd] = (y * sc_ref[:, gi * gd:(gi + 1) * gd]).astype(o_ref.dtype)


def _pool(proj3, pool_w_bf16, pool_scale, t_rows):
    b, s, _ = proj3.shape
    gd = POOL_GROUP_DIM
    ublk = COL_U // gd
    hb = t_rows // POOL_HALO
    u_spec = lambda gi: pl.BlockSpec((1, t_rows, gd), lambda bi, c: (bi, c, ublk + gi))
    h_spec = lambda gi: pl.BlockSpec((1, POOL_HALO, gd), lambda bi, c: (bi, jnp.maximum(c * hb - 1, 0), ublk + gi))
    return pl.pallas_call(
        _pool_kernel,
        grid=(b, s // t_rows),
        in_specs=[u_spec(0), u_spec(1), u_spec(2), u_spec(3), h_spec(0), h_spec(1), h_spec(2), h_spec(3),
                  pl.BlockSpec((len(POOL_WINDOWS), gd, gd), lambda bi, c: (0, 0, 0)),
                  pl.BlockSpec((1, BRANCH_DIM), lambda bi, c: (0, 0))],
        out_specs=pl.BlockSpec((1, t_rows, BRANCH_DIM), lambda bi, c: (bi, c, 0)),
        out_shape=jax.ShapeDtypeStruct((b, s, BRANCH_DIM), BF16),
        scratch_shapes=[pltpu.VMEM((t_rows + POOL_HALO, gd), F32)],
        compiler_params=_cp("parallel", "arbitrary"),
        name="pool",
    )(proj3, proj3, proj3, proj3, proj3, proj3, proj3, proj3, pool_w_bf16, pool_scale.reshape(1, BRANCH_DIM))


def _rope_table_kernel(pos_ref, invf_ref, cos_ref, sin_ref):
    ang = pos_ref[...] * invf_ref[...]
    lane = lax.broadcasted_iota(jnp.int32, (1, PAIR), 1)
    rope_lo = (lane % HEAD_DIM) < (HEAD_DIM // 2)
    cos_ref[...] = jnp.cos(ang)
    sin_ref[...] = jnp.where(rope_lo, -jnp.sin(ang), jnp.sin(ang))


def _rope_tables(pos_lanes, inv_freq, tm):
    n = pos_lanes.shape[0]
    blk = pl.BlockSpec((tm, PAIR), lambda i: (i, 0))
    return pl.pallas_call(
        _rope_table_kernel,
        grid=(n // tm,),
        in_specs=[blk, pl.BlockSpec((1, PAIR), lambda i: (0, 0))],
        out_specs=[blk, blk],
        out_shape=[jax.ShapeDtypeStruct((n, PAIR), F32), jax.ShapeDtypeStruct((n, PAIR), F32)],
        compiler_params=_cp("parallel"),
        name="rope_tables",
    )(pos_lanes, inv_freq)


def _attn_kernel(sink_ref, q_ref, kv_ref, kvh_ref, cos_ref, sin_ref, cosh_ref, sinh_ref, qg_ref, kg_ref, o_ref):
    w = WINDOW
    tq = q_ref.shape[1]
    not_first = pl.program_id(1) > 0
    head_avg = _head_block_diag(1.0 / HEAD_DIM)
    lane = lax.broadcasted_iota(jnp.int32, (1, PAIR), 1)
    lane_is_a = lane < HEAD_DIM
    rope_lo = (lane % HEAD_DIM) < (HEAD_DIM // 2)

    def norm_rope(x, cos, sin, gain):
        ms = _dot_split_lhs(x * x, head_avg)
        x = x * lax.rsqrt(ms + NORM_EPS) * gain
        rot = jnp.where(rope_lo, pltpu.roll(x, PAIR - HEAD_DIM // 2, 1), pltpu.roll(x, HEAD_DIM // 2, 1))
        return x * cos + rot * sin

    cos, sin = cos_ref[0], sin_ref[0]
    cos_k = jnp.concatenate([cosh_ref[0], cos], axis=0)
    sin_k = jnp.concatenate([sinh_ref[0], sin], axis=0)
    kv = jnp.concatenate([kvh_ref[0], kv_ref[0]], axis=0)

    qi = lax.broadcasted_iota(jnp.int32, (2 * w, 2 * w), 0) % w
    kj = lax.broadcasted_iota(jnp.int32, (2 * w, 2 * w), 1)
    rel = kj - qi
    band = (rel >= 1) & (rel <= w)
    band_first = band & ((kj >= w) | not_first)
    row_is_a = lax.broadcasted_iota(jnp.int32, (2 * w, 1), 0) < w

    for kb in range(ATTN_KV_DIM // PAIR):
        kn = norm_rope(kv[:, kb * PAIR:(kb + 1) * PAIR], cos_k, sin_k, kg_ref[...])
        vv = kv[:, ATTN_KV_DIM + kb * PAIR:ATTN_KV_DIM + (kb + 1) * PAIR]
        kn_sw = pltpu.roll(kn, HEAD_DIM, 1)
        vv_sw = pltpu.roll(vv, HEAD_DIM, 1)
        for half in range(2):
            g = 2 * kb + half
            if half == 0:
                k2 = jnp.where(lane_is_a, kn, kn_sw)
                v2 = jnp.where(lane_is_a, vv, vv_sw)
            else:
                k2 = jnp.where(lane_is_a, kn_sw, kn)
                v2 = jnp.where(lane_is_a, vv_sw, vv)
            k2 = k2.astype(BF16)
            v2 = v2.astype(BF16)
            for jp in range(2):
                qb = 2 * g + jp
                qn = norm_rope(q_ref[0, :, qb * PAIR:(qb + 1) * PAIR], cos, sin, qg_ref[...])
                qn = qn * (HEAD_DIM ** -0.5)
                sink = jnp.where(row_is_a, sink_ref[2 * qb], sink_ref[2 * qb + 1])
                for sb in range(tq // w):
                    qs = _stack_heads(qn[sb * w:(sb + 1) * w], lane_is_a).astype(BF16)
                    sc = _dot_nt(qs, k2[sb * w:(sb + 2) * w])
                    sc = jnp.where(band_first if sb == 0 else band, sc, -jnp.inf)
                    mx = jnp.maximum(jnp.max(sc, axis=-1, keepdims=True), sink)
                    e = jnp.exp(sc - mx)
                    inv_den = 1.0 / (jnp.sum(e, axis=-1, keepdims=True) + jnp.exp(sink - mx))
                    o2 = _dot((e * inv_den).astype(BF16), v2[sb * w:(sb + 2) * w])
                    o_ref[0, sb * w:(sb + 1) * w, qb * PAIR:(qb + 1) * PAIR] = (
                        jnp.where(lane_is_a, o2[0:w], o2[w:]).astype(o_ref.dtype))


def _attn(proj3, cos_tab, sin_tab, q_gain, k_gain, sinks, tq):
    b, s, _ = proj3.shape
    w = WINDOW
    kvw = 2 * ATTN_KV_DIM
    hb = tq // w
    cur = lambda width, col: pl.BlockSpec((1, tq, width), lambda bi, c: (bi, c, col))
    halo = lambda width, col: pl.BlockSpec((1, w, width), lambda bi, c: (bi, jnp.maximum(c * hb - 1, 0), col))
    vec = pl.BlockSpec((1, PAIR), lambda bi, c: (0, 0))
    return pl.pallas_call(
        _attn_kernel,
        grid=(b, s // tq),
        in_specs=[
            pl.BlockSpec(memory_space=pltpu.SMEM),
            cur(BRANCH_DIM, COL_Q // BRANCH_DIM), cur(kvw, COL_KV // kvw), halo(kvw, COL_KV // kvw),
            cur(PAIR, 0), cur(PAIR, 0), halo(PAIR, 0), halo(PAIR, 0), vec, vec,
        ],
        out_specs=pl.BlockSpec((1, tq, BRANCH_DIM), lambda bi, c: (bi, c, 0)),
        out_shape=jax.ShapeDtypeStruct((b, s, BRANCH_DIM), BF16),
        compiler_params=_cp("parallel", "arbitrary"),
        name="attn",
    )(sinks, proj3, proj3, proj3, cos_tab, sin_tab, cos_tab, sin_tab, q_gain, k_gain)


def _merge_kernel(a_ref, b_ref, c_ref, g0_ref, g1_ref, g2_ref, w_ref, o_ref):
    acc = _sigmoid(g0_ref[...]) * _dot(a_ref[...], w_ref[0])
    acc = acc + _sigmoid(g1_ref[...]) * _dot(b_ref[...], w_ref[1])
    acc = acc + _sigmoid(g2_ref[...]) * _dot(c_ref[...], w_ref[2])
    o_ref[...] = acc.astype(o_ref.dtype)


def _merge(a_out, b_out, c_out, proj, w_branch_bf16, tm, tn):
    n = proj.shape[0]
    br = lambda: pl.BlockSpec((tm, BRANCH_DIM), lambda i, j: (i, 0))
    gate = lambda g: pl.BlockSpec((tm, tn), lambda i, j: (i, (COL_G + g * D_MODEL) // tn + j))
    return pl.pallas_call(
        _merge_kernel,
        grid=(n // tm, D_MODEL // tn),
        in_specs=[br(), br(), br(), gate(0), gate(1), gate(2),
                  pl.BlockSpec((N_BRANCHES, BRANCH_DIM, tn), lambda i, j: (0, 0, j))],
        out_specs=pl.BlockSpec((tm, tn), lambda i, j: (i, j)),
        out_shape=jax.ShapeDtypeStruct((n, D_MODEL), BF16),
        compiler_params=_cp("parallel", "arbitrary"),
        name="merge",
    )(a_out, b_out, c_out, proj, proj, proj, w_branch_bf16)


def _out_proj_kernel(m_ref, x_ref, w_ref, g_ref, wr_ref, x1_ref, hn_ref, lg_ref):
    x1 = x_ref[...] + _dot(m_ref[...], w_ref[...])
    x1_ref[...] = x1
    ms = jnp.mean(x1 * x1, axis=-1, keepdims=True)
    hn = x1 * lax.rsqrt(ms + NORM_EPS) * g_ref[...]
    hn_ref[...] = hn
    hh, hl = _split_bf16(hn)
    wh, wl = _split_bf16(wr_ref[...])
    lg_ref[...] = _dot_nt(wh, hh) + _dot_nt(wh, hl) + _dot_nt(wl, hh)


def _out_proj(merged, x2d, w_out_bf16, ffn_gain, w_router_t, tm):
    n, d = x2d.shape
    return pl.pallas_call(
        _out_proj_kernel,
        grid=(n // tm,),
        in_specs=[
            pl.BlockSpec((tm, d), lambda i: (i, 0)),
            pl.BlockSpec((tm, d), lambda i: (i, 0)),
            pl.BlockSpec((d, d), lambda i: (0, 0)),
            pl.BlockSpec((1, d), lambda i: (0, 0)),
            pl.BlockSpec((ROUTER_ROWS, d), lambda i: (0, 0)),
        ],
        out_specs=[
            pl.BlockSpec((tm, d), lambda i: (i, 0)),
            pl.BlockSpec((tm, d), lambda i: (i, 0)),
            pl.BlockSpec((ROUTER_ROWS, tm), lambda i: (0, i)),
        ],
        out_shape=[
            jax.ShapeDtypeStruct((n, d), F32),
            jax.ShapeDtypeStruct((n, d), F32),
            jax.ShapeDtypeStruct((ROUTER_ROWS, n), F32),
        ],
        compiler_params=_cp("parallel"),
        name="out_proj",
    )(merged, x2d, w_out_bf16, ffn_gain.reshape(1, d), w_router_t)


def _experts_kernel(be_ref, nu_ref, x_ref, w1_ref, w3_ref, w2_ref, o_ref, w1b_ref, w3b_ref, w2b_ref):
    i = pl.program_id(0)
    new_expert = (i == 0) | (be_ref[i] != be_ref[jnp.maximum(i - 1, 0)])

    @pl.when(new_expert)
    def _():
        w1b_ref[...] = w1_ref[0, 0].astype(BF16)
        w3b_ref[...] = w3_ref[0, 0].astype(BF16)
        w2b_ref[...] = w2_ref[0, 0].astype(BF16)

    used = i < nu_ref[0]

    @pl.when(used)
    def _():
        xb = x_ref[...].astype(BF16)
        h1 = _dot(xb, w1b_ref[...])
        h3 = _dot(xb, w3b_ref[...])
        act = (h1 * _sigmoid(h1)) * h3
        o_ref[...] = _dot(act.astype(BF16), w2b_ref[...])

    @pl.when(jnp.logical_not(used))
    def _():
        o_ref[...] = jnp.zeros_like(o_ref)


def _experts(xg, block_expert, n_used, w1, w3, w2, layer):
    n_rows, d = xg.shape
    ff = w1.shape[-1]
    blk = MOE_BLOCK
    return pl.pallas_call(
        _experts_kernel,
        grid_spec=pltpu.PrefetchScalarGridSpec(
            num_scalar_prefetch=2,
            grid=(n_rows // blk,),
            in_specs=[
                pl.BlockSpec((blk, d), lambda i, be, nu: (jnp.minimum(i, nu[0] - 1), 0)),
                pl.BlockSpec((1, 1, d, ff), lambda i, be, nu: (layer, be[i], 0, 0)),
                pl.BlockSpec((1, 1, d, ff), lambda i, be, nu: (layer, be[i], 0, 0)),
                pl.BlockSpec((1, 1, ff, d), lambda i, be, nu: (layer, be[i], 0, 0)),
            ],
            out_specs=pl.BlockSpec((blk, d), lambda i, be, nu: (i, 0)),
            scratch_shapes=[pltpu.VMEM((d, ff), BF16), pltpu.VMEM((d, ff), BF16), pltpu.VMEM((ff, d), BF16)],
        ),
        out_shape=jax.ShapeDtypeStruct((n_rows, d), F32),
        compiler_params=_cp("arbitrary"),
        name="experts",
    )(block_expert, n_used, xg, w1, w3, w2)


def _pack_segments():
    c = BRANCH_DIM
    o_ad, o_gd = 3 * c + DECAY_LORA, 3 * c + DECAY_LORA + ICLR_LORA
    o_u = o_gd + GATE_LORA
    o_q = o_u + c
    o_k = o_q + c
    o_g = o_k + 2 * ATTN_KV_DIM
    segs = [(0, 0, LANES),
            (3 * c, 3 * c, DECAY_LORA),
            (3 * c + LORA_PAD, o_ad, ICLR_LORA),
            (3 * c + 2 * LORA_PAD, o_gd, LANES),
            (COL_U, o_u, LANES), (COL_KV, o_k, LANES), (COL_Q, o_q, LANES), (COL_G, o_g, LANES)]
    return [(p // LANES, src, keep) for p, src, keep in segs]


def _pack_w_in_kernel(a_ref, b_ref, o_ref):
    blk = pl.program_id(0)
    lane = lax.broadcasted_iota(jnp.int32, (1, LANES), 1)
    segs = _pack_segments()
    for si, (first, src, keep) in enumerate(segs):
        last = segs[si + 1][0] if si + 1 < len(segs) else P_COLS // LANES
        shift = (-src) % LANES

        @pl.when((blk >= first) & (blk < last))
        def _(shift=shift, keep=keep):
            if shift == 0:
                x = a_ref[0]
            else:
                x = jnp.where(lane < shift, pltpu.roll(a_ref[0], shift, 1), pltpu.roll(b_ref[0], shift, 1))
            if keep < LANES:
                x = jnp.where(lane < keep, x, 0.0)
            o_ref[...] = x.astype(o_ref.dtype)


def _pack_w_in(w, layer):
    _, d, cols = w.shape
    segs = _pack_segments()
    last_src = (cols - 1) // LANES

    def src_block(blk):
        q = blk
        for first, src, _ in segs:
            q = jnp.where(blk >= first, blk - first + src // LANES, q)
        return q

    return pl.pallas_call(
        _pack_w_in_kernel,
        grid=(P_COLS // LANES,),
        in_specs=[pl.BlockSpec((1, d, LANES), lambda i: (layer, 0, src_block(i))),
                  pl.BlockSpec((1, d, LANES), lambda i: (layer, 0, jnp.minimum(src_block(i) + 1, last_src)))],
        out_specs=pl.BlockSpec((d, LANES), lambda i: (0, i)),
        out_shape=jax.ShapeDtypeStruct((d, P_COLS), BF16),
        compiler_params=_cp("parallel"),
        name="pack_w_in",
    )(w, w)


def _pack_mu(mu):
    z32 = jnp.zeros((LORA_PAD - DECAY_LORA,), mu.dtype)
    c = BRANCH_DIM
    o_wd, o_ad, o_gd = 3 * c, 3 * c + DECAY_LORA, 3 * c + DECAY_LORA + ICLR_LORA
    return jnp.concatenate([mu[:o_wd], mu[o_wd:o_ad], z32, mu[o_ad:o_gd], z32, mu[o_gd:]]).reshape(1, Z_COLS)


def _pad_rows(w, rows):
    return jnp.concatenate([w, jnp.zeros((rows - w.shape[0], w.shape[1]), w.dtype)], axis=0)


def _router_weights(w_grp, w_exp):
    d = w_grp.shape[0]
    pad = jnp.zeros((d, 8 - N_GROUPS), w_grp.dtype)
    return jnp.concatenate([w_grp, pad, w_exp], axis=1).T


def _route_kernel(lg_ref, bias_ref, ids_ref, gcol_ref, cnt_ref, carry_ref):
    tm = lg_ref.shape[1]

    @pl.when(pl.program_id(0) == 0)
    def _():
        carry_ref[...] = jnp.zeros_like(carry_ref)

    lg = lg_ref[...] + bias_ref[...]
    row8 = lax.broadcasted_iota(jnp.int32, (8, tm), 0)
    row8f = row8.astype(F32)
    neg_inf = -jnp.inf

    def first_argmax(x):
        mx = jnp.max(x, axis=0, keepdims=True)
        return mx, jnp.min(jnp.where(x == mx, row8f, 8.0), axis=0, keepdims=True).astype(jnp.int32)

    grp_logits = jnp.where(row8 < N_GROUPS, lg[0:8], neg_inf)
    gmax, grp = first_argmax(grp_logits)
    p_grp = 1.0 / jnp.sum(jnp.exp(grp_logits - gmax), axis=0, keepdims=True)
    in_grp = lg[8:16]
    for g in range(1, N_GROUPS):
        in_grp = jnp.where(grp == g, lg[8 + 8 * g:16 + 8 * g], in_grp)
    m1, i1 = first_argmax(in_grp)
    rest = jnp.where(row8 == i1, neg_inf, in_grp)
    m2, i2 = first_argmax(rest)
    e2 = jnp.exp(m2 - m1)
    gate1 = p_grp / (1.0 + e2)
    gate2 = p_grp * e2 / (1.0 + e2)
    exp1 = grp * EXPERTS_PER_GROUP + i1
    exp2 = grp * EXPERTS_PER_GROUP + i2

    rows = lax.broadcasted_iota(jnp.int32, (N_EXPERTS, tm), 0)
    hot1 = (rows == exp1).astype(F32)
    hot2 = (rows == exp2).astype(F32)
    cnt = (hot1 + hot2).astype(BF16)
    src = lax.broadcasted_iota(jnp.int32, (tm, tm), 0)
    dst = lax.broadcasted_iota(jnp.int32, (tm, tm), 1)
    before = _dot(cnt, (src < dst).astype(BF16)) + carry_ref[...]
    rank1 = jnp.sum(hot1 * before, axis=0, keepdims=True).astype(jnp.int32)
    rank2 = jnp.sum(hot2 * before, axis=0, keepdims=True).astype(jnp.int32)
    carry_ref[...] += _dot(cnt, jnp.ones((tm, tm), BF16))
    cnt_ref[...] = carry_ref[...]

    ids_ref[...] = jnp.where(row8 == 0, exp1, jnp.where(row8 == 1, exp2,
                             jnp.where(row8 == 2, rank1, jnp.where(row8 == 3, rank2, 0))))
    row128 = lax.broadcasted_iota(jnp.int32, (LANES, tm), 0)
    gates_t = jnp.where(row128 == 0, gate1, jnp.where(row128 == 1, gate2, 0.0))
    gcol_ref[...] = gates_t.T


def _route(logits_t, bias_col, tm):
    n = logits_t.shape[1]
    bias = jnp.broadcast_to(bias_col, (ROUTER_ROWS, tm))
    return pl.pallas_call(
        _route_kernel,
        grid=(n // tm,),
        in_specs=[pl.BlockSpec((ROUTER_ROWS, tm), lambda i: (0, i)),
                  pl.BlockSpec((ROUTER_ROWS, tm), lambda i: (0, 0))],
        out_specs=[pl.BlockSpec((8, tm), lambda i: (0, i)),
                   pl.BlockSpec((tm, LANES), lambda i: (i, 0)),
                   pl.BlockSpec((N_EXPERTS, tm), lambda i: (0, 0))],
        out_shape=[jax.ShapeDtypeStruct((8, n), jnp.int32),
                   jax.ShapeDtypeStruct((n, LANES), F32),
                   jax.ShapeDtypeStruct((N_EXPERTS, tm), F32)],
        scratch_shapes=[pltpu.VMEM((N_EXPERTS, tm), F32)],
        compiler_params=_cp("arbitrary"),
        name="route",
    )(logits_t, bias)


def _row_copy(src_ref, src_row, dst_ref, dst_row, sem):
    return pltpu.make_async_copy(src_ref.at[pl.ds(src_row, 1)], dst_ref.at[pl.ds(dst_row, 1)], sem)


def _scatter_kernel(dest_ref, pad_start_ref, pad_len_ref, hn_ref, xg_ref, zero_ref, tile_ref, tile_sems, sems,
                    pad_sem, *, tm):
    n = hn_ref.shape[0]
    n_steps = pl.num_programs(0)
    step = pl.program_id(0)
    slot = step % 2
    base = step * tm

    def tile_load(for_step, into):
        return pltpu.make_async_copy(hn_ref.at[pl.ds(for_step * tm, tm)], tile_ref.at[into], tile_sems.at[into])

    @pl.when(step == 0)
    def _():
        tile_load(step, slot).start()

    @pl.when(pl.program_id(0) == 0)
    def _():
        zero_ref[...] = jnp.zeros_like(zero_ref)

        def each_pad_row(fn):
            def per_expert(e, carry):
                lax.fori_loop(0, pad_len_ref[e], lambda j, c: fn(pad_start_ref[e] + j, c), 0)
                return carry
            lax.fori_loop(0, pad_len_ref.shape[0], per_expert, 0)

        def start(row, carry):
            _row_copy(zero_ref, 0, xg_ref, row, pad_sem).start()
            return carry

        def wait(row, carry):
            _row_copy(zero_ref, 0, xg_ref, row, pad_sem).wait()
            return carry

        each_pad_row(start)
        each_pad_row(wait)

    def wait_rows(sl):
        for _ in range(TOP_K):
            pltpu.make_async_copy(tile_ref.at[sl], xg_ref.at[pl.ds(0, tm)], sems.at[sl]).wait()

    tile_load(step, slot).wait()

    @pl.when(step > 0)
    def _():
        wait_rows(1 - slot)

    @pl.when(step + 1 < n_steps)
    def _():
        tile_load(step + 1, 1 - slot).start()

    def issue(r, carry):
        for k in range(TOP_K):
            _row_copy(tile_ref.at[slot], r, xg_ref, dest_ref[k * n + base + r], sems.at[slot]).start()
        return carry

    lax.fori_loop(0, tm, issue, 0, unroll=ISSUE_UNROLL)

    @pl.when(step == n_steps - 1)
    def _():
        wait_rows(slot)


def _scatter(dest_flat, pad_start, pad_len, hn, n_rows, tm):
    n, d = hn.shape
    return pl.pallas_call(
        functools.partial(_scatter_kernel, tm=tm),
        grid_spec=pltpu.PrefetchScalarGridSpec(
            num_scalar_prefetch=3,
            grid=(n // tm,),
            in_specs=[pl.BlockSpec(memory_space=pl.ANY)],
            out_specs=pl.BlockSpec(memory_space=pl.ANY),
            scratch_shapes=[pltpu.VMEM((8, d), hn.dtype), pltpu.VMEM((2, tm, d), hn.dtype),
                            pltpu.SemaphoreType.DMA((2,)), pltpu.SemaphoreType.DMA((2,)), pltpu.SemaphoreType.DMA],
        ),
        out_shape=jax.ShapeDtypeStruct((n_rows, d), hn.dtype),
        compiler_params=_cp("arbitrary"),
        name="moe_scatter",
    )(dest_flat, pad_start, pad_len, hn)


def _combine_kernel(dest_ref, x_ref, gcol_ref, yg_ref, o_ref, buf_ref, sems):
    tm = x_ref.shape[0]
    n_steps = pl.num_programs(0)
    n = n_steps * tm
    step = pl.program_id(0)
    slot = step % 2

    def gather(for_step, into):
        base = for_step * tm

        def issue(r, carry):
            for k in range(TOP_K):
                _row_copy(yg_ref, dest_ref[k * n + base + r], buf_ref.at[into, k], r, sems.at[into]).start()
            return carry

        lax.fori_loop(0, tm, issue, 0, unroll=ISSUE_UNROLL)

    @pl.when(step == 0)
    def _():
        gather(step, slot)

    @pl.when(step + 1 < n_steps)
    def _():
        gather(step + 1, 1 - slot)

    for k in range(TOP_K):
        pltpu.make_async_copy(yg_ref.at[pl.ds(0, tm)], buf_ref.at[slot, k], sems.at[slot]).wait()
    g = gcol_ref[...]
    o_ref[...] = x_ref[...] + g[:, 0:1] * buf_ref[slot, 0] + g[:, 1:2] * buf_ref[slot, 1]


def _combine(dest_flat, x1, gcol, yg, tm):
    n, d = x1.shape
    return pl.pallas_call(
        _combine_kernel,
        grid_spec=pltpu.PrefetchScalarGridSpec(
            num_scalar_prefetch=1,
            grid=(n // tm,),
            in_specs=[pl.BlockSpec((tm, d), lambda i, dest: (i, 0)),
                      pl.BlockSpec((tm, LANES), lambda i, dest: (i, 0)),
                      pl.BlockSpec(memory_space=pl.ANY)],
            out_specs=pl.BlockSpec((tm, d), lambda i, dest: (i, 0)),
            scratch_shapes=[pltpu.VMEM((2, TOP_K, tm, d), F32), pltpu.SemaphoreType.DMA((2,))],
        ),
        out_shape=jax.ShapeDtypeStruct((n, d), F32),
        compiler_params=_cp("arbitrary"),
        name="moe_combine",
    )(dest_flat, x1, gcol, yg)


def _moe(x1, hn, logits_t, bias_col, w1, w3, w2, layer):
    n, d = hn.shape
    blk = MOE_BLOCK
    tm = _pick(n, 512)
    ids, gcol, cnt = _route(logits_t, bias_col, tm)
    counts = cnt[:, 0].astype(jnp.int32)
    padded = (counts + blk - 1) // blk * blk
    pends = jnp.cumsum(padded)
    pstarts = pends - padded
    experts = jnp.arange(N_EXPERTS, dtype=jnp.int32)
    seg_start = jnp.sum(jnp.where(ids[0:TOP_K, :, None] == experts, pstarts, 0), axis=-1)
    dest_flat = (seg_start + ids[TOP_K:2 * TOP_K]).reshape(-1)
    n_rows = n * TOP_K + N_EXPERTS * blk
    block_start = jnp.arange(n_rows // blk, dtype=jnp.int32) * blk
    block_expert = jnp.minimum(jnp.sum((block_start[:, None] >= pends[None, :]).astype(jnp.int32), axis=1),
                               N_EXPERTS - 1)
    n_used = (pends[N_EXPERTS - 1:] // blk).astype(jnp.int32)
    pad_start = jnp.concatenate([pstarts + counts, pends[N_EXPERTS - 1:]])
    pad_len = jnp.concatenate([padded - counts, n_rows - pends[N_EXPERTS - 1:]])
    xg = _scatter(dest_flat, pad_start, pad_len, hn, n_rows, tm)
    yg = _experts(xg, block_expert, n_used, w1, w3, w2, layer)
    return _combine(dest_flat, x1, gcol, yg, tm)


def _pick(n, pref):
    t = pref
    while n % t:
        t //= 2
    return t


def _layer(x2d, b, s, cos_tab, sin_tab, p, rwkv_precision):
    n = x2d.shape[0]
    proj = _norm_matmul(x2d, p['attn_norm'], p['w_in'], _pick(n, 1024), 1024)
    proj3 = proj.reshape(b, s, P_COLS)
    a_out = _rwkv(proj3, p['mu'], p['w0'], p['a0'], p['k_k'], p['k_a'], p['r_k'], p['ln_g'], p['ln_b'],
                  p['w_up'], p['a_up'], p['g_up'], _pick(s, 512), rwkv_precision)
    b_out = _pool(proj3, p['pool_w'], p['pool_scale'], _pick(s, 512))
    c_out = _attn(proj3, cos_tab, sin_tab, p['q_gain'], p['k_gain'], p['sinks'], _pick(s, 256))
    merged = _merge(a_out.reshape(n, -1), b_out.reshape(n, -1), c_out.reshape(n, -1), proj,
                    p['w_branch'], _pick(n, 1024), 512)
    x1, hn, logits_t = _out_proj(merged, x2d, p['w_out'], p['ffn_norm'], p['w_router_t'], _pick(n, 512))
    return _moe(x1, hn, logits_t, p['router_bias'], p['w1'], p['w3'], p['w2'], p['layer'])


def kernel(x, positions, attn_norm, w_in, tmix_mu, rwkv_w0, rwkv_w_up, rwkv_a0, rwkv_a_up, rwkv_g_up, rwkv_k_k, rwkv_k_a, rwkv_r_k, rwkv_ln_g, rwkv_ln_b, pool_w, pool_scale, q_norm, k_norm, attn_sinks, w_branch, w_out, ffn_norm, router_grp_w, router_grp_b, router_exp_w, router_exp_b, expert_w1, expert_w3, expert_w2):
    b, s, d = x.shape
    n = b * s
    half = HEAD_DIM // 2
    inv_freq = ROPE_THETA ** (-jnp.arange(half, dtype=F32) / half)
    inv_freq = jnp.tile(inv_freq, PAIR // half).reshape(1, PAIR)
    pos_lanes = jnp.broadcast_to(positions.astype(F32).reshape(n, 1), (n, PAIR))
    cos_tab, sin_tab = _rope_tables(pos_lanes, inv_freq, _pick(n, 2048))
    cos_tab = cos_tab.reshape(b, s, PAIR)
    sin_tab = sin_tab.reshape(b, s, PAIR)
    x2d = x.reshape(n, d)
    for l in range(w_in.shape[0]):
        p = {
            'attn_norm': attn_norm[l],
            'layer': l,
            'w_in': _pack_w_in(w_in, l),
            'mu': _pack_mu(tmix_mu[l]),
            'w0': rwkv_w0[l], 'a0': rwkv_a0[l], 'k_k': rwkv_k_k[l], 'k_a': rwkv_k_a[l],
            'r_k': rwkv_r_k[l].reshape(-1), 'ln_g': rwkv_ln_g[l], 'ln_b': rwkv_ln_b[l],
            'w_up': _pad_rows(rwkv_w_up[l], LORA_PAD), 'a_up': _pad_rows(rwkv_a_up[l], LORA_PAD),
            'g_up': rwkv_g_up[l],
            'pool_w': pool_w[l].astype(BF16), 'pool_scale': pool_scale[l],
            'q_gain': jnp.tile(q_norm[l], PAIR // HEAD_DIM).reshape(1, PAIR),
            'k_gain': jnp.tile(k_norm[l], PAIR // HEAD_DIM).reshape(1, PAIR),
            'sinks': attn_sinks[l],
            'w_branch': w_branch[l].astype(BF16), 'w_out': w_out[l].astype(BF16),
            'ffn_norm': ffn_norm[l],
            'w_router_t': _router_weights(router_grp_w[l], router_exp_w[l]),
            'router_bias': jnp.concatenate([router_grp_b[l], jnp.zeros((8 - N_GROUPS,), F32),
                                            router_exp_b[l]]).reshape(ROUTER_ROWS, 1),
            'w1': expert_w1, 'w3': expert_w3, 'w2': expert_w2,
        }
        x2d = _layer(x2d, b, s, cos_tab, sin_tab, p, None)
    return x2d.reshape(b, s, d)
```

```python
import functools

import jax
import jax.numpy as jnp
from jax import lax
from jax.experimental import pallas as pl
from jax.experimental.pallas import tpu as pltpu

F32 = jnp.float32
BF16 = jnp.bfloat16

D_MODEL = 2048
HEAD_DIM = 64
BRANCH_DIM = D_MODEL // 2
DECAY_LORA = 96
ICLR_LORA = 96
GATE_LORA = 256
LORA_PAD = 128
POOL_WINDOWS = (2, 4, 8, 16)
POOL_GROUP_DIM = BRANCH_DIM // len(POOL_WINDOWS)
POOL_HALO = 16
ATTN_Q_HEADS = BRANCH_DIM // HEAD_DIM
ATTN_KV_HEADS = 4
ATTN_KV_DIM = ATTN_KV_HEADS * HEAD_DIM
WINDOW = 128
ROPE_THETA = 10000.0
N_BRANCHES = 3
N_GROUPS = 4
EXPERTS_PER_GROUP = 8
N_EXPERTS = N_GROUPS * EXPERTS_PER_GROUP
TOP_K = 2
EXPERT_FF = 512
NORM_EPS = 1e-6
GN_EPS = 64e-5

LANES = 128
PAIR = 2 * HEAD_DIM
CHUNK = 64
RWKV_PAIRS_PER_STEP = 4
ROUTER_ROWS = 40
MOE_BLOCK = 256
ISSUE_UNROLL = 8

Z_COLS = 3 * BRANCH_DIM + 2 * LORA_PAD + GATE_LORA
COL_U = Z_COLS
COL_KV = COL_U + BRANCH_DIM
COL_Q = COL_KV + 2 * ATTN_KV_DIM
COL_G = COL_Q + BRANCH_DIM
P_COLS = COL_G + N_BRANCHES * D_MODEL

VMEM_LIMIT = 56 * 1024 * 1024


def _cp(*sem):
    return pltpu.CompilerParams(dimension_semantics=sem, vmem_limit_bytes=VMEM_LIMIT)


def _dot(a, b):
    return jnp.dot(a, b, preferred_element_type=F32)


def _dot_nt(a, b):
    return lax.dot_general(a, b, (((1,), (1,)), ((), ())), preferred_element_type=F32)


def _sigmoid(x):
    return 1.0 / (1.0 + jnp.exp(-x))


def _split_bf16(x):
    hi = x.astype(BF16)
    return hi, (x - hi.astype(F32)).astype(BF16)


def _dot_split_lhs(x, m):
    m = m.astype(BF16)
    return _dot(jnp.concatenate(_split_bf16(x), axis=1), jnp.concatenate([m, m], axis=0))


def _dot_split3(x, w):
    xh, xl = _split_bf16(x)
    wh, wl = _split_bf16(w)
    return _dot(jnp.concatenate([xh, xl, xh], axis=1), jnp.concatenate([wh, wh, wl], axis=0))


def _head_block_diag(scale):
    r = lax.broadcasted_iota(jnp.int32, (PAIR, PAIR), 0) // HEAD_DIM
    c = lax.broadcasted_iota(jnp.int32, (PAIR, PAIR), 1) // HEAD_DIM
    return jnp.where(r == c, scale, 0.0).astype(F32)


def _norm_matmul_kernel(x_ref, g_ref, w_ref, o_ref, h_ref):
    @pl.when(pl.program_id(1) == 0)
    def _():
        x = x_ref[...]
        ms = jnp.mean(x * x, axis=-1, keepdims=True)
        h_ref[...] = (x * lax.rsqrt(ms + NORM_EPS) * g_ref[...]).astype(BF16)

    o_ref[...] = _dot(h_ref[...], w_ref[...])


def _norm_matmul(x2d, gain, w_bf16, tm, tn):
    n, d = x2d.shape
    cols = w_bf16.shape[1]
    return pl.pallas_call(
        _norm_matmul_kernel,
        grid=(n // tm, cols // tn),
        in_specs=[
            pl.BlockSpec((tm, d), lambda i, j: (i, 0)),
            pl.BlockSpec((1, d), lambda i, j: (0, 0)),
            pl.BlockSpec((d, tn), lambda i, j: (0, j)),
        ],
        out_specs=pl.BlockSpec((tm, tn), lambda i, j: (i, j)),
        out_shape=jax.ShapeDtypeStruct((n, cols), F32),
        scratch_shapes=[pltpu.VMEM((tm, d), BF16)],
        compiler_params=_cp("arbitrary", "arbitrary"),
        name="norm_matmul",
    )(x2d, gain.reshape(1, d), w_bf16)


def _stack_heads(x, lane_is_a):
    return jnp.concatenate([jnp.where(lane_is_a, x, 0.0), jnp.where(lane_is_a, 0.0, x)], axis=0)


def _rwkv_kernel(zr_ref, zk_ref, zv_ref, zl_ref, mur_ref, muk_ref, muv_ref, mul_ref,
                 w0_ref, a0_ref, kkg_ref, ka_ref, rk_ref, lng_ref, lnb_ref,
                 wup_ref, aup_ref, gup_ref, o_ref,
                 h_ref, cr_ref, ck_ref, cv_ref, cl_ref):
    t_rows = zr_ref.shape[1]

    @pl.when(pl.program_id(2) == 0)
    def _():
        h_ref[...] = jnp.zeros_like(h_ref)
        cr_ref[...] = jnp.zeros_like(cr_ref)
        ck_ref[...] = jnp.zeros_like(ck_ref)
        cv_ref[...] = jnp.zeros_like(cv_ref)
        cl_ref[...] = jnp.zeros_like(cl_ref)

    row = lax.broadcasted_iota(jnp.int32, (t_rows, 1), 0)

    def shifted(z_ref, carry_ref, mu_ref):
        z = z_ref[0]
        prev = jnp.where(row == 0, carry_ref[...], pltpu.roll(z, 1, 0))
        carry_ref[...] = z_ref[0, t_rows - 1:t_rows, :]
        return z + (prev - z) * mu_ref[...]

    r = shifted(zr_ref, cr_ref, mur_ref)
    k = shifted(zk_ref, ck_ref, muk_ref)
    v = shifted(zv_ref, cv_ref, muv_ref)
    zl = shifted(zl_ref, cl_ref, mul_ref)
    wd = zl[:, 0:LORA_PAD]
    ad = zl[:, LORA_PAD:2 * LORA_PAD]
    gd = zl[:, 2 * LORA_PAD:]

    head_sum = _head_block_diag(1.0)
    head_avg = _head_block_diag(1.0 / HEAD_DIM)
    n_sub = zr_ref.shape[2] // PAIR
    lanes = [slice(j * PAIR, (j + 1) * PAIR) for j in range(n_sub)]

    def per_head(x, m):
        return jnp.concatenate([_dot_split_lhs(x[:, ln], m) for ln in lanes], axis=1)

    w_pre = w0_ref[...] + _dot_split3(jnp.tanh(wd), wup_ref[...])
    neg = -w_pre
    softplus = jnp.maximum(neg, 0.0) + jnp.log(1.0 + jnp.exp(-jnp.abs(neg)))
    log_decay = -jnp.exp(-softplus - 0.5)
    a = _sigmoid(a0_ref[...] + _dot_split3(ad, aup_ref[...]))
    gate = _dot_split3(_sigmoid(gd), gup_ref[...])
    kk = k * kkg_ref[...]
    kk = kk * lax.rsqrt(jnp.maximum(per_head(kk * kk, head_sum), 1e-24))
    k = k * (1.0 + (a - 1.0) * ka_ref[...])
    bonus = per_head(r * k * rk_ref[...], head_sum) * v

    c = CHUNK
    ri = lax.broadcasted_iota(jnp.int32, (c, c), 0)
    ci = lax.broadcasted_iota(jnp.int32, (c, c), 1)
    tri_incl = (ci <= ri).astype(F32)
    tri2 = jnp.concatenate([tri_incl, tri_incl], axis=1).astype(BF16)
    lane_is_a = lax.broadcasted_iota(jnp.int32, (1, PAIR), 1) < HEAD_DIM
    rs = lax.broadcasted_iota(jnp.int32, (PAIR, PAIR), 0)
    cs = lax.broadcasted_iota(jnp.int32, (PAIR, PAIR), 1)
    strict = (cs % c) < (rs % c)
    incl = (cs % c) <= (rs % c)
    eye = rs == cs
    eye_f = eye.astype(F32)

    def dot(x, y):
        return _dot(x.astype(BF16), y.astype(BF16))

    def dot_nt(x, y):
        return _dot_nt(x.astype(BF16), y.astype(BF16))

    n_ch = t_rows // c
    sls = [slice(ch * c, (ch + 1) * c) for ch in range(n_ch)]
    cums = [_dot(tri2, jnp.concatenate(_split_bf16(log_decay[sl]), axis=0)) for sl in sls]
    pre = []
    for sl, cum_all in zip(sls, cums):
        for ln in lanes:
            cum = cum_all[:, ln]
            lw = log_decay[sl, ln]
            tot = cum[c - 1:c, :]
            p_in = jnp.exp(cum)
            p_ex = jnp.exp(cum - lw)
            q_inv = jnp.exp(-cum)
            q_end = jnp.exp(tot - cum)
            kk_c, a_c, k_c = kk[sl, ln], a[sl, ln], k[sl, ln]
            beta = kk_c * a_c
            pre.append(dict(
                tot=tot,
                at_s=_stack_heads(-kk_c * p_ex, lane_is_a),
                rt_s=_stack_heads(r[sl, ln] * p_in, lane_is_a),
                bh_s=_stack_heads(beta * q_inv, lane_is_a),
                kh_s=_stack_heads(k_c * q_inv, lane_is_a),
                be_s=_stack_heads(beta * q_end, lane_is_a),
                ke_s=_stack_heads(k_c * q_end, lane_is_a),
                v_s=_stack_heads(v[sl, ln], lane_is_a)))
    gms = [dot_nt(jnp.concatenate([d['at_s'], d['rt_s']], axis=0),
                  jnp.concatenate([d['bh_s'], d['kh_s']], axis=0)) for d in pre]
    a_ab = [jnp.where(strict, g[0:PAIR, 0:PAIR], 0.0) for g in gms]
    a_ak = [jnp.where(strict, g[0:PAIR, PAIR:], 0.0) for g in gms]
    a_rb = [jnp.where(incl, g[PAIR:, 0:PAIR], 0.0) for g in gms]
    a_rk = [jnp.where(incl, g[PAIR:, PAIR:], 0.0) for g in gms]
    pw = [dot(x, x) for x in a_ab]
    xv = [dot(jnp.concatenate([ak, rk, d['ke_s'].T], axis=0), d['v_s'])
          for ak, rk, d in zip(a_ak, a_rk, pre)]
    inv = [eye_f + x for x in a_ab]
    for _ in range(4):
        both = [dot(jnp.concatenate([p, x], axis=0), p) for p, x in zip(pw, inv)]
        pw = [t[0:PAIR] for t in both]
        inv = [x + t[PAIR:] for x, t in zip(inv, both)]
    inv = [x + dot(x, p) for x, p in zip(inv, pw)]
    wu = [dot(x, jnp.concatenate([d['at_s'], t[0:PAIR]], axis=1)) for x, d, t in zip(inv, pre, xv)]
    xw = [dot(jnp.concatenate([rb, d['be_s'].T], axis=0), y) for rb, d, y in zip(a_rb, pre, wu)]
    chunks = []
    for i in range(len(pre)):
        rt2 = pre[i]['rt_s'] + xw[i][0:PAIR, 0:PAIR]
        y0 = xw[i][0:PAIR, PAIR:] + xv[i][PAIR:2 * PAIR]
        m = jnp.where(eye, jnp.exp(pre[i]['tot']), 0.0) + xw[i][PAIR:, 0:PAIR]
        h0 = xw[i][PAIR:, PAIR:] + xv[i][2 * PAIR:]
        chunks.append((jnp.concatenate([rt2, m], axis=0), y0, h0))

    hs = [h_ref[j] for j in range(n_sub)]
    ys = [[] for _ in range(n_sub)]
    for i, (rm, y0, h0) in enumerate(chunks):
        j = i % n_sub
        t = dot(rm, hs[j])
        y_s = t[0:PAIR] + y0
        hs[j] = t[PAIR:] + h0
        ys[j].append(y_s[0:c] + y_s[c:])
    for j in range(n_sub):
        h_ref[j] = hs[j]

    y = jnp.concatenate([jnp.concatenate(yj, axis=0) for yj in ys], axis=1)
    mean = per_head(y, head_avg)
    yc = y - mean
    var = per_head(yc * yc, head_avg)
    gn = yc * lax.rsqrt(var + GN_EPS)
    o_ref[0] = ((gn * lng_ref[...] + lnb_ref[...] + bonus) * gate).astype(o_ref.dtype)


def _rwkv(proj3, mu_pack, w0, a0, k_k, k_a, r_k, ln_g, ln_b, w_up, a_up, g_up, t_rows):
    b, s, _ = proj3.shape
    wide = RWKV_PAIRS_PER_STEP * PAIR
    n_blk = BRANCH_DIM // wide
    vec = lambda off: pl.BlockSpec((1, wide), lambda bi, p, c: (0, off + p))
    zcol = lambda off: pl.BlockSpec((1, t_rows, wide), lambda bi, p, c: (bi, c, off + p))
    lora_w = 2 * LORA_PAD + GATE_LORA
    lora_blk = (3 * BRANCH_DIM) // lora_w
    row = lambda x: x.reshape(1, BRANCH_DIM)
    return pl.pallas_call(
        _rwkv_kernel,
        grid=(b, n_blk, s // t_rows),
        in_specs=[
            zcol(0), zcol(n_blk), zcol(2 * n_blk),
            pl.BlockSpec((1, t_rows, lora_w), lambda bi, p, c: (bi, c, lora_blk)),
            vec(0), vec(n_blk), vec(2 * n_blk),
            pl.BlockSpec((1, lora_w), lambda bi, p, c: (0, lora_blk)),
            vec(0), vec(0), vec(0), vec(0), vec(0), vec(0), vec(0),
            pl.BlockSpec((LORA_PAD, wide), lambda bi, p, c: (0, p)),
            pl.BlockSpec((LORA_PAD, wide), lambda bi, p, c: (0, p)),
            pl.BlockSpec((GATE_LORA, wide), lambda bi, p, c: (0, p)),
        ],
        out_specs=pl.BlockSpec((1, t_rows, wide), lambda bi, p, c: (bi, c, p)),
        out_shape=jax.ShapeDtypeStruct((b, s, BRANCH_DIM), BF16),
        scratch_shapes=[
            pltpu.VMEM((RWKV_PAIRS_PER_STEP, PAIR, PAIR), F32),
            pltpu.VMEM((1, wide), F32), pltpu.VMEM((1, wide), F32), pltpu.VMEM((1, wide), F32),
            pltpu.VMEM((1, lora_w), F32),
        ],
        compiler_params=_cp("arbitrary", "arbitrary", "arbitrary"),
        name="rwkv",
    )(proj3, proj3, proj3, proj3, mu_pack, mu_pack, mu_pack, mu_pack,
      row(w0), row(a0), row(k_k), row(k_a), row(r_k), row(ln_g), row(ln_b), w_up, a_up, g_up)


def _pool_kernel(u0_ref, u1_ref, u2_ref, u3_ref, h0_ref, h1_ref, h2_ref, h3_ref, w_ref, sc_ref, o_ref, buf_ref):
    t_rows = u0_ref.shape[1]
    first = pl.program_id(1) == 0
    t = pl.program_id(1) * t_rows + lax.broadcasted_iota(jnp.int32, (t_rows, 1), 0)
    gd = POOL_GROUP_DIM
    for gi, (m, u_ref, halo_ref) in enumerate(zip(POOL_WINDOWS, (u0_ref, u1_ref, u2_ref, u3_ref),
                                                  (h0_ref, h1_ref, h2_ref, h3_ref))):
        u = u_ref[0]
        buf_ref[0:POOL_HALO, :] = jnp.where(first, 0.0, halo_ref[0])
        buf_ref[POOL_HALO:, :] = u
        acc = u
        for sft in range(1, m):
            acc = acc + buf_ref[pl.ds(POOL_HALO - sft, t_rows), :]
        count = jnp.minimum(t + 1, m).astype(F32)
        pooled = acc / count - u
        y = _dot(pooled.astype(BF16), w_ref[gi])
        o_ref[0, :, gi * gd:(gi + 1) * gd] = (y * sc_ref[:, gi * gd:(gi + 1) * gd]).astype(o_ref.dtype)


def _pool(proj3, pool_w_bf16, pool_scale, t_rows):
    b, s, _ = proj3.shape
    gd = POOL_GROUP_DIM
    ublk = COL_U // gd
    hb = t_rows // POOL_HALO
    u_spec = lambda gi: pl.BlockSpec((1, t_rows, gd), lambda bi, c: (bi, c, ublk + gi))
    h_spec = lambda gi: pl.BlockSpec((1, POOL_HALO, gd), lambda bi, c: (bi, jnp.maximum(c * hb - 1, 0), ublk + gi))
    return pl.pallas_call(
        _pool_kernel,
        grid=(b, s // t_rows),
        in_specs=[u_spec(0), u_spec(1), u_spec(2), u_spec(3), h_spec(0), h_spec(1), h_spec(2), h_spec(3),
                  pl.BlockSpec((len(POOL_WINDOWS), gd, gd), lambda bi, c: (0, 0, 0)),
                  pl.BlockSpec((1, BRANCH_DIM), lambda bi, c: (0, 0))],
        out_specs=pl.BlockSpec((1, t_rows, BRANCH_DIM), lambda bi, c: (bi, c, 0)),
        out_shape=jax.ShapeDtypeStruct((b, s, BRANCH_DIM), BF16),
        scratch_shapes=[pltpu.VMEM((t_rows + POOL_HALO, gd), F32)],
        compiler_params=_cp("parallel", "arbitrary"),
        name="pool",
    )(proj3, proj3, proj3, proj3, proj3, proj3, proj3, proj3, pool_w_bf16, pool_scale.reshape(1, BRANCH_DIM))


def _rope_table_kernel(pos_ref, invf_ref, cos_ref, sin_ref):
    ang = pos_ref[...] * invf_ref[...]
    lane = lax.broadcasted_iota(jnp.int32, (1, PAIR), 1)
    rope_lo = (lane % HEAD_DIM) < (HEAD_DIM // 2)
    cos_ref[...] = jnp.cos(ang)
    sin_ref[...] = jnp.where(rope_lo, -jnp.sin(ang), jnp.sin(ang))


def _rope_tables(pos_lanes, inv_freq, tm):
    n = pos_lanes.shape[0]
    blk = pl.BlockSpec((tm, PAIR), lambda i: (i, 0))
    return pl.pallas_call(
        _rope_table_kernel,
        grid=(n // tm,),
        in_specs=[blk, pl.BlockSpec((1, PAIR), lambda i: (0, 0))],
        out_specs=[blk, blk],
        out_shape=[jax.ShapeDtypeStruct((n, PAIR), F32), jax.ShapeDtypeStruct((n, PAIR), F32)],
        compiler_params=_cp("parallel"),
        name="rope_tables",
    )(pos_lanes, inv_freq)


def _attn_kernel(sink_ref, q_ref, kv_ref, kvh_ref, cos_ref, sin_ref, cosh_ref, sinh_ref, qg_ref, kg_ref, o_ref):
    w = WINDOW
    tq = q_ref.shape[1]
    not_first = pl.program_id(1) > 0
    head_avg = _head_block_diag(1.0 / HEAD_DIM)
    lane = lax.broadcasted_iota(jnp.int32, (1, PAIR), 1)
    lane_is_a = lane < HEAD_DIM
    rope_lo = (lane % HEAD_DIM) < (HEAD_DIM // 2)

    def norm_rope(x, cos, sin, gain):
        ms = _dot_split_lhs(x * x, head_avg)
        x = x * lax.rsqrt(ms + NORM_EPS) * gain
        rot = jnp.where(rope_lo, pltpu.roll(x, PAIR - HEAD_DIM // 2, 1), pltpu.roll(x, HEAD_DIM // 2, 1))
        return x * cos + rot * sin

    cos, sin = cos_ref[0], sin_ref[0]
    cos_k = jnp.concatenate([cosh_ref[0], cos], axis=0)
    sin_k = jnp.concatenate([sinh_ref[0], sin], axis=0)
    kv = jnp.concatenate([kvh_ref[0], kv_ref[0]], axis=0)

    qi = lax.broadcasted_iota(jnp.int32, (2 * w, 2 * w), 0) % w
    kj = lax.broadcasted_iota(jnp.int32, (2 * w, 2 * w), 1)
    rel = kj - qi
    band = (rel >= 1) & (rel <= w)
    band_first = band & ((kj >= w) | not_first)
    row_is_a = lax.broadcasted_iota(jnp.int32, (2 * w, 1), 0) < w

    for kb in range(ATTN_KV_DIM // PAIR):
        kn = norm_rope(kv[:, kb * PAIR:(kb + 1) * PAIR], cos_k, sin_k, kg_ref[...])
        vv = kv[:, ATTN_KV_DIM + kb * PAIR:ATTN_KV_DIM + (kb + 1) * PAIR]
        kn_sw = pltpu.roll(kn, HEAD_DIM, 1)
        vv_sw = pltpu.roll(vv, HEAD_DIM, 1)
        for half in range(2):
            g = 2 * kb + half
            if half == 0:
                k2 = jnp.where(lane_is_a, kn, kn_sw)
                v2 = jnp.where(lane_is_a, vv, vv_sw)
            else:
                k2 = jnp.where(lane_is_a, kn_sw, kn)
                v2 = jnp.where(lane_is_a, vv_sw, vv)
            k2 = k2.astype(BF16)
            v2 = v2.astype(BF16)
            for jp in range(2):
                qb = 2 * g + jp
                qn = norm_rope(q_ref[0, :, qb * PAIR:(qb + 1) * PAIR], cos, sin, qg_ref[...])
                qn = qn * (HEAD_DIM ** -0.5)
                sink = jnp.where(row_is_a, sink_ref[2 * qb], sink_ref[2 * qb + 1])
                for sb in range(tq // w):
                    qs = _stack_heads(qn[sb * w:(sb + 1) * w], lane_is_a).astype(BF16)
                    sc = _dot_nt(qs, k2[sb * w:(sb + 2) * w])
                    sc = jnp.where(band_first if sb == 0 else band, sc, -jnp.inf)
                    mx = jnp.maximum(jnp.max(sc, axis=-1, keepdims=True), sink)
                    e = jnp.exp(sc - mx)
                    inv_den = 1.0 / (jnp.sum(e, axis=-1, keepdims=True) + jnp.exp(sink - mx))
                    o2 = _dot((e * inv_den).astype(BF16), v2[sb * w:(sb + 2) * w])
                    o_ref[0, sb * w:(sb + 1) * w, qb * PAIR:(qb + 1) * PAIR] = (
                        jnp.where(lane_is_a, o2[0:w], o2[w:]).astype(o_ref.dtype))


def _attn(proj3, cos_tab, sin_tab, q_gain, k_gain, sinks, tq):
    b, s, _ = proj3.shape
    w = WINDOW
    kvw = 2 * ATTN_KV_DIM
    hb = tq // w
    cur = lambda width, col: pl.BlockSpec((1, tq, width), lambda bi, c: (bi, c, col))
    halo = lambda width, col: pl.BlockSpec((1, w, width), lambda bi, c: (bi, jnp.maximum(c * hb - 1, 0), col))
    vec = pl.BlockSpec((1, PAIR), lambda bi, c: (0, 0))
    return pl.pallas_call(
        _attn_kernel,
        grid=(b, s // tq),
        in_specs=[
            pl.BlockSpec(memory_space=pltpu.SMEM),
            cur(BRANCH_DIM, COL_Q // BRANCH_DIM), cur(kvw, COL_KV // kvw), halo(kvw, COL_KV // kvw),
            cur(PAIR, 0), cur(PAIR, 0), halo(PAIR, 0), halo(PAIR, 0), vec, vec,
        ],
        out_specs=pl.BlockSpec((1, tq, BRANCH_DIM), lambda bi, c: (bi, c, 0)),
        out_shape=jax.ShapeDtypeStruct((b, s, BRANCH_DIM), BF16),
        compiler_params=_cp("parallel", "arbitrary"),
        name="attn",
    )(sinks, proj3, proj3, proj3, cos_tab, sin_tab, cos_tab, sin_tab, q_gain, k_gain)


def _merge_kernel(a_ref, b_ref, c_ref, g0_ref, g1_ref, g2_ref, w_ref, o_ref):
    acc = _sigmoid(g0_ref[...]) * _dot(a_ref[...], w_ref[0])
    acc = acc + _sigmoid(g1_ref[...]) * _dot(b_ref[...], w_ref[1])
    acc = acc + _sigmoid(g2_ref[...]) * _dot(c_ref[...], w_ref[2])
    o_ref[...] = acc.astype(o_ref.dtype)


def _merge(a_out, b_out, c_out, proj, w_branch_bf16, tm, tn):
    n = proj.shape[0]
    br = lambda: pl.BlockSpec((tm, BRANCH_DIM), lambda i, j: (i, 0))
    gate = lambda g: pl.BlockSpec((tm, tn), lambda i, j: (i, (COL_G + g * D_MODEL) // tn + j))
    return pl.pallas_call(
        _merge_kernel,
        grid=(n // tm, D_MODEL // tn),
        in_specs=[br(), br(), br(), gate(0), gate(1), gate(2),
                  pl.BlockSpec((N_BRANCHES, BRANCH_DIM, tn), lambda i, j: (0, 0, j))],
        out_specs=pl.BlockSpec((tm, tn), lambda i, j: (i, j)),
        out_shape=jax.ShapeDtypeStruct((n, D_MODEL), BF16),
        compiler_params=_cp("parallel", "arbitrary"),
        name="merge",
    )(a_out, b_out, c_out, proj, proj, proj, w_branch_bf16)


def _out_proj_kernel(m_ref, x_ref, w_ref, g_ref, wr_ref, x1_ref, hn_ref, lg_ref):
    x1 = x_ref[...] + _dot(m_ref[...], w_ref[...])
    x1_ref[...] = x1
    ms = jnp.mean(x1 * x1, axis=-1, keepdims=True)
    hn = x1 * lax.rsqrt(ms + NORM_EPS) * g_ref[...]
    hn_ref[...] = hn
    hh, hl = _split_bf16(hn)
    wh, wl = _split_bf16(wr_ref[...])
    lg_ref[...] = _dot_nt(wh, hh) + _dot_nt(wh, hl) + _dot_nt(wl, hh)


def _out_proj(merged, x2d, w_out_bf16, ffn_gain, w_router_t, tm):
    n, d = x2d.shape
    return pl.pallas_call(
        _out_proj_kernel,
        grid=(n // tm,),
        in_specs=[
            pl.BlockSpec((tm, d), lambda i: (i, 0)),
            pl.BlockSpec((tm, d), lambda i: (i, 0)),
            pl.BlockSpec((d, d), lambda i: (0, 0)),
            pl.BlockSpec((1, d), lambda i: (0, 0)),
            pl.BlockSpec((ROUTER_ROWS, d), lambda i: (0, 0)),
        ],
        out_specs=[
            pl.BlockSpec((tm, d), lambda i: (i, 0)),
            pl.BlockSpec((tm, d), lambda i: (i, 0)),
            pl.BlockSpec((ROUTER_ROWS, tm), lambda i: (0, i)),
        ],
        out_shape=[
            jax.ShapeDtypeStruct((n, d), F32),
            jax.ShapeDtypeStruct((n, d), F32),
            jax.ShapeDtypeStruct((ROUTER_ROWS, n), F32),
        ],
        compiler_params=_cp("parallel"),
        name="out_proj",
    )(merged, x2d, w_out_bf16, ffn_gain.reshape(1, d), w_router_t)


def _experts_kernel(be_ref, nu_ref, x_ref, w1_ref, w3_ref, w2_ref, o_ref, w1b_ref, w3b_ref, w2b_ref):
    i = pl.program_id(0)
    new_expert = (i == 0) | (be_ref[i] != be_ref[jnp.maximum(i - 1, 0)])

    @pl.when(new_expert)
    def _():
        w1b_ref[...] = w1_ref[0, 0].astype(BF16)
        w3b_ref[...] = w3_ref[0, 0].astype(BF16)
        w2b_ref[...] = w2_ref[0, 0].astype(BF16)

    used = i < nu_ref[0]

    @pl.when(used)
    def _():
        xb = x_ref[...].astype(BF16)
        h1 = _dot(xb, w1b_ref[...])
        h3 = _dot(xb, w3b_ref[...])
        act = (h1 * _sigmoid(h1)) * h3
        o_ref[...] = _dot(act.astype(BF16), w2b_ref[...])

    @pl.when(jnp.logical_not(used))
    def _():
        o_ref[...] = jnp.zeros_like(o_ref)


def _experts(xg, block_expert, n_used, w1, w3, w2, layer):
    n_rows, d = xg.shape
    ff = w1.shape[-1]
    blk = MOE_BLOCK
    return pl.pallas_call(
        _experts_kernel,
        grid_spec=pltpu.PrefetchScalarGridSpec(
            num_scalar_prefetch=2,
            grid=(n_rows // blk,),
            in_specs=[
                pl.BlockSpec((blk, d), lambda i, be, nu: (jnp.minimum(i, nu[0] - 1), 0)),
                pl.BlockSpec((1, 1, d, ff), lambda i, be, nu: (layer, be[i], 0, 0)),
                pl.BlockSpec((1, 1, d, ff), lambda i, be, nu: (layer, be[i], 0, 0)),
                pl.BlockSpec((1, 1, ff, d), lambda i, be, nu: (layer, be[i], 0, 0)),
            ],
            out_specs=pl.BlockSpec((blk, d), lambda i, be, nu: (i, 0)),
            scratch_shapes=[pltpu.VMEM((d, ff), BF16), pltpu.VMEM((d, ff), BF16), pltpu.VMEM((ff, d), BF16)],
        ),
        out_shape=jax.ShapeDtypeStruct((n_rows, d), F32),
        compiler_params=_cp("arbitrary"),
        name="experts",
    )(block_expert, n_used, xg, w1, w3, w2)


def _pack_segments():
    c = BRANCH_DIM
    o_ad, o_gd = 3 * c + DECAY_LORA, 3 * c + DECAY_LORA + ICLR_LORA
    o_u = o_gd + GATE_LORA
    o_q = o_u + c
    o_k = o_q + c
    o_g = o_k + 2 * ATTN_KV_DIM
    segs = [(0, 0, LANES),
            (3 * c, 3 * c, DECAY_LORA),
            (3 * c + LORA_PAD, o_ad, ICLR_LORA),
            (3 * c + 2 * LORA_PAD, o_gd, LANES),
            (COL_U, o_u, LANES), (COL_KV, o_k, LANES), (COL_Q, o_q, LANES), (COL_G, o_g, LANES)]
    return [(p // LANES, src, keep) for p, src, keep in segs]


def _pack_w_in_kernel(a_ref, b_ref, o_ref):
    blk = pl.program_id(0)
    lane = lax.broadcasted_iota(jnp.int32, (1, LANES), 1)
    segs = _pack_segments()
    for si, (first, src, keep) in enumerate(segs):
        last = segs[si + 1][0] if si + 1 < len(segs) else P_COLS // LANES
        shift = (-src) % LANES

        @pl.when((blk >= first) & (blk < last))
        def _(shift=shift, keep=keep):
            if shift == 0:
                x = a_ref[0]
            else:
                x = jnp.where(lane < shift, pltpu.roll(a_ref[0], shift, 1), pltpu.roll(b_ref[0], shift, 1))
            if keep < LANES:
                x = jnp.where(lane < keep, x, 0.0)
            o_ref[...] = x.astype(o_ref.dtype)


def _pack_w_in(w, layer):
    _, d, cols = w.shape
    segs = _pack_segments()
    last_src = (cols - 1) // LANES

    def src_block(blk):
        q = blk
        for first, src, _ in segs:
            q = jnp.where(blk >= first, blk - first + src // LANES, q)
        return q

    return pl.pallas_call(
        _pack_w_in_kernel,
        grid=(P_COLS // LANES,),
        in_specs=[pl.BlockSpec((1, d, LANES), lambda i: (layer, 0, src_block(i))),
                  pl.BlockSpec((1, d, LANES), lambda i: (layer, 0, jnp.minimum(src_block(i) + 1, last_src)))],
        out_specs=pl.BlockSpec((d, LANES), lambda i: (0, i)),
        out_shape=jax.ShapeDtypeStruct((d, P_COLS), BF16),
        compiler_params=_cp("parallel"),
        name="pack_w_in",
    )(w, w)


def _pack_mu(mu):
    z32 = jnp.zeros((LORA_PAD - DECAY_LORA,), mu.dtype)
    c = BRANCH_DIM
    o_wd, o_ad, o_gd = 3 * c, 3 * c + DECAY_LORA, 3 * c + DECAY_LORA + ICLR_LORA
    return jnp.concatenate([mu[:o_wd], mu[o_wd:o_ad], z32, mu[o_ad:o_gd], z32, mu[o_gd:]]).reshape(1, Z_COLS)


def _pad_rows(w, rows):
    return jnp.concatenate([w, jnp.zeros((rows - w.shape[0], w.shape[1]), w.dtype)], axis=0)


def _router_weights(w_grp, w_exp):
    d = w_grp.shape[0]
    pad = jnp.zeros((d, 8 - N_GROUPS), w_grp.dtype)
    return jnp.concatenate([w_grp, pad, w_exp], axis=1).T


def _route_kernel(lg_ref, bias_ref, ids_ref, gcol_ref, cnt_ref, carry_ref):
    tm = lg_ref.shape[1]

    @pl.when(pl.program_id(0) == 0)
    def _():
        carry_ref[...] = jnp.zeros_like(carry_ref)

    lg = lg_ref[...] + bias_ref[...]
    row8 = lax.broadcasted_iota(jnp.int32, (8, tm), 0)
    row8f = row8.astype(F32)
    neg_inf = -jnp.inf

    def first_argmax(x):
        mx = jnp.max(x, axis=0, keepdims=True)
        return mx, jnp.min(jnp.where(x == mx, row8f, 8.0), axis=0, keepdims=True).astype(jnp.int32)

    grp_logits = jnp.where(row8 < N_GROUPS, lg[0:8], neg_inf)
    gmax, grp = first_argmax(grp_logits)
    p_grp = 1.0 / jnp.sum(jnp.exp(grp_logits - gmax), axis=0, keepdims=True)
    in_grp = lg[8:16]
    for g in range(1, N_GROUPS):
        in_grp = jnp.where(grp == g, lg[8 + 8 * g:16 + 8 * g], in_grp)
    m1, i1 = first_argmax(in_grp)
    rest = jnp.where(row8 == i1, neg_inf, in_grp)
    m2, i2 = first_argmax(rest)
    e2 = jnp.exp(m2 - m1)
    gate1 = p_grp / (1.0 + e2)
    gate2 = p_grp * e2 / (1.0 + e2)
    exp1 = grp * EXPERTS_PER_GROUP + i1
    exp2 = grp * EXPERTS_PER_GROUP + i2

    rows = lax.broadcasted_iota(jnp.int32, (N_EXPERTS, tm), 0)
    hot1 = (rows == exp1).astype(F32)
    hot2 = (rows == exp2).astype(F32)
    cnt = (hot1 + hot2).astype(BF16)
    src = lax.broadcasted_iota(jnp.int32, (tm, tm), 0)
    dst = lax.broadcasted_iota(jnp.int32, (tm, tm), 1)
    before = _dot(cnt, (src < dst).astype(BF16)) + carry_ref[...]
    rank1 = jnp.sum(hot1 * before, axis=0, keepdims=True).astype(jnp.int32)
    rank2 = jnp.sum(hot2 * before, axis=0, keepdims=True).astype(jnp.int32)
    carry_ref[...] += _dot(cnt, jnp.ones((tm, tm), BF16))
    cnt_ref[...] = carry_ref[...]

    ids_ref[...] = jnp.where(row8 == 0, exp1, jnp.where(row8 == 1, exp2,
                             jnp.where(row8 == 2, rank1, jnp.where(row8 == 3, rank2, 0))))
    row128 = lax.broadcasted_iota(jnp.int32, (LANES, tm), 0)
    gates_t = jnp.where(row128 == 0, gate1, jnp.where(row128 == 1, gate2, 0.0))
    gcol_ref[...] = gates_t.T


def _route(logits_t, bias_col, tm):
    n = logits_t.shape[1]
    bias = jnp.broadcast_to(bias_col, (ROUTER_ROWS, tm))
    return pl.pallas_call(
        _route_kernel,
        grid=(n // tm,),
        in_specs=[pl.BlockSpec((ROUTER_ROWS, tm), lambda i: (0, i)),
                  pl.BlockSpec((ROUTER_ROWS, tm), lambda i: (0, 0))],
        out_specs=[pl.BlockSpec((8, tm), lambda i: (0, i)),
                   pl.BlockSpec((tm, LANES), lambda i: (i, 0)),
                   pl.BlockSpec((N_EXPERTS, tm), lambda i: (0, 0))],
        out_shape=[jax.ShapeDtypeStruct((8, n), jnp.int32),
                   jax.ShapeDtypeStruct((n, LANES), F32),
                   jax.ShapeDtypeStruct((N_EXPERTS, tm), F32)],
        scratch_shapes=[pltpu.VMEM((N_EXPERTS, tm), F32)],
        compiler_params=_cp("arbitrary"),
        name="route",
    )(logits_t, bias)


def _row_copy(src_ref, src_row, dst_ref, dst_row, sem):
    return pltpu.make_async_copy(src_ref.at[pl.ds(src_row, 1)], dst_ref.at[pl.ds(dst_row, 1)], sem)


def _scatter_kernel(dest_ref, pad_start_ref, pad_len_ref, hn_ref, xg_ref, zero_ref, tile_ref, tile_sems, sems,
                    pad_sem, *, tm):
    n = hn_ref.shape[0]
    n_steps = pl.num_programs(0)
    step = pl.program_id(0)
    slot = step % 2
    base = step * tm

    def tile_load(for_step, into):
        return pltpu.make_async_copy(hn_ref.at[pl.ds(for_step * tm, tm)], tile_ref.at[into], tile_sems.at[into])

    @pl.when(step == 0)
    def _():
        tile_load(step, slot).start()

    @pl.when(pl.program_id(0) == 0)
    def _():
        zero_ref[...] = jnp.zeros_like(zero_ref)

        def each_pad_row(fn):
            def per_expert(e, carry):
                lax.fori_loop(0, pad_len_ref[e], lambda j, c: fn(pad_start_ref[e] + j, c), 0)
                return carry
            lax.fori_loop(0, pad_len_ref.shape[0], per_expert, 0)

        def start(row, carry):
            _row_copy(zero_ref, 0, xg_ref, row, pad_sem).start()
            return carry

        def wait(row, carry):
            _row_copy(zero_ref, 0, xg_ref, row, pad_sem).wait()
            return carry

        each_pad_row(start)
        each_pad_row(wait)

    def wait_rows(sl):
        for _ in range(TOP_K):
            pltpu.make_async_copy(tile_ref.at[sl], xg_ref.at[pl.ds(0, tm)], sems.at[sl]).wait()

    tile_load(step, slot).wait()

    @pl.when(step > 0)
    def _():
        wait_rows(1 - slot)

    @pl.when(step + 1 < n_steps)
    def _():
        tile_load(step + 1, 1 - slot).start()

    def issue(grp, carry):
        row0 = pl.multiple_of(grp * ISSUE_UNROLL, ISSUE_UNROLL)
        for j in range(ISSUE_UNROLL):
            for k in range(TOP_K):
                _row_copy(tile_ref.at[slot], row0 + j, xg_ref, dest_ref[k * n + base + row0 + j],
                          sems.at[slot]).start()
        return carry

    lax.fori_loop(0, tm // ISSUE_UNROLL, issue, 0)

    @pl.when(step == n_steps - 1)
    def _():
        wait_rows(slot)


def _scatter(dest_flat, pad_start, pad_len, hn, n_rows, tm):
    n, d = hn.shape
    return pl.pallas_call(
        functools.partial(_scatter_kernel, tm=tm),
        grid_spec=pltpu.PrefetchScalarGridSpec(
            num_scalar_prefetch=3,
            grid=(n // tm,),
            in_specs=[pl.BlockSpec(memory_space=pl.ANY)],
            out_specs=pl.BlockSpec(memory_space=pl.ANY),
            scratch_shapes=[pltpu.VMEM((8, d), hn.dtype), pltpu.VMEM((2, tm, d), hn.dtype),
                            pltpu.SemaphoreType.DMA((2,)), pltpu.SemaphoreType.DMA((2,)), pltpu.SemaphoreType.DMA],
        ),
        out_shape=jax.ShapeDtypeStruct((n_rows, d), hn.dtype),
        compiler_params=_cp("arbitrary"),
        name="moe_scatter",
    )(dest_flat, pad_start, pad_len, hn)


def _combine_kernel(dest_ref, x_ref, gcol_ref, yg_ref, o_ref, buf_ref, sems):
    tm = x_ref.shape[0]
    n_steps = pl.num_programs(0)
    n = n_steps * tm
    step = pl.program_id(0)
    slot = step % 2

    def gather(for_step, into):
        base = for_step * tm

        def issue(grp, carry):
            row0 = pl.multiple_of(grp * ISSUE_UNROLL, ISSUE_UNROLL)
            for j in range(ISSUE_UNROLL):
                for k in range(TOP_K):
                    _row_copy(yg_ref, dest_ref[k * n + base + row0 + j], buf_ref.at[into, k], row0 + j,
                              sems.at[into]).start()
            return carry

        lax.fori_loop(0, tm // ISSUE_UNROLL, issue, 0)

    @pl.when(step == 0)
    def _():
        gather(step, slot)

    @pl.when(step + 1 < n_steps)
    def _():
        gather(step + 1, 1 - slot)

    for k in range(TOP_K):
        pltpu.make_async_copy(yg_ref.at[pl.ds(0, tm)], buf_ref.at[slot, k], sems.at[slot]).wait()
    g = gcol_ref[...]
    o_ref[...] = x_ref[...] + g[:, 0:1] * buf_ref[slot, 0] + g[:, 1:2] * buf_ref[slot, 1]


def _combine(dest_flat, x1, gcol, yg, tm):
    n, d = x1.shape
    return pl.pallas_call(
        _combine_kernel,
        grid_spec=pltpu.PrefetchScalarGridSpec(
            num_scalar_prefetch=1,
            grid=(n // tm,),
            in_specs=[pl.BlockSpec((tm, d), lambda i, dest: (i, 0)),
                      pl.BlockSpec((tm, LANES), lambda i, dest: (i, 0)),
                      pl.BlockSpec(memory_space=pl.ANY)],
            out_specs=pl.BlockSpec((tm, d), lambda i, dest: (i, 0)),
            scratch_shapes=[pltpu.VMEM((2, TOP_K, tm, d), F32), pltpu.SemaphoreType.DMA((2,))],
        ),
        out_shape=jax.ShapeDtypeStruct((n, d), F32),
        compiler_params=_cp("arbitrary"),
        name="moe_combine",
    )(dest_flat, x1, gcol, yg)


def _moe(x1, hn, logits_t, bias_col, w1, w3, w2, layer, tm):
    n, d = hn.shape
    blk = MOE_BLOCK
    ids, gcol, cnt = _route(logits_t, bias_col, tm)
    counts = cnt[:, 0].astype(jnp.int32)
    padded = (counts + blk - 1) // blk * blk
    pends = jnp.cumsum(padded)
    pstarts = pends - padded
    experts = jnp.arange(N_EXPERTS, dtype=jnp.int32)
    seg_start = jnp.sum(jnp.where(ids[0:TOP_K, :, None] == experts, pstarts, 0), axis=-1)
    dest_flat = (seg_start + ids[TOP_K:2 * TOP_K]).reshape(-1)
    n_rows = n * TOP_K + N_EXPERTS * blk
    block_start = jnp.arange(n_rows // blk, dtype=jnp.int32) * blk
    block_expert = jnp.minimum(jnp.sum((block_start[:, None] >= pends[None, :]).astype(jnp.int32), axis=1),
                               N_EXPERTS - 1)
    n_used = (pends[N_EXPERTS - 1:] // blk).astype(jnp.int32)
    pad_start = jnp.concatenate([pstarts + counts, pends[N_EXPERTS - 1:]])
    pad_len = jnp.concatenate([padded - counts, n_rows - pends[N_EXPERTS - 1:]])
    xg = _scatter(dest_flat, pad_start, pad_len, hn, n_rows, tm)
    yg = _experts(xg, block_expert, n_used, w1, w3, w2, layer)
    return _combine(dest_flat, x1, gcol, yg, tm)


def _pick(n, pref):
    t = pref
    while n % t:
        t //= 2
    return t


def _tiles(n, s):
    return dict(
        proj_rows=_pick(n, 1024), proj_cols=1024,
        rwkv_rows=_pick(s, 512),
        pool_rows=_pick(s, 512),
        attn_rows=_pick(s, 256),
        merge_rows=_pick(n, 1024), merge_cols=512,
        out_rows=_pick(n, 512),
        moe_rows=_pick(n, 512),
        rope_rows=_pick(n, 2048),
    )


def _layer(x2d, b, s, cos_tab, sin_tab, p, t):
    n = x2d.shape[0]
    proj = _norm_matmul(x2d, p['attn_norm'], p['w_in'], t['proj_rows'], t['proj_cols'])
    proj3 = proj.reshape(b, s, P_COLS)
    a_out = _rwkv(proj3, p['mu'], p['w0'], p['a0'], p['k_k'], p['k_a'], p['r_k'], p['ln_g'], p['ln_b'],
                  p['w_up'], p['a_up'], p['g_up'], t['rwkv_rows'])
    b_out = _pool(proj3, p['pool_w'], p['pool_scale'], t['pool_rows'])
    c_out = _attn(proj3, cos_tab, sin_tab, p['q_gain'], p['k_gain'], p['sinks'], t['attn_rows'])
    merged = _merge(a_out.reshape(n, -1), b_out.reshape(n, -1), c_out.reshape(n, -1), proj,
                    p['w_branch'], t['merge_rows'], t['merge_cols'])
    x1, hn, logits_t = _out_proj(merged, x2d, p['w_out'], p['ffn_norm'], p['w_router_t'], t['out_rows'])
    return _moe(x1, hn, logits_t, p['router_bias'], p['w1'], p['w3'], p['w2'], p['layer'], t['moe_rows'])


def kernel(x, positions, attn_norm, w_in, tmix_mu, rwkv_w0, rwkv_w_up, rwkv_a0, rwkv_a_up, rwkv_g_up, rwkv_k_k, rwkv_k_a, rwkv_r_k, rwkv_ln_g, rwkv_ln_b, pool_w, pool_scale, q_norm, k_norm, attn_sinks, w_branch, w_out, ffn_norm, router_grp_w, router_grp_b, router_exp_w, router_exp_b, expert_w1, expert_w3, expert_w2):
    b, s, d = x.shape
    n = b * s
    half = HEAD_DIM // 2
    inv_freq = ROPE_THETA ** (-jnp.arange(half, dtype=F32) / half)
    inv_freq = jnp.tile(inv_freq, PAIR // half).reshape(1, PAIR)
    pos_lanes = jnp.broadcast_to(positions.astype(F32).reshape(n, 1), (n, PAIR))
    tiles = _tiles(n, s)
    cos_tab, sin_tab = _rope_tables(pos_lanes, inv_freq, tiles['rope_rows'])
    cos_tab = cos_tab.reshape(b, s, PAIR)
    sin_tab = sin_tab.reshape(b, s, PAIR)
    x2d = x.reshape(n, d)
    for l in range(w_in.shape[0]):
        p = {
            'attn_norm': attn_norm[l],
            'layer': l,
            'w_in': _pack_w_in(w_in, l),
            'mu': _pack_mu(tmix_mu[l]),
            'w0': rwkv_w0[l], 'a0': rwkv_a0[l], 'k_k': rwkv_k_k[l], 'k_a': rwkv_k_a[l],
            'r_k': rwkv_r_k[l].reshape(-1), 'ln_g': rwkv_ln_g[l], 'ln_b': rwkv_ln_b[l],
            'w_up': _pad_rows(rwkv_w_up[l], LORA_PAD), 'a_up': _pad_rows(rwkv_a_up[l], LORA_PAD),
            'g_up': rwkv_g_up[l],
            'pool_w': pool_w[l].astype(BF16), 'pool_scale': pool_scale[l],
            'q_gain': jnp.tile(q_norm[l], PAIR // HEAD_DIM).reshape(1, PAIR),
            'k_gain': jnp.tile(k_norm[l], PAIR // HEAD_DIM).reshape(1, PAIR),
            'sinks': attn_sinks[l],
            'w_branch': w_branch[l].astype(BF16), 'w_out': w_out[l].astype(BF16),
            'ffn_norm': ffn_norm[l],
            'w_router_t': _router_weights(router_grp_w[l], router_exp_w[l]),
            'router_bias': jnp.concatenate([router_grp_b[l], jnp.zeros((8 - N_GROUPS,), F32),
                                            router_exp_b[l]]).reshape(ROUTER_ROWS, 1),
            'w1': expert_w1, 'w3': expert_w3, 'w2': expert_w2,
        }
        x2d = _layer(x2d, b, s, cos_tab, sin_tab, p, tiles)
    return x2d.reshape(b, s, d)
```
